```python
import math
import jax
import jax.numpy as jnp
from jax import lax
import numpy as np

D_MODEL = 2048
BATCH = 4
SEQ = 2048
DEPTH = 2
DEC_BATCH = 16
DEC_SEQ = 64
PAST_LEN = 4096

CHUNK = 64
Q_BLOCK = 128
HEAD_DIM = 128
H_A = 8
D_HALF = HEAD_DIM // 2
H_B = 4
H_C = 4
C_BAND_CHUNKS = 8
C_WIN = C_BAND_CHUNKS * CHUNK
MAX_REL = 128
N_REL = 2 * MAX_REL + 1
W_A = H_A * HEAD_DIM
W_B = H_B * HEAD_DIM
W_C = H_C * HEAD_DIM
W_MIX = W_A + W_B + W_C
N_BRANCH = 3
N_IN = 3 * W_MIX + N_BRANCH * D_MODEL
ROPE_THETA = 10000.0
N_GROUPS = 4
E_PER_GROUP = 4
N_EXP = N_GROUPS * E_PER_GROUP
TOP_K_IN_GROUP = 2
D_EXP = 512
EPS = 1e-6
F32 = jnp.float32

kernel_name = 'hybrid_stream_diff_sb_chunkband_hmoe_step'


def rms_norm(x, g):
    x32 = x.astype(F32)
    y = x32 * lax.rsqrt(jnp.mean(x32 * x32, axis=-1, keepdims=True) + EPS)
    return y.astype(x.dtype) * g


def rope(x, pos):
    d = x.shape[-1]
    half = d // 2
    inv = ROPE_THETA ** (-2.0 * jnp.arange(half, dtype=F32) / d)
    ang = pos.astype(F32)[:, None] * inv[None, :]
    cos = jnp.cos(ang)[:, None, :]
    sin = jnp.sin(ang)[:, None, :]
    x32 = x.astype(F32)
    x1, x2 = x32[..., :half], x32[..., half:]
    return jnp.concatenate([x1 * cos - x2 * sin, x2 * cos + x1 * sin], axis=-1).astype(x.dtype)


def lambda_init(layer_idx):
    return 0.8 - 0.6 * math.exp(-0.3 * layer_idx)


def diff_lambda(lq1, lk1, lq2, lk2, layer_idx):
    a = jnp.exp(jnp.sum(lq1.astype(F32) * lk1.astype(F32)))
    b = jnp.exp(jnp.sum(lq2.astype(F32) * lk2.astype(F32)))
    return a - b + lambda_init(layer_idx)


def project(x, pos, norm_g, w_in_l, qn_a, kn_a, qn_c, kn_c):
    B, L, _ = x.shape
    z = rms_norm(x, norm_g) @ w_in_l
    bounds = [W_A, 2 * W_A, 3 * W_A,
              3 * W_A + W_B, 3 * W_A + 2 * W_B, 3 * W_A + 3 * W_B,
              3 * W_A + 3 * W_B + W_C, 3 * W_A + 3 * W_B + 2 * W_C, 3 * W_MIX]
    q_a, k_a, v_a, q_b, k_b, v_b, q_c, k_c, v_c, gates = jnp.split(z, bounds, axis=-1)
    q_a = rope(rms_norm(q_a.reshape(B, L, 2 * H_A, D_HALF), qn_a), pos).reshape(B, L, H_A, HEAD_DIM)
    k_a = rope(rms_norm(k_a.reshape(B, L, 2 * H_A, D_HALF), kn_a), pos).reshape(B, L, H_A, HEAD_DIM)
    v_a = v_a.reshape(B, L, H_A, HEAD_DIM)
    q_b = q_b.reshape(B, L, H_B, HEAD_DIM)
    k_b = k_b.reshape(B, L, H_B, HEAD_DIM)
    v_b = v_b.reshape(B, L, H_B, HEAD_DIM)
    q_c = rms_norm(q_c.reshape(B, L, H_C, HEAD_DIM), qn_c)
    k_c = rms_norm(k_c.reshape(B, L, H_C, HEAD_DIM), kn_c)
    v_c = v_c.reshape(B, L, H_C, HEAD_DIM)
    return q_a, k_a, v_a, q_b, k_b, v_b, q_c, k_c, v_c, gates


def diff_core(q, k, v, lam, mask):
    B, Lq, H, _ = q.shape
    Lk = k.shape[1]
    q = q.reshape(B, Lq, H, 2, D_HALF)
    k = k.reshape(B, Lk, H, 2, D_HALF)
    s = jnp.einsum('bqhcd,bkhcd->bhcqk', q, k).astype(F32) * (D_HALF ** -0.5)
    if mask is not None:
        s = jnp.where(mask, s, -jnp.inf)
    p = jax.nn.softmax(s, axis=-1)
    a = p[:, :, 0] - lam * p[:, :, 1]
    return jnp.einsum('bhqk,bkhd->bqhd', a.astype(v.dtype), v)


def diff_post(o, subln_g, layer_idx):
    return rms_norm(o, subln_g) * (1.0 - lambda_init(layer_idx))


def sb_core(q, k, v, q_pos, k_pos):
    z = jnp.einsum('bqhd,bkhd->bhqk', q, k).astype(F32) * (HEAD_DIM ** -0.5)
    strict = k_pos[None, :] < q_pos[:, None]
    log_stay = jnp.where(strict, jax.nn.log_sigmoid(-z), 0.0)
    log_between = lax.cumsum(log_stay, axis=3, reverse=True) - log_stay
    w = jnp.where(strict, jnp.exp(jax.nn.log_sigmoid(z) + log_between), 0.0)
    return jnp.einsum('bhqk,bkhd->bqhd', w.astype(v.dtype), v)


def sweep_query_blocks(q, fn):
    B, S, H, Dh = q.shape
    nb = S // Q_BLOCK
    qb = jnp.moveaxis(q.reshape(B, nb, Q_BLOCK, H, Dh), 1, 0)

    def body(args):
        qi, bi = args
        return fn(qi, bi * Q_BLOCK + jnp.arange(Q_BLOCK))

    o = lax.map(body, (qb, jnp.arange(nb)))
    return jnp.moveaxis(o, 0, 1).reshape(B, S, H, o.shape[-1])


def rel_bias(table, dist):
    return table[:, jnp.clip(dist, -MAX_REL, MAX_REL) + MAX_REL]


def band_core(q, k, v, bias, valid):
    s = jnp.einsum('...qhd,...khd->...hqk', q, k).astype(F32) * (HEAD_DIM ** -0.5) + bias.astype(F32)
    if valid is not None:
        s = jnp.where(valid, s, -jnp.inf)
    p = jax.nn.softmax(s, axis=-1)
    return jnp.einsum('...hqk,...khd->...qhd', p.astype(v.dtype), v)


def chunk_attn_prompt(q, k, v, table):
    B, S, H, Dh = q.shape
    nc = S // CHUNK
    band = C_BAND_CHUNKS + 1

    def chunks(t):
        return t.reshape(B, nc, CHUNK, H, Dh)

    def gather_band(t):
        tp = jnp.pad(chunks(t), ((0, 0), (C_BAND_CHUNKS, 0), (0, 0), (0, 0), (0, 0)))
        return jnp.stack([tp[:, o:o + nc] for o in range(band)], axis=2).reshape(B, nc, band * CHUNK, H, Dh)

    kb, vb = gather_band(k), gather_band(v)
    j = jnp.arange(band * CHUNK)
    i = jnp.arange(CHUNK)
    bias = rel_bias(table, i[:, None] + C_WIN - j[None, :])
    valid = (jnp.arange(nc)[:, None] + j[None, :] // CHUNK) >= C_BAND_CHUNKS
    o = band_core(chunks(q), kb, vb, bias, valid[:, None, None, :])
    return o.reshape(B, S, H, Dh)


def chunk_attn_sample(q, k_all, v_all, table):
    L = q.shape[1]
    Lk = k_all.shape[1]
    w_rows = Lk - L
    dist = jnp.arange(L)[:, None] + w_rows - jnp.arange(Lk)[None, :]
    return band_core(q, k_all, v_all, rel_bias(table, dist), None)


def merge_out(x, o_a, o_b, o_c, gates, b_gate_l, w_branch_l, w_out_l):
    B, L, _ = x.shape
    g = jax.nn.sigmoid(gates + b_gate_l).reshape(B, L, N_BRANCH, D_MODEL)
    y_a = o_a.reshape(B, L, W_A) @ w_branch_l[:W_A]
    y_b = o_b.reshape(B, L, W_B) @ w_branch_l[W_A:W_A + W_B]
    y_c = o_c.reshape(B, L, W_C) @ w_branch_l[W_A + W_B:]
    m = g[:, :, 0] * y_a + g[:, :, 1] * y_b + g[:, :, 2] * y_c
    return x + m @ w_out_l


def hier_moe(x, w_group, b_group, w_er, b_er, w_g, w_u, w_d):
    B, L, D = x.shape
    t = x.reshape(-1, D)
    g_logits = (t @ w_group).astype(F32) + b_group.astype(F32)
    g_prob = jax.nn.softmax(g_logits, axis=-1)
    _, g_top = lax.top_k(g_logits, 1)
    g_idx = g_top[:, 0]
    p_group = jnp.take_along_axis(g_prob, g_idx[:, None], axis=1)
    e_all = jnp.einsum('nd,gde->nge', t, w_er).astype(F32) + b_er.astype(F32)
    e_logits = jnp.take_along_axis(e_all, g_idx[:, None, None], axis=1)[:, 0]
    top_v, top_i = lax.top_k(e_logits, TOP_K_IN_GROUP)
    combine = p_group * jax.nn.softmax(top_v, axis=-1)
    ids = g_idx[:, None] * E_PER_GROUP + top_i
    c = jnp.sum(jax.nn.one_hot(ids, N_EXP, dtype=F32) * combine[..., None], axis=1)
    h = jax.nn.silu(jnp.einsum('nd,edf->nef', t, w_g)) * jnp.einsum('nd,edf->nef', t, w_u)
    h = h * c[..., None].astype(h.dtype)
    y = jnp.einsum('nef,efd->nd', h, w_d)
    return y.reshape(B, L, D)


def moe_residual(x, norm_g, w_group, b_group, w_er, b_er, w_g, w_u, w_d):
    return x + hier_moe(rms_norm(x, norm_g), w_group, b_group, w_er, b_er, w_g, w_u, w_d)


def setup_inputs(seed: int = 0) -> dict:
    key = jax.random.key(seed)
    keys = jax.random.split(key, 40)
    counter = [0]

    def nrm(shape, scale):
        k = keys[counter[0]]
        counter[0] += 1
        return jax.random.normal(k, shape, F32) * scale

    def gain(shape):
        return 1.0 + nrm(shape, 0.02)

    c_rows = min(C_WIN, PAST_LEN)
    return {
        'x_prompt': nrm((BATCH, SEQ, D_MODEL), 1.0),
        'x_sample': nrm((DEC_BATCH, DEC_SEQ, D_MODEL), 1.0),
        'cache_a_k': nrm((DEPTH, DEC_BATCH, PAST_LEN, H_A, HEAD_DIM), 1.0),
        'cache_a_v': nrm((DEPTH, DEC_BATCH, PAST_LEN, H_A, HEAD_DIM), 1.0),
        'cache_b_k': nrm((DEPTH, DEC_BATCH, PAST_LEN, H_B, HEAD_DIM), 1.0),
        'cache_b_v': nrm((DEPTH, DEC_BATCH, PAST_LEN, H_B, HEAD_DIM), 1.0),
        'cache_c_k': nrm((DEPTH, DEC_BATCH, c_rows, H_C, HEAD_DIM), 1.0),
        'cache_c_v': nrm((DEPTH, DEC_BATCH, c_rows, H_C, HEAD_DIM), 1.0),
        'norm_mix': gain((DEPTH, D_MODEL)),
        'w_in': nrm((DEPTH, D_MODEL, N_IN), D_MODEL ** -0.5),
        'b_gate': nrm((DEPTH, N_BRANCH * D_MODEL), 0.01),
        'q_norm_a': gain((DEPTH, D_HALF)),
        'k_norm_a': gain((DEPTH, D_HALF)),
        'lam_q1': nrm((DEPTH, D_HALF), 0.1),
        'lam_k1': nrm((DEPTH, D_HALF), 0.1),
        'lam_q2': nrm((DEPTH, D_HALF), 0.1),
        'lam_k2': nrm((DEPTH, D_HALF), 0.1),
        'subln_a': gain((DEPTH, HEAD_DIM)),
        'q_norm_c': gain((DEPTH, HEAD_DIM)),
        'k_norm_c': gain((DEPTH, HEAD_DIM)),
        'rel_bias_c': nrm((DEPTH, H_C, N_REL), 0.1),
        'w_branch': jnp.concatenate([nrm((DEPTH, W_A, D_MODEL), W_A ** -0.5),
                                     nrm((DEPTH, W_B, D_MODEL), W_B ** -0.5),
                                     nrm((DEPTH, W_C, D_MODEL), W_C ** -0.5)], axis=1),
        'w_out': nrm((DEPTH, D_MODEL, D_MODEL), D_MODEL ** -0.5),
        'norm_ffn': gain((DEPTH, D_MODEL)),
        'w_group': nrm((DEPTH, D_MODEL, N_GROUPS), D_MODEL ** -0.5),
        'b_group': nrm((DEPTH, N_GROUPS), 0.01),
        'w_expert_router': nrm((DEPTH, N_GROUPS, D_MODEL, E_PER_GROUP), D_MODEL ** -0.5),
        'b_expert_router': nrm((DEPTH, N_GROUPS, E_PER_GROUP), 0.01),
        'w_gate_e': nrm((DEPTH, N_EXP, D_MODEL, D_EXP), D_MODEL ** -0.5),
        'w_up_e': nrm((DEPTH, N_EXP, D_MODEL, D_EXP), D_MODEL ** -0.5),
        'w_down_e': nrm((DEPTH, N_EXP, D_EXP, D_MODEL), D_EXP ** -0.5),
    }


def reference(x_prompt, x_sample, cache_a_k, cache_a_v, cache_b_k, cache_b_v, cache_c_k, cache_c_v,
              norm_mix, w_in, b_gate, q_norm_a, k_norm_a, lam_q1, lam_k1, lam_q2, lam_k2, subln_a,
              q_norm_c, k_norm_c, rel_bias_c, w_branch, w_out, norm_ffn, w_group, b_group,
              w_expert_router, b_expert_router, w_gate_e, w_up_e, w_down_e):
    xp, xs = x_prompt, x_sample
    S = xp.shape[1]
    L = xs.shape[1]
    P = cache_a_k.shape[2]
    pos_p = jnp.arange(S)
    pos_s = P + jnp.arange(L)
    k_pos_p = jnp.arange(S)
    k_chunk_p = k_pos_p // CHUNK
    k_pos_s = jnp.arange(P + L)
    c_rows = min(C_WIN, S)
    p_new, s_new = [], []
    for l in range(DEPTH):
        lam = diff_lambda(lam_q1[l], lam_k1[l], lam_q2[l], lam_k2[l], l)
        proj_w = (norm_mix[l], w_in[l], q_norm_a[l], k_norm_a[l], q_norm_c[l], k_norm_c[l])
        merge_w = (b_gate[l], w_branch[l], w_out[l])
        moe_w = (norm_ffn[l], w_group[l], b_group[l], w_expert_router[l], b_expert_router[l],
                 w_gate_e[l], w_up_e[l], w_down_e[l])

        qa, ka, va, qb, kb, vb, qc, kc, vc, gp = project(xp, pos_p, *proj_w)
        o_a = sweep_query_blocks(
            qa, lambda qi, qpos: diff_core(qi, ka, va, lam,
                                           k_chunk_p[None, :] <= (qpos // CHUNK)[:, None]))
        o_b = sweep_query_blocks(qb, lambda qi, qpos: sb_core(qi, kb, vb, qpos, k_pos_p))
        o_c = chunk_attn_prompt(qc, kc, vc, rel_bias_c[l])
        xp = merge_out(xp, diff_post(o_a, subln_a[l], l), o_b, o_c, gp, *merge_w)
        xp = moe_residual(xp, *moe_w)
        p_new.append((ka, va, kb, vb, kc[:, S - c_rows:], vc[:, S - c_rows:]))

        qa, ka, va, qb, kb, vb, qc, kc, vc, gs = project(xs, pos_s, *proj_w)
        o_a = diff_core(qa, jnp.concatenate([cache_a_k[l], ka], axis=1),
                        jnp.concatenate([cache_a_v[l], va], axis=1), lam, None)
        o_b = sb_core(qb, jnp.concatenate([cache_b_k[l], kb], axis=1),
                      jnp.concatenate([cache_b_v[l], vb], axis=1), pos_s, k_pos_s)
        o_c = chunk_attn_sample(qc, jnp.concatenate([cache_c_k[l], kc], axis=1),
                                jnp.concatenate([cache_c_v[l], vc], axis=1), rel_bias_c[l])
        xs = merge_out(xs, diff_post(o_a, subln_a[l], l), o_b, o_c, gs, *merge_w)
        xs = moe_residual(xs, *moe_w)
        s_new.append((ka, va, kb, vb, kc, vc))

    pak, pav, pbk, pbv, pck, pcv = [jnp.stack(t) for t in zip(*p_new)]
    sak, sav, sbk, sbv, sck, scv = [jnp.stack(t) for t in zip(*s_new)]
    return (xp, xs, pak, pav, pbk, pbv, pck, pcv, sak, sav, sbk, sbv, sck, scv)
```

```python
import functools
import math

import jax
import jax.numpy as jnp
from jax import lax
from jax.experimental import pallas as pl
from jax.experimental.pallas import tpu as pltpu

F32 = jnp.float32
BF16 = jnp.bfloat16

CHUNK = 64
HEAD_DIM = 128
D_HALF = HEAD_DIM // 2
C_BAND_CHUNKS = 8
C_WIN = C_BAND_CHUNKS * CHUNK
MAX_REL = 128
N_REL = 2 * MAX_REL + 1
ROPE_THETA = 10000.0
N_GROUPS = 4
E_PER_GROUP = 4
N_EXP = N_GROUPS * E_PER_GROUP
EPS = 1e-6
N_BRANCH = 3

LANES = 128
VMEM_LIMIT = 56 * 1024 * 1024

PROJ_TM = 512
PROJ_TN = 512
ATTN_TQ = 256
SAMPLE_TK = 512
SB_SUB = 256
MERGE_TM = 256
MOE_TM = 512


def _lambda_init(layer_idx):
    return 0.8 - 0.6 * math.exp(-0.3 * layer_idx)


def _dot(a, b):
    return jnp.dot(a, b, preferred_element_type=F32)


def _dot_nt(a, b):
    return lax.dot_general(a, b, (((1,), (1,)), ((), ())), preferred_element_type=F32)


def _split_bf16(x):
    hi = x.astype(BF16)
    lo = (x - hi.astype(F32)).astype(BF16)
    return hi, lo


def _dot_f32acc(x, w_bf16):
    hi, lo = _split_bf16(x)
    return _dot(hi, w_bf16) + _dot(lo, w_bf16)


def _softplus(z):
    return jnp.maximum(z, 0.0) + jnp.log(1.0 + jnp.exp(-jnp.abs(z)))


def _params(sem, vmem=VMEM_LIMIT):
    return pltpu.CompilerParams(dimension_semantics=sem, vmem_limit_bytes=vmem)


def _bias_kernel(tab_ref, far_ref, own_ref):
    r = pl.program_id(0)
    qi = lax.broadcasted_iota(jnp.int32, (CHUNK, C_WIN), 0)
    kj = lax.broadcasted_iota(jnp.int32, (CHUNK, C_WIN), 1)
    idx_far = jnp.clip(qi + C_WIN - kj, -MAX_REL, MAX_REL) + MAX_REL
    qo = lax.broadcasted_iota(jnp.int32, (CHUNK, CHUNK), 0)
    ko = lax.broadcasted_iota(jnp.int32, (CHUNK, CHUNK), 1)
    idx_own = jnp.clip(qo - ko, -MAX_REL, MAX_REL) + MAX_REL

    def body(t, carry):
        af, ao = carry
        val = tab_ref[r, t]
        return jnp.where(idx_far == t, val, af), jnp.where(idx_own == t, val, ao)

    af, ao = lax.fori_loop(0, N_REL, body,
                           (jnp.zeros((CHUNK, C_WIN), F32), jnp.zeros((CHUNK, CHUNK), F32)))
    far_ref[0] = af
    own_ref[0] = ao


def _expand_bias(table):
    rows = table.shape[0]
    return pl.pallas_call(
        _bias_kernel,
        grid=(rows,),
        in_specs=[pl.BlockSpec(memory_space=pltpu.SMEM)],
        out_specs=[pl.BlockSpec((1, CHUNK, C_WIN), lambda r: (r, 0, 0)),
                   pl.BlockSpec((1, CHUNK, CHUNK), lambda r: (r, 0, 0))],
        out_shape=[jax.ShapeDtypeStruct((rows, CHUNK, C_WIN), F32),
                   jax.ShapeDtypeStruct((rows, CHUNK, CHUNK), F32)],
        compiler_params=_params(("arbitrary",)),
        name="bias_expand",
    )(table)


def _proj_kernel(x_ref, ng_ref, w_ref, bg_ref, qna_ref, kna_ref, qnc_ref, knc_ref, cos_ref, sin_ref,
                 qa_ref, ka_ref, va_ref, qb_ref, kb_ref, vb_ref, qc_ref, kc_ref, vc_ref, g_ref,
                 xn_ref, *, bounds):
    j = pl.program_id(1)

    @pl.when(j == 0)
    def _():
        x = x_ref[...]
        ms = jnp.mean(x * x, axis=-1, keepdims=True)
        xn_ref[...] = (x * lax.rsqrt(ms + EPS) * ng_ref[...]).astype(BF16)

    z = _dot(xn_ref[...], w_ref[...])
    tm, tn = z.shape
    heads = tn // HEAD_DIM

    r_i = lax.broadcasted_iota(jnp.int32, (HEAD_DIM, HEAD_DIM), 0)
    c_i = lax.broadcasted_iota(jnp.int32, (HEAD_DIM, HEAD_DIM), 1)
    ones_map = ((r_i // D_HALF) == (c_i // D_HALF)).astype(BF16)
    ones_head = jnp.ones((HEAD_DIM, HEAD_DIM), BF16)
    lane = lax.broadcasted_iota(jnp.int32, (tm, HEAD_DIM), 1)
    first_half = (lane % D_HALF) < (D_HALF // 2)

    def map_norm_rope(zh, gain):
        ms = _dot_f32acc(zh * zh, ones_map) * (1.0 / D_HALF)
        y = zh * lax.rsqrt(ms + EPS) * gain
        partner = jnp.where(first_half,
                            pltpu.roll(y, HEAD_DIM - D_HALF // 2, 1),
                            pltpu.roll(y, D_HALF // 2, 1))
        return y * cos_ref[...] + partner * sin_ref[...]

    def head_norm(zh, gain):
        ms = _dot_f32acc(zh * zh, ones_head) * (1.0 / HEAD_DIM)
        return zh * lax.rsqrt(ms + EPS) * gain

    def per_head(fn, out_ref, scale=None):
        for hh in range(heads):
            sl = slice(hh * HEAD_DIM, (hh + 1) * HEAD_DIM)
            y = fn(z[:, sl])
            if scale is not None:
                y = y * scale
            out_ref[:, sl] = y.astype(out_ref.dtype)

    def in_range(name):
        lo, hi = bounds[name]
        return jnp.logical_and(j >= lo, j < hi)

    @pl.when(in_range("qa"))
    def _():
        per_head(lambda zh: map_norm_rope(zh, qna_ref[...]), qa_ref, scale=D_HALF ** -0.5)

    @pl.when(in_range("ka"))
    def _():
        per_head(lambda zh: map_norm_rope(zh, kna_ref[...]), ka_ref)

    @pl.when(in_range("va"))
    def _():
        va_ref[...] = z

    @pl.when(in_range("qb"))
    def _():
        qb_ref[...] = z.astype(BF16)

    @pl.when(in_range("kb"))
    def _():
        kb_ref[...] = z

    @pl.when(in_range("vb"))
    def _():
        vb_ref[...] = z

    @pl.when(in_range("qc"))
    def _():
        per_head(lambda zh: head_norm(zh, qnc_ref[...]), qc_ref)

    @pl.when(in_range("kc"))
    def _():
        per_head(lambda zh: head_norm(zh, knc_ref[...]), kc_ref)

    @pl.when(in_range("vc"))
    def _():
        vc_ref[...] = z

    @pl.when(in_range("g"))
    def _():
        g_ref[...] = jax.nn.sigmoid(z + bg_ref[...]).astype(BF16)


def _proj(x, norm_g, w16, b_gate, qn_a, kn_a, qn_c, kn_c, cos_tab, sin_tab, *, w_a, w_b, w_c, d_model,
          prompt_tiles, tiles_per_seq):
    n = x.shape[0]
    tm, tn = PROJ_TM, PROJ_TN
    n_in = w16.shape[1]
    assert n % tm == 0 and n_in % tn == 0 and w_a % tn == 0 and w_b == tn and w_c == tn
    na = w_a // tn
    names = ["qa", "ka", "va", "qb", "kb", "vb", "qc", "kc", "vc", "g"]
    widths = [na, na, na, 1, 1, 1, 1, 1, 1, N_BRANCH * d_model // tn]
    bounds, start = {}, 0
    for name, wd in zip(names, widths):
        bounds[name] = (start, start + wd)
        start += wd
    assert start == n_in // tn

    def col_map(name):
        lo, hi = bounds[name]
        return lambda i, j: (i, jnp.clip(j - lo, 0, hi - lo - 1))

    def tab_map(i, j):
        return (jnp.where(i < prompt_tiles, i % tiles_per_seq, tiles_per_seq), 0)

    g_lo, g_hi = bounds["g"]
    vec = lambda width: pl.BlockSpec((1, width), lambda i, j: (0, 0))
    in_specs = [
        pl.BlockSpec((tm, d_model), lambda i, j: (i, 0)),
        vec(d_model),
        pl.BlockSpec((d_model, tn), lambda i, j: (0, j)),
        pl.BlockSpec((1, tn), lambda i, j: (0, jnp.clip(j - g_lo, 0, g_hi - g_lo - 1))),
        vec(HEAD_DIM), vec(HEAD_DIM), vec(HEAD_DIM), vec(HEAD_DIM),
        pl.BlockSpec((tm, HEAD_DIM), tab_map),
        pl.BlockSpec((tm, HEAD_DIM), tab_map),
    ]
    out_widths = [w_a, w_a, w_a, w_b, w_b, w_b, w_c, w_c, w_c, N_BRANCH * d_model]
    out_dtypes = [BF16, F32, F32, BF16, F32, F32, BF16, F32, F32, BF16]
    out_specs = [pl.BlockSpec((tm, tn), col_map(name)) for name in names]
    out_shape = [jax.ShapeDtypeStruct((n, wd), dt) for wd, dt in zip(out_widths, out_dtypes)]
    tile2 = lambda v: jnp.concatenate([v, v]).reshape(1, HEAD_DIM)
    return pl.pallas_call(
        functools.partial(_proj_kernel, bounds=bounds),
        grid=(n // tm, n_in // tn),
        in_specs=in_specs,
        out_specs=out_specs,
        out_shape=out_shape,
        scratch_shapes=[pltpu.VMEM((tm, d_model), BF16)],
        compiler_params=_params(("arbitrary", "arbitrary")),
        name="proj",
    )(x, norm_g.reshape(1, -1), w16, b_gate.reshape(1, -1), tile2(qn_a), tile2(kn_a),
      qn_c.reshape(1, -1), kn_c.reshape(1, -1), cos_tab, sin_tab)


def _stack_maps(q):
    lane = lax.broadcasted_iota(jnp.int32, q.shape, 1)
    zero = jnp.zeros_like(q)
    return jnp.concatenate([jnp.where(lane < D_HALF, q, zero), jnp.where(lane >= D_HALF, q, zero)], axis=0)


def _online_update(s, vblk, m, l, acc):
    m_new = jnp.maximum(m, jnp.max(s, axis=1, keepdims=True))
    p = jnp.exp(s - m_new)
    alpha = jnp.exp(m - m_new)
    l = alpha * l + jnp.sum(p, axis=1, keepdims=True)
    acc = alpha * acc + _dot(p.astype(BF16), vblk)
    return m_new, l, acc


def _diff_lambda(lq1_ref, lk1_ref, lq2_ref, lk2_ref, lam_init):
    a = jnp.exp(jnp.sum(lq1_ref[...] * lk1_ref[...], axis=1, keepdims=True))
    b = jnp.exp(jnp.sum(lq2_ref[...] * lk2_ref[...], axis=1, keepdims=True))
    return a - b + lam_init


def _diff_finish(m, l, acc, t, lam, subln, lam_init):
    o1 = acc[:t] / l[:t]
    o2 = acc[t:] / l[t:]
    o = o1 - lam * o2
    ms = jnp.mean(o * o, axis=-1, keepdims=True)
    return o * lax.rsqrt(ms + EPS) * subln * (1.0 - lam_init)


def _attn_a_prompt_kernel(q_ref, k_ref, v_ref, lq1_ref, lk1_ref, lq2_ref, lk2_ref, sg_ref, o_ref, *,
                          tq, lam_init):
    qi = pl.program_id(2)
    qq = _stack_maps(q_ref[...])

    def block(kb, carry, masked):
        start = pl.multiple_of(kb * tq, tq)
        kblk = k_ref[pl.ds(start, tq), :].astype(BF16)
        vblk = v_ref[pl.ds(start, tq), :].astype(BF16)
        s = _dot_nt(qq, kblk)
        if masked:
            row = lax.broadcasted_iota(jnp.int32, s.shape, 0)
            col = lax.broadcasted_iota(jnp.int32, s.shape, 1)
            s = jnp.where((col // CHUNK) <= ((row % tq) // CHUNK), s, -jnp.inf)
        return _online_update(s, vblk, *carry)

    init = (jnp.full((2 * tq, 1), -jnp.inf, F32), jnp.zeros((2 * tq, 1), F32),
            jnp.zeros((2 * tq, HEAD_DIM), F32))
    carry = lax.fori_loop(0, qi, lambda kb, c: block(kb, c, False), init)
    m, l, acc = block(qi, carry, True)
    lam = _diff_lambda(lq1_ref, lk1_ref, lq2_ref, lk2_ref, lam_init)
    o_ref[...] = _diff_finish(m, l, acc, tq, lam, sg_ref[...], lam_init).astype(o_ref.dtype)


def _lam_specs(nd):
    zeros = (0,) * nd
    spec = pl.BlockSpec((1, D_HALF), lambda *_: (0, 0))
    return [spec, spec, spec, spec, pl.BlockSpec((1, HEAD_DIM), lambda *_: (0, 0))]


def _lam_args(lq1, lk1, lq2, lk2, subln):
    return (lq1.reshape(1, -1), lk1.reshape(1, -1), lq2.reshape(1, -1), lk2.reshape(1, -1),
            subln.reshape(1, -1))


def _attn_a_prompt(q16, k32, v32, lam_args, *, batch, seq, heads, layer_idx):
    n = q16.shape[0]
    tq = ATTN_TQ
    assert seq % tq == 0 and tq % CHUNK == 0
    nq = seq // tq
    return pl.pallas_call(
        functools.partial(_attn_a_prompt_kernel, tq=tq, lam_init=_lambda_init(layer_idx)),
        grid=(batch, heads, nq),
        in_specs=[pl.BlockSpec((tq, HEAD_DIM), lambda b, h, qi: (b * nq + qi, h)),
                  pl.BlockSpec((seq, HEAD_DIM), lambda b, h, qi: (b, h)),
                  pl.BlockSpec((seq, HEAD_DIM), lambda b, h, qi: (b, h))] + _lam_specs(3),
        out_specs=pl.BlockSpec((tq, HEAD_DIM), lambda b, h, qi: (b * nq + qi, h)),
        out_shape=jax.ShapeDtypeStruct((n, heads * HEAD_DIM), BF16),
        compiler_params=_params(("arbitrary", "arbitrary", "arbitrary")),
        name="attn_a_prompt",
    )(q16, k32, v32, *lam_args)


def _attn_a_sample_kernel(q_ref, ck_ref, cv_ref, kn_ref, vn_ref, lq1_ref, lk1_ref, lq2_ref, lk2_ref, sg_ref,
                          oprev_ref, o_ref, qq_ref, m_ref, l_ref, acc_ref, *, heads, new_len, lam_init):
    del oprev_ref
    kt = pl.program_id(1)
    nkt = pl.num_programs(1)

    @pl.when(kt == 0)
    def _():
        for h in range(heads):
            qq_ref[h] = _stack_maps(q_ref[:, h * HEAD_DIM:(h + 1) * HEAD_DIM])
        m_ref[...] = jnp.full(m_ref.shape, -jnp.inf, F32)
        l_ref[...] = jnp.zeros(l_ref.shape, F32)
        acc_ref[...] = jnp.zeros(acc_ref.shape, F32)

    def update(h, kblk, vblk):
        s = _dot_nt(qq_ref[h], kblk.astype(BF16))
        m, l, acc = _online_update(s, vblk.astype(BF16), m_ref[h], l_ref[h], acc_ref[h])
        m_ref[h] = m
        l_ref[h] = l
        acc_ref[h] = acc

    for h in range(heads):
        sl = slice(h * HEAD_DIM, (h + 1) * HEAD_DIM)
        update(h, ck_ref[:, sl], cv_ref[:, sl])

    @pl.when(kt == nkt - 1)
    def _():
        lam = _diff_lambda(lq1_ref, lk1_ref, lq2_ref, lk2_ref, lam_init)
        for h in range(heads):
            sl = slice(h * HEAD_DIM, (h + 1) * HEAD_DIM)
            update(h, kn_ref[:, sl], vn_ref[:, sl])
            y = _diff_finish(m_ref[h], l_ref[h], acc_ref[h], new_len, lam, sg_ref[...], lam_init)
            o_ref[:, sl] = y.astype(o_ref.dtype)


def _attn_a_sample(o_prev, q16, cache_k, cache_v, k32, v32, lam_args, *, layer, streams, new_len, heads,
                   row0_blocks):
    past = cache_k.shape[2]
    tk = SAMPLE_TK
    assert past % tk == 0
    width = heads * HEAD_DIM
    new_map = lambda b, kt: (row0_blocks + b, 0)
    cache_map = lambda b, kt: (layer, b, kt, 0)
    return pl.pallas_call(
        functools.partial(_attn_a_sample_kernel, heads=heads, new_len=new_len,
                          lam_init=_lambda_init(layer)),
        grid=(streams, past // tk),
        in_specs=[pl.BlockSpec((new_len, width), new_map),
                  pl.BlockSpec((None, None, tk, width), cache_map),
                  pl.BlockSpec((None, None, tk, width), cache_map),
                  pl.BlockSpec((new_len, width), new_map),
                  pl.BlockSpec((new_len, width), new_map)] + _lam_specs(2)
                 + [pl.BlockSpec(memory_space=pl.ANY)],
        out_specs=pl.BlockSpec((new_len, width), new_map),
        out_shape=jax.ShapeDtypeStruct(o_prev.shape, o_prev.dtype),
        scratch_shapes=[pltpu.VMEM((heads, 2 * new_len, HEAD_DIM), BF16),
                        pltpu.VMEM((heads, 2 * new_len, 1), F32),
                        pltpu.VMEM((heads, 2 * new_len, 1), F32),
                        pltpu.VMEM((heads, 2 * new_len, HEAD_DIM), F32)],
        input_output_aliases={10: 0},
        compiler_params=_params(("arbitrary", "arbitrary")),
        name="attn_a_sample",
    )(q16, cache_k, cache_v, k32, v32, *lam_args, o_prev)


def _upper_ones(t):
    r = lax.broadcasted_iota(jnp.int32, (t, t), 0)
    c = lax.broadcasted_iota(jnp.int32, (t, t), 1)
    return (r > c).astype(BF16)


def _sb_block(q, kblk, vblk, ones_u, c, acc, scale, strict):
    z = _dot_nt(q, kblk) * scale
    sp = _softplus(z)
    log_stay = -sp
    if strict is not None:
        log_stay = jnp.where(strict, log_stay, 0.0)
    between = _dot_f32acc(log_stay, ones_u)
    w = jnp.exp((z - sp) + between + c)
    if strict is not None:
        w = jnp.where(strict, w, 0.0)
    acc = acc + _dot(w.astype(BF16), vblk)
    c = c + jnp.sum(log_stay, axis=1, keepdims=True)
    return c, acc


def _attn_b_prompt_kernel(q_ref, k_ref, v_ref, o_ref, *, tq):
    qi = pl.program_id(2)
    q = q_ref[...]
    scale = HEAD_DIM ** -0.5
    ones_u = _upper_ones(tq)
    row = lax.broadcasted_iota(jnp.int32, (tq, tq), 0)
    col = lax.broadcasted_iota(jnp.int32, (tq, tq), 1)

    def block(kb, carry, strict):
        start = pl.multiple_of(kb * tq, tq)
        kblk = k_ref[pl.ds(start, tq), :].astype(BF16)
        vblk = v_ref[pl.ds(start, tq), :].astype(BF16)
        return _sb_block(q, kblk, vblk, ones_u, *carry, scale, strict)

    carry = block(qi, (jnp.zeros((tq, 1), F32), jnp.zeros((tq, HEAD_DIM), F32)), col < row)
    _, acc = lax.fori_loop(0, qi, lambda t, cr: block(qi - 1 - t, cr, None), carry)
    o_ref[...] = acc.astype(o_ref.dtype)


def _attn_b_prompt(q16, k32, v32, *, batch, seq, heads):
    n = q16.shape[0]
    tq = ATTN_TQ
    nq = seq // tq
    return pl.pallas_call(
        functools.partial(_attn_b_prompt_kernel, tq=tq),
        grid=(batch, heads, nq),
        in_specs=[pl.BlockSpec((tq, HEAD_DIM), lambda b, h, qi: (b * nq + qi, h)),
                  pl.BlockSpec((seq, HEAD_DIM), lambda b, h, qi: (b, h)),
                  pl.BlockSpec((seq, HEAD_DIM), lambda b, h, qi: (b, h))],
        out_specs=pl.BlockSpec((tq, HEAD_DIM), lambda b, h, qi: (b * nq + qi, h)),
        out_shape=jax.ShapeDtypeStruct((n, heads * HEAD_DIM), BF16),
        compiler_params=_params(("arbitrary", "arbitrary", "arbitrary")),
        name="attn_b_prompt",
    )(q16, k32, v32)


def _attn_b_sample_kernel(q_ref, ck_ref, cv_ref, kn_ref, vn_ref, oprev_ref, o_ref, c_ref, acc_ref, *,
                          heads, new_len, sub):
    del oprev_ref
    kt = pl.program_id(1)
    nkt = pl.num_programs(1)
    scale = HEAD_DIM ** -0.5
    tk = ck_ref.shape[0]

    @pl.when(kt == 0)
    def _():
        ones_new = _upper_ones(new_len)
        row = lax.broadcasted_iota(jnp.int32, (new_len, new_len), 0)
        col = lax.broadcasted_iota(jnp.int32, (new_len, new_len), 1)
        for h in range(heads):
            sl = slice(h * HEAD_DIM, (h + 1) * HEAD_DIM)
            c, acc = _sb_block(q_ref[:, sl], kn_ref[:, sl].astype(BF16), vn_ref[:, sl].astype(BF16), ones_new,
                               jnp.zeros((new_len, 1), F32), jnp.zeros((new_len, HEAD_DIM), F32),
                               scale, col < row)
            c_ref[h] = c
            acc_ref[h] = acc

    ones_u = _upper_ones(sub)
    for sb in reversed(range(tk // sub)):
        rows = slice(sb * sub, (sb + 1) * sub)
        for h in range(heads):
            sl = slice(h * HEAD_DIM, (h + 1) * HEAD_DIM)
            c, acc = _sb_block(q_ref[:, sl], ck_ref[rows, sl].astype(BF16), cv_ref[rows, sl].astype(BF16),
                               ones_u, c_ref[h], acc_ref[h], scale, None)
            c_ref[h] = c
            acc_ref[h] = acc

    @pl.when(kt == nkt - 1)
    def _():
        for h in range(heads):
            o_ref[:, h * HEAD_DIM:(h + 1) * HEAD_DIM] = acc_ref[h].astype(o_ref.dtype)


def _attn_b_sample(o_prev, q16, cache_k, cache_v, k32, v32, *, layer, streams, new_len, heads, row0_blocks):
    past = cache_k.shape[2]
    tk = SAMPLE_TK
    nkt = past // tk
    assert past % tk == 0 and tk % SB_SUB == 0
    width = heads * HEAD_DIM
    new_map = lambda b, kt: (row0_blocks + b, 0)
    cache_map = lambda b, kt: (layer, b, nkt - 1 - kt, 0)
    return pl.pallas_call(
        functools.partial(_attn_b_sample_kernel, heads=heads, new_len=new_len, sub=SB_SUB),
        grid=(streams, nkt),
        in_specs=[pl.BlockSpec((new_len, width), new_map),
                  pl.BlockSpec((None, None, tk, width), cache_map),
                  pl.BlockSpec((None, None, tk, width), cache_map),
                  pl.BlockSpec((new_len, width), new_map),
                  pl.BlockSpec((new_len, width), new_map),
                  pl.BlockSpec(memory_space=pl.ANY)],
        out_specs=pl.BlockSpec((new_len, width), new_map),
        out_shape=jax.ShapeDtypeStruct(o_prev.shape, o_prev.dtype),
        scratch_shapes=[pltpu.VMEM((heads, new_len, 1), F32),
                        pltpu.VMEM((heads, new_len, HEAD_DIM), F32)],
        input_output_aliases={5: 0},
        compiler_params=_params(("arbitrary", "arbitrary")),
        name="attn_b_sample",
    )(q16, cache_k, cache_v, k32, v32, o_prev)


def _band_chunk(q, k_far, v_far, k_own, v_own, bias_far, bias_own, far_valid):
    scale = HEAD_DIM ** -0.5
    s_far = _dot_nt(q, k_far) * scale + bias_far
    if far_valid is not None:
        s_far = jnp.where(far_valid, s_far, -jnp.inf)
    s_own = _dot_nt(q, k_own) * scale + bias_own
    m = jnp.maximum(jnp.max(s_far, axis=1, keepdims=True), jnp.max(s_own, axis=1, keepdims=True))
    p_far = jnp.exp(s_far - m)
    p_own = jnp.exp(s_own - m)
    l = jnp.sum(p_far, axis=1, keepdims=True) + jnp.sum(p_own, axis=1, keepdims=True)
    return (_dot(p_far.astype(BF16), v_far) + _dot(p_own.astype(BF16), v_own)) / l


def _attn_c_prompt_kernel(q_ref, k_ref, v_ref, bfar_ref, bown_ref, o_ref, k16_ref, v16_ref, *, seq):
    k16_ref[0:C_WIN, :] = jnp.zeros((C_WIN, HEAD_DIM), BF16)
    v16_ref[0:C_WIN, :] = jnp.zeros((C_WIN, HEAD_DIM), BF16)
    k16_ref[C_WIN:, :] = k_ref[...].astype(BF16)
    v16_ref[C_WIN:, :] = v_ref[...].astype(BF16)
    bias_far = bfar_ref[0]
    bias_own = bown_ref[0]
    key_chunk = lax.broadcasted_iota(jnp.int32, (CHUNK, C_WIN), 1) // CHUNK

    def body(c, _):
        far0 = pl.multiple_of(c * CHUNK, CHUNK)
        own0 = pl.multiple_of(c * CHUNK + C_WIN, CHUNK)
        o = _band_chunk(q_ref[pl.ds(far0, CHUNK), :],
                        k16_ref[pl.ds(far0, C_WIN), :], v16_ref[pl.ds(far0, C_WIN), :],
                        k16_ref[pl.ds(own0, CHUNK), :], v16_ref[pl.ds(own0, CHUNK), :],
                        bias_far, bias_own, key_chunk + c >= C_BAND_CHUNKS)
        o_ref[pl.ds(far0, CHUNK), :] = o.astype(o_ref.dtype)
        return 0

    lax.fori_loop(0, seq // CHUNK, body, 0)


def _attn_c_prompt(q16, k32, v32, bias_far, bias_own, *, layer, batch, seq, heads):
    n = q16.shape[0]
    return pl.pallas_call(
        functools.partial(_attn_c_prompt_kernel, seq=seq),
        grid=(batch, heads),
        in_specs=[pl.BlockSpec((seq, HEAD_DIM), lambda b, h: (b, h)),
                  pl.BlockSpec((seq, HEAD_DIM), lambda b, h: (b, h)),
                  pl.BlockSpec((seq, HEAD_DIM), lambda b, h: (b, h)),
                  pl.BlockSpec((1, CHUNK, C_WIN), lambda b, h: (layer * heads + h, 0, 0)),
                  pl.BlockSpec((1, CHUNK, CHUNK), lambda b, h: (layer * heads + h, 0, 0))],
        out_specs=pl.BlockSpec((seq, HEAD_DIM), lambda b, h: (b, h)),
        out_shape=jax.ShapeDtypeStruct((n, heads * HEAD_DIM), BF16),
        scratch_shapes=[pltpu.VMEM((seq + C_WIN, HEAD_DIM), BF16),
                        pltpu.VMEM((seq + C_WIN, HEAD_DIM), BF16)],
        compiler_params=_params(("arbitrary", "arbitrary")),
        name="attn_c_prompt",
    )(q16, k32, v32, bias_far, bias_own)


def _attn_c_sample_kernel(q_ref, ck_ref, cv_ref, kn_ref, vn_ref, bfar_ref, bown_ref, oprev_ref, o_ref, *,
                          heads):
    del oprev_ref
    for h in range(heads):
        sl = slice(h * HEAD_DIM, (h + 1) * HEAD_DIM)
        o = _band_chunk(q_ref[:, sl], ck_ref[:, sl].astype(BF16), cv_ref[:, sl].astype(BF16),
                        kn_ref[:, sl].astype(BF16), vn_ref[:, sl].astype(BF16),
                        bfar_ref[h], bown_ref[h], None)
        o_ref[:, sl] = o.astype(o_ref.dtype)


def _attn_c_sample(o_prev, q16, cache_k, cache_v, k32, v32, bias_far, bias_own, *, layer, streams, new_len,
                   heads, row0_blocks):
    assert cache_k.shape[2] == C_WIN and new_len == CHUNK
    width = heads * HEAD_DIM
    new_map = lambda b: (row0_blocks + b, 0)
    cache_map = lambda b: (layer, b, 0, 0)
    return pl.pallas_call(
        functools.partial(_attn_c_sample_kernel, heads=heads),
        grid=(streams,),
        in_specs=[pl.BlockSpec((new_len, width), new_map),
                  pl.BlockSpec((None, None, C_WIN, width), cache_map),
                  pl.BlockSpec((None, None, C_WIN, width), cache_map),
                  pl.BlockSpec((new_len, width), new_map),
                  pl.BlockSpec((new_len, width), new_map),
                  pl.BlockSpec((heads, CHUNK, C_WIN), lambda b: (layer, 0, 0)),
                  pl.BlockSpec((heads, CHUNK, CHUNK), lambda b: (layer, 0, 0)),
                  pl.BlockSpec(memory_space=pl.ANY)],
        out_specs=pl.BlockSpec((new_len, width), new_map),
        out_shape=jax.ShapeDtypeStruct(o_prev.shape, o_prev.dtype),
        input_output_aliases={7: 0},
        compiler_params=_params(("arbitrary",)),
        name="attn_c_sample",
    )(q16, cache_k, cache_v, k32, v32, bias_far, bias_own, o_prev)


def _merge_kernel(x_ref, oa_ref, ob_ref, oc_ref, g_ref, wb_ref, wo_ref, o_ref, *, w_a, w_b, d_model):
    ya = _dot(oa_ref[...], wb_ref[0:w_a, :])
    yb = _dot(ob_ref[...], wb_ref[w_a:w_a + w_b, :])
    yc = _dot(oc_ref[...], wb_ref[w_a + w_b:, :])
    m = (g_ref[:, 0:d_model].astype(F32) * ya + g_ref[:, d_model:2 * d_model].astype(F32) * yb
         + g_ref[:, 2 * d_model:].astype(F32) * yc)
    o_ref[...] = x_ref[...] + _dot(m.astype(BF16), wo_ref[...])


def _merge(x, oa, ob, oc, g, wb16, wo16):
    n, d_model = x.shape
    tm = MERGE_TM
    w_a, w_b, w_c = oa.shape[1], ob.shape[1], oc.shape[1]
    rows = lambda width: pl.BlockSpec((tm, width), lambda i: (i, 0))
    resident = lambda shape: pl.BlockSpec(shape, lambda i: (0, 0), pipeline_mode=pl.Buffered(1))
    return pl.pallas_call(
        functools.partial(_merge_kernel, w_a=w_a, w_b=w_b, d_model=d_model),
        grid=(n // tm,),
        in_specs=[rows(d_model), rows(w_a), rows(w_b), rows(w_c), rows(N_BRANCH * d_model),
                  resident(wb16.shape), resident(wo16.shape)],
        out_specs=rows(d_model),
        out_shape=jax.ShapeDtypeStruct((n, d_model), F32),
        compiler_params=_params(("arbitrary",)),
        name="merge",
    )(x, oa, ob, oc, g, wb16, wo16)


def _moe_kernel(x_ref, ng_ref, wr_ref, br_ref, wg_ref, wu_ref, wd_ref, o_ref, xn_ref, cw_ref):
    e = pl.program_id(1)
    tm = x_ref.shape[0]
    lane = lax.broadcasted_iota(jnp.int32, (tm, LANES), 1)
    lane_f = lane.astype(F32)

    @pl.when(e == 0)
    def _():
        x = x_ref[...]
        ms = jnp.mean(x * x, axis=-1, keepdims=True)
        xn = x * lax.rsqrt(ms + EPS) * ng_ref[...]
        xn_ref[...] = xn.astype(BF16)
        o_ref[...] = x

        x_hi, x_lo = _split_bf16(xn)
        w_hi, w_lo = _split_bf16(wr_ref[...])
        logits = _dot(x_hi, w_hi) + _dot(x_lo, w_hi) + _dot(x_hi, w_lo) + br_ref[...]
        is_group = lane < N_GROUPS
        gl = jnp.where(is_group, logits, -jnp.inf)
        g_max = jnp.max(gl, axis=1, keepdims=True)
        g_idx = jnp.min(jnp.where(gl == g_max, lane_f, float(LANES)), axis=1, keepdims=True)
        p_group = 1.0 / jnp.sum(jnp.where(is_group, jnp.exp(logits - g_max), 0.0), axis=1, keepdims=True)
        in_group = jnp.logical_and(lane >= N_GROUPS, lane < N_GROUPS + N_EXP)
        in_group = jnp.logical_and(in_group, ((lane - N_GROUPS) // E_PER_GROUP).astype(F32) == g_idx)
        el = jnp.where(in_group, logits, -jnp.inf)
        v1 = jnp.max(el, axis=1, keepdims=True)
        i1 = jnp.min(jnp.where(el == v1, lane_f, float(LANES)), axis=1, keepdims=True)
        el2 = jnp.where(lane_f == i1, -jnp.inf, el)
        v2 = jnp.max(el2, axis=1, keepdims=True)
        i2 = jnp.min(jnp.where(el2 == v2, lane_f, float(LANES)), axis=1, keepdims=True)
        t = jnp.exp(v2 - v1)
        s1 = 1.0 / (1.0 + t)
        s2 = t / (1.0 + t)
        cw_ref[...] = (jnp.where(lane_f == i1, p_group * s1, 0.0)
                       + jnp.where(lane_f == i2, p_group * s2, 0.0))

    ce = jnp.sum(jnp.where(lane == e + N_GROUPS, cw_ref[...], 0.0), axis=1, keepdims=True)
    xn = xn_ref[...]
    h = jax.nn.silu(_dot(xn, wg_ref[...])) * _dot(xn, wu_ref[...]) * ce
    o_ref[...] += _dot(h.astype(BF16), wd_ref[...])


def _moe(x, norm_g, w_router, b_router, wg16, wu16, wd16):
    n, d_model = x.shape
    tm = MOE_TM
    n_exp, _, d_exp = wg16.shape
    return pl.pallas_call(
        _moe_kernel,
        grid=(n // tm, n_exp),
        in_specs=[pl.BlockSpec((tm, d_model), lambda i, e: (i, 0)),
                  pl.BlockSpec((1, d_model), lambda i, e: (0, 0)),
                  pl.BlockSpec((d_model, LANES), lambda i, e: (0, 0)),
                  pl.BlockSpec((1, LANES), lambda i, e: (0, 0)),
                  pl.BlockSpec((None, d_model, d_exp), lambda i, e: (e, 0, 0)),
                  pl.BlockSpec((None, d_model, d_exp), lambda i, e: (e, 0, 0)),
                  pl.BlockSpec((None, d_exp, d_model), lambda i, e: (e, 0, 0))],
        out_specs=pl.BlockSpec((tm, d_model), lambda i, e: (i, 0)),
        out_shape=jax.ShapeDtypeStruct((n, d_model), F32),
        scratch_shapes=[pltpu.VMEM((tm, d_model), BF16), pltpu.VMEM((tm, LANES), F32)],
        compiler_params=_params(("arbitrary", "arbitrary")),
        name="moe",
    )(x, norm_g.reshape(1, -1), w_router, b_router, wg16, wu16, wd16)


def _rope_tables(seq, past, new_len, tm):
    half = D_HALF // 2
    inv = ROPE_THETA ** (-2.0 * jnp.arange(half, dtype=F32) / D_HALF)
    pos = jnp.concatenate([jnp.arange(seq), jnp.tile(past + jnp.arange(new_len), tm // new_len)])
    ang = pos.astype(F32)[:, None] * inv[None, :]
    cos, sin = jnp.cos(ang), jnp.sin(ang)
    cos_full = jnp.tile(cos, (1, HEAD_DIM // half))
    sin_signed = jnp.tile(jnp.concatenate([-sin, sin], axis=1), (1, HEAD_DIM // D_HALF))
    return cos_full, sin_signed


def kernel(x_prompt, x_sample, cache_a_k, cache_a_v, cache_b_k, cache_b_v, cache_c_k, cache_c_v, norm_mix, w_in, b_gate, q_norm_a, k_norm_a, lam_q1, lam_k1, lam_q2, lam_k2, subln_a, q_norm_c, k_norm_c, rel_bias_c, w_branch, w_out, norm_ffn, w_group, b_group, w_expert_router, b_expert_router, w_gate_e, w_up_e, w_down_e):
    batch, seq, d_model = x_prompt.shape
    streams, new_len, _ = x_sample.shape
    depth, _, past, h_a, _ = cache_a_k.shape
    h_b, h_c = cache_b_k.shape[3], cache_c_k.shape[3]
    w_a, w_b, w_c = h_a * HEAD_DIM, h_b * HEAD_DIM, h_c * HEAD_DIM
    n_prompt, n_sample = batch * seq, streams * new_len
    tm = PROJ_TM
    assert seq % tm == 0 and n_sample % tm == 0 and tm % new_len == 0 and n_prompt % new_len == 0
    c_rows = min(C_WIN, seq)

    x = jnp.concatenate([x_prompt.reshape(n_prompt, d_model), x_sample.reshape(n_sample, d_model)])
    cos_tab, sin_tab = _rope_tables(seq, past, new_len, tm)
    bias_far, bias_own = _expand_bias(rel_bias_c.reshape(depth * h_c, N_REL))
    flat = lambda c: c.reshape(c.shape[0], c.shape[1], c.shape[2], -1)
    ca_k, ca_v, cb_k, cb_v, cc_k, cc_v = map(flat, (cache_a_k, cache_a_v, cache_b_k, cache_b_v,
                                                     cache_c_k, cache_c_v))
    row0 = n_prompt // new_len

    new_kv = []
    for l in range(depth):
        qa, ka, va, qb, kb, vb, qc, kc, vc, g = _proj(
            x, norm_mix[l], w_in[l].astype(BF16), b_gate[l], q_norm_a[l], k_norm_a[l], q_norm_c[l],
            k_norm_c[l], cos_tab, sin_tab, w_a=w_a, w_b=w_b, w_c=w_c, d_model=d_model,
            prompt_tiles=n_prompt // tm, tiles_per_seq=seq // tm)
        lam_args = _lam_args(lam_q1[l], lam_k1[l], lam_q2[l], lam_k2[l], subln_a[l])

        oa = _attn_a_prompt(qa, ka, va, lam_args, batch=batch, seq=seq, heads=h_a, layer_idx=l)
        oa = _attn_a_sample(oa, qa, ca_k, ca_v, ka, va, lam_args, layer=l, streams=streams,
                            new_len=new_len, heads=h_a, row0_blocks=row0)
        ob = _attn_b_prompt(qb, kb, vb, batch=batch, seq=seq, heads=h_b)
        ob = _attn_b_sample(ob, qb, cb_k, cb_v, kb, vb, layer=l, streams=streams, new_len=new_len,
                            heads=h_b, row0_blocks=row0)
        oc = _attn_c_prompt(qc, kc, vc, bias_far, bias_own, layer=l, batch=batch, seq=seq, heads=h_c)
        oc = _attn_c_sample(oc, qc, cc_k, cc_v, kc, vc, bias_far, bias_own, layer=l, streams=streams,
                            new_len=new_len, heads=h_c, row0_blocks=row0)

        x = _merge(x, oa, ob, oc, g, w_branch[l].astype(BF16), w_out[l].astype(BF16))

        w_router = jnp.concatenate(
            [w_group[l], jnp.transpose(w_expert_router[l], (1, 0, 2)).reshape(d_model, N_EXP)], axis=1)
        w_router = jnp.pad(w_router, ((0, 0), (0, LANES - w_router.shape[1])))
        b_router = jnp.pad(jnp.concatenate([b_group[l], b_expert_router[l].reshape(-1)]),
                           (0, LANES - N_GROUPS - N_EXP)).reshape(1, LANES)
        x = _moe(x, norm_ffn[l], w_router, b_router, w_gate_e[l].astype(BF16), w_up_e[l].astype(BF16),
                 w_down_e[l].astype(BF16))
        new_kv.append((ka, va, kb, vb, kc, vc))

    def prompt_part(t, heads, tail=None):
        t = t[:n_prompt].reshape(batch, seq, heads, HEAD_DIM)
        return t if tail is None else t[:, seq - tail:]

    def sample_part(t, heads):
        return t[n_prompt:].reshape(streams, new_len, heads, HEAD_DIM)

    stack = lambda fn, idx, *a: jnp.stack([fn(kv[idx], *a) for kv in new_kv])
    y_prompt = x[:n_prompt].reshape(batch, seq, d_model)
    y_sample = x[n_prompt:].reshape(streams, new_len, d_model)
    return (y_prompt, y_sample,
            stack(prompt_part, 0, h_a), stack(prompt_part, 1, h_a),
            stack(prompt_part, 2, h_b), stack(prompt_part, 3, h_b),
            stack(prompt_part, 4, h_c, c_rows), stack(prompt_part, 5, h_c, c_rows),
            stack(sample_part, 0, h_a), stack(sample_part, 1, h_a),
            stack(sample_part, 2, h_b), stack(sample_part, 3, h_b),
            stack(sample_part, 4, h_c), stack(sample_part, 5, h_c))
```

```python
import functools
import math

import jax
import jax.numpy as jnp
from jax import lax
from jax.experimental import pallas as pl
from jax.experimental.pallas import tpu as pltpu

F32 = jnp.float32
BF16 = jnp.bfloat16

CHUNK = 64
HEAD_DIM = 128
D_HALF = HEAD_DIM // 2
C_BAND_CHUNKS = 8
C_WIN = C_BAND_CHUNKS * CHUNK
MAX_REL = 128
N_REL = 2 * MAX_REL + 1
ROPE_THETA = 10000.0
N_GROUPS = 4
E_PER_GROUP = 4
N_EXP = N_GROUPS * E_PER_GROUP
EPS = 1e-6
N_BRANCH = 3

LANES = 128
VMEM_LIMIT = 56 * 1024 * 1024

PROJ_TM = 512
PROJ_TN = 512
ATTN_A_TQ = 512
ATTN_B_TQ = 512
SB_SUB = 256
SAMPLE_TK = 512
SAMPLE_A_TK = 1024
BAND_GROUP = 4
BAND_ROWS = BAND_GROUP * CHUNK
BAND_KEYS = (BAND_GROUP + C_BAND_CHUNKS) * CHUNK
MERGE_TM = 256
MOE_TM = 512


def _lambda_init(layer_idx):
    return 0.8 - 0.6 * math.exp(-0.3 * layer_idx)


def _dot(a, b):
    return jnp.dot(a, b, preferred_element_type=F32)


def _dot_nt(a, b):
    return lax.dot_general(a, b, (((1,), (1,)), ((), ())), preferred_element_type=F32)


def _split_bf16(x):
    hi = x.astype(BF16)
    lo = (x - hi.astype(F32)).astype(BF16)
    return hi, lo


def _dot_f32acc(x, w_bf16):
    hi, lo = _split_bf16(x)
    return _dot(hi, w_bf16) + _dot(lo, w_bf16)


def _softplus(z):
    return jnp.maximum(z, 0.0) + jnp.log(1.0 + jnp.exp(-jnp.abs(z)))


def _params(sem, vmem=VMEM_LIMIT):
    return pltpu.CompilerParams(dimension_semantics=sem, vmem_limit_bytes=vmem)


def _bias_kernel(tab_ref, far_ref, own_ref, grp_ref):
    r = pl.program_id(0)
    half = BAND_KEYS - 2 * CHUNK

    def rel_index(shape, key_shift):
        qi = lax.broadcasted_iota(jnp.int32, shape, 0)
        kj = lax.broadcasted_iota(jnp.int32, shape, 1) - key_shift
        return jnp.clip(qi + C_WIN - kj, -MAX_REL, MAX_REL) + MAX_REL, kj

    idx_even, kj_even = rel_index((CHUNK, half), 0)
    idx_odd, kj_odd = rel_index((CHUNK, half), CHUNK)

    def body(t, carry):
        val = tab_ref[r, t]
        return tuple(jnp.where(idx == t, val, acc) for idx, acc in zip((idx_even, idx_odd), carry))

    zeros = jnp.zeros((CHUNK, half), F32)
    even, odd = lax.fori_loop(0, N_REL, body, (zeros, zeros))
    far_ref[0] = even[:, :C_WIN]
    own_ref[0] = even[:, C_WIN:C_WIN + CHUNK]
    band = lambda kj: jnp.logical_and(kj >= 0, kj < C_WIN + CHUNK)
    even = jnp.where(band(kj_even), even, -jnp.inf)
    odd = jnp.where(band(kj_odd), odd, -jnp.inf)
    for a in range(BAND_GROUP):
        lead = (a // 2) * 2 * CHUNK
        tail = BAND_KEYS - lead - half
        pieces = [even if a % 2 == 0 else odd]
        if lead:
            pieces.insert(0, jnp.full((CHUNK, lead), -jnp.inf, F32))
        if tail:
            pieces.append(jnp.full((CHUNK, tail), -jnp.inf, F32))
        grp_ref[0, a * CHUNK:(a + 1) * CHUNK, :] = jnp.concatenate(pieces, axis=1)


def _expand_bias(table):
    rows = table.shape[0]
    shapes = [(CHUNK, C_WIN), (CHUNK, CHUNK), (BAND_ROWS, BAND_KEYS)]
    return pl.pallas_call(
        _bias_kernel,
        grid=(rows,),
        in_specs=[pl.BlockSpec(memory_space=pltpu.SMEM)],
        out_specs=[pl.BlockSpec((1,) + s, lambda r: (r, 0, 0)) for s in shapes],
        out_shape=[jax.ShapeDtypeStruct((rows,) + s, F32) for s in shapes],
        compiler_params=_params(("arbitrary",)),
        name="bias_expand",
    )(table)


_PROJ_NAMES = ["qa", "ka", "va", "qb", "kb", "vb", "qc", "kc", "vc", "g"]
_PROJ_KV = [1, 2, 4, 5, 7, 8]


def _proj_kernel(*refs, bounds, n_prev):
    (x_ref, ng_ref, w_ref, bg_ref, qna_ref, kna_ref, qnc_ref, knc_ref, cos_ref, sin_ref) = refs[:10]
    (qa_ref, ka_ref, va_ref, qb_ref, kb_ref, vb_ref, qc_ref, kc_ref, vc_ref, g_ref,
     xn_ref) = refs[10 + n_prev:]
    j = pl.program_id(1)

    @pl.when(j == 0)
    def _():
        x = x_ref[...]
        ms = jnp.mean(x * x, axis=-1, keepdims=True)
        xn_ref[...] = (x * lax.rsqrt(ms + EPS) * ng_ref[...]).astype(BF16)

    z = _dot(xn_ref[...], w_ref[...])
    tm, tn = z.shape
    heads = tn // HEAD_DIM

    r_i = lax.broadcasted_iota(jnp.int32, (HEAD_DIM, HEAD_DIM), 0)
    c_i = lax.broadcasted_iota(jnp.int32, (HEAD_DIM, HEAD_DIM), 1)
    ones_map = ((r_i // D_HALF) == (c_i // D_HALF)).astype(BF16)
    ones_head = jnp.ones((HEAD_DIM, HEAD_DIM), BF16)
    lane = lax.broadcasted_iota(jnp.int32, (tm, HEAD_DIM), 1)
    first_half = (lane % D_HALF) < (D_HALF // 2)

    def map_norm_rope(zh, gain):
        ms = _dot_f32acc(zh * zh, ones_map) * (1.0 / D_HALF)
        y = zh * lax.rsqrt(ms + EPS) * gain
        partner = jnp.where(first_half,
                            pltpu.roll(y, HEAD_DIM - D_HALF // 2, 1),
                            pltpu.roll(y, D_HALF // 2, 1))
        return y * cos_ref[...] + partner * sin_ref[...]

    def head_norm(zh, gain):
        ms = _dot_f32acc(zh * zh, ones_head) * (1.0 / HEAD_DIM)
        return zh * lax.rsqrt(ms + EPS) * gain

    def per_head(fn, out_ref, scale=None):
        for hh in range(heads):
            sl = slice(hh * HEAD_DIM, (hh + 1) * HEAD_DIM)
            y = fn(z[:, sl])
            if scale is not None:
                y = y * scale
            out_ref[:, sl] = y.astype(out_ref.dtype)

    def in_range(name):
        lo, hi = bounds[name]
        return jnp.logical_and(j >= lo, j < hi)

    @pl.when(in_range("qa"))
    def _():
        per_head(lambda zh: map_norm_rope(zh, qna_ref[...]), qa_ref, scale=D_HALF ** -0.5)

    @pl.when(in_range("ka"))
    def _():
        per_head(lambda zh: map_norm_rope(zh, kna_ref[...]), ka_ref)

    @pl.when(in_range("va"))
    def _():
        va_ref[...] = z

    @pl.when(in_range("qb"))
    def _():
        qb_ref[...] = z.astype(BF16)

    @pl.when(in_range("kb"))
    def _():
        kb_ref[...] = z

    @pl.when(in_range("vb"))
    def _():
        vb_ref[...] = z

    @pl.when(in_range("qc"))
    def _():
        per_head(lambda zh: head_norm(zh, qnc_ref[...]), qc_ref)

    @pl.when(in_range("kc"))
    def _():
        per_head(lambda zh: head_norm(zh, knc_ref[...]), kc_ref)

    @pl.when(in_range("vc"))
    def _():
        vc_ref[...] = z

    @pl.when(in_range("g"))
    def _():
        g_ref[...] = jax.nn.sigmoid(z + bg_ref[...]).astype(BF16)


def _proj(x, norm_g, w16, b_gate, qn_a, kn_a, qn_c, kn_c, cos_tab, sin_tab, prev_kv, *, layer, depth,
          w_a, w_b, w_c, d_model):
    n = x.shape[0]
    tm, tn = PROJ_TM, PROJ_TN
    n_in = w16.shape[1]
    assert n % tm == 0 and n_in % tn == 0 and w_a % tn == 0 and w_b == tn and w_c == tn
    assert cos_tab.shape[0] % tm == 0
    tab_tiles = cos_tab.shape[0] // tm
    na = w_a // tn
    widths = [na, na, na, 1, 1, 1, 1, 1, 1, N_BRANCH * d_model // tn]
    bounds, start = {}, 0
    for name, wd in zip(_PROJ_NAMES, widths):
        bounds[name] = (start, start + wd)
        start += wd
    assert start == n_in // tn

    def col_block(name):
        lo, hi = bounds[name]
        return lambda j: jnp.clip(j - lo, 0, hi - lo - 1)

    tab_map = lambda i, j: (i % tab_tiles, 0)
    g_col = col_block("g")
    vec = lambda width: pl.BlockSpec((1, width), lambda i, j: (0, 0))
    in_specs = [
        pl.BlockSpec((tm, d_model), lambda i, j: (i, 0)),
        vec(d_model),
        pl.BlockSpec((d_model, tn), lambda i, j: (0, j)),
        pl.BlockSpec((1, tn), lambda i, j: (0, g_col(j))),
        vec(HEAD_DIM), vec(HEAD_DIM), vec(HEAD_DIM), vec(HEAD_DIM),
        pl.BlockSpec((tm, HEAD_DIM), tab_map),
        pl.BlockSpec((tm, HEAD_DIM), tab_map),
    ]
    out_widths = [w_a, w_a, w_a, w_b, w_b, w_b, w_c, w_c, w_c, N_BRANCH * d_model]
    out_specs, out_shape = [], []
    for idx, (name, wd) in enumerate(zip(_PROJ_NAMES, out_widths)):
        cb = col_block(name)
        if idx in _PROJ_KV:
            out_specs.append(pl.BlockSpec((None, tm, tn), lambda i, j, cb=cb: (layer, i, cb(j))))
            out_shape.append(jax.ShapeDtypeStruct((depth, n, wd), F32))
        else:
            out_specs.append(pl.BlockSpec((tm, tn), lambda i, j, cb=cb: (i, cb(j))))
            out_shape.append(jax.ShapeDtypeStruct((n, wd), BF16))
    prev = [] if prev_kv is None else list(prev_kv)
    aliases = {len(in_specs) + k: _PROJ_KV[k] for k in range(len(prev))}
    in_specs = in_specs + [pl.BlockSpec(memory_space=pl.ANY)] * len(prev)
    tile2 = lambda v: jnp.concatenate([v, v]).reshape(1, HEAD_DIM)
    outs = pl.pallas_call(
        functools.partial(_proj_kernel, bounds=bounds, n_prev=len(prev)),
        grid=(n // tm, n_in // tn),
        in_specs=in_specs,
        out_specs=out_specs,
        out_shape=out_shape,
        scratch_shapes=[pltpu.VMEM((tm, d_model), BF16)],
        input_output_aliases=aliases,
        compiler_params=_params(("arbitrary", "arbitrary")),
        name="proj",
    )(x, norm_g.reshape(1, -1), w16, b_gate.reshape(1, -1), tile2(qn_a), tile2(kn_a),
      qn_c.reshape(1, -1), kn_c.reshape(1, -1), cos_tab, sin_tab, *prev)
    return outs[0], outs[3], outs[6], outs[9], [outs[k] for k in _PROJ_KV]


def _head_rows(ref, h, heads, start=0, size=None):
    size = ref.shape[0] // heads if size is None else size
    return ref[pl.ds(start * heads + h, size, stride=heads), :]


def _stack_maps(q):
    lane = lax.broadcasted_iota(jnp.int32, q.shape, 1)
    zero = jnp.zeros_like(q)
    return jnp.concatenate([jnp.where(lane < D_HALF, q, zero), jnp.where(lane >= D_HALF, q, zero)], axis=0)


def _diff_lambda(lq1_ref, lk1_ref, lq2_ref, lk2_ref, lam_init):
    a = jnp.exp(jnp.sum(lq1_ref[...] * lk1_ref[...], axis=1, keepdims=True))
    b = jnp.exp(jnp.sum(lq2_ref[...] * lk2_ref[...], axis=1, keepdims=True))
    return a - b + lam_init


def _diff_finish(l, acc, t, lam, subln, lam_init):
    o1 = acc[:t] / l[:t]
    o2 = acc[t:] / l[t:]
    o = o1 - lam * o2
    ms = jnp.mean(o * o, axis=-1, keepdims=True)
    return o * lax.rsqrt(ms + EPS) * subln * (1.0 - lam_init)


def _attn_a_prompt_kernel(q_ref, k_ref, v_ref, lq1_ref, lk1_ref, lq2_ref, lk2_ref, sg_ref, o_ref,
                          s_ref, m_ref, l_ref, acc_ref, *, tq, lam_init):
    qi = pl.program_id(2)
    qq = _stack_maps(q_ref[...])

    def lane_fold(op, x, acc):
        for c in range(x.shape[1] // LANES):
            acc = op(acc, x[:, c * LANES:(c + 1) * LANES])
        return acc

    def score_block(kb, masked):
        start = pl.multiple_of(kb * tq, tq)
        s = _dot_nt(qq, k_ref[pl.ds(start, tq), :].astype(BF16))
        if masked:
            row = lax.broadcasted_iota(jnp.int32, s.shape, 0)
            col = lax.broadcasted_iota(jnp.int32, s.shape, 1)
            s = jnp.where(col // CHUNK <= (row % tq) // CHUNK, s, -jnp.inf)
        s_ref[kb] = s
        m_ref[...] = lane_fold(jnp.maximum, s, m_ref[...])

    def score_body(kb, carry):
        score_block(kb, False)
        return carry

    m_ref[...] = jnp.full(m_ref.shape, -jnp.inf, F32)
    lax.fori_loop(0, qi, score_body, 0)
    score_block(qi, True)
    m = jnp.max(m_ref[...], axis=1, keepdims=True)

    def value_body(kb, carry):
        start = pl.multiple_of(kb * tq, tq)
        p = jnp.exp(s_ref[kb] - m)
        l_ref[...] = lane_fold(jnp.add, p, l_ref[...])
        acc_ref[...] += _dot(p.astype(BF16), v_ref[pl.ds(start, tq), :].astype(BF16))
        return carry

    l_ref[...] = jnp.zeros(l_ref.shape, F32)
    acc_ref[...] = jnp.zeros(acc_ref.shape, F32)
    lax.fori_loop(0, qi + 1, value_body, 0)
    l = jnp.sum(l_ref[...], axis=1, keepdims=True)
    lam = _diff_lambda(lq1_ref, lk1_ref, lq2_ref, lk2_ref, lam_init)
    o_ref[...] = _diff_finish(l, acc_ref[...], tq, lam, sg_ref[...], lam_init).astype(o_ref.dtype)


def _lam_specs():
    spec = pl.BlockSpec((1, D_HALF), lambda *_: (0, 0))
    return [spec, spec, spec, spec, pl.BlockSpec((1, HEAD_DIM), lambda *_: (0, 0))]


def _lam_args(lq1, lk1, lq2, lk2, subln):
    return (lq1.reshape(1, -1), lk1.reshape(1, -1), lq2.reshape(1, -1), lk2.reshape(1, -1),
            subln.reshape(1, -1))


def _attn_a_prompt(q16, k32, v32, lam_args, *, layer, batch, seq, heads):
    n = q16.shape[0]
    tq = ATTN_A_TQ
    assert seq % tq == 0 and tq % CHUNK == 0
    nq = seq // tq
    kv_spec = pl.BlockSpec((None, seq, HEAD_DIM), lambda b, h, qi: (layer, b, h))
    return pl.pallas_call(
        functools.partial(_attn_a_prompt_kernel, tq=tq, lam_init=_lambda_init(layer)),
        grid=(batch, heads, nq),
        in_specs=[pl.BlockSpec((tq, HEAD_DIM), lambda b, h, qi: (b * nq + qi, h)), kv_spec, kv_spec]
                 + _lam_specs(),
        out_specs=pl.BlockSpec((tq, HEAD_DIM), lambda b, h, qi: (b * nq + qi, h)),
        out_shape=jax.ShapeDtypeStruct((n, heads * HEAD_DIM), BF16),
        scratch_shapes=[pltpu.VMEM((nq, 2 * tq, tq), F32),
                        pltpu.VMEM((2 * tq, LANES), F32), pltpu.VMEM((2 * tq, LANES), F32),
                        pltpu.VMEM((2 * tq, HEAD_DIM), F32)],
        compiler_params=_params(("arbitrary", "arbitrary", "arbitrary")),
        name="attn_a_prompt",
    )(q16, k32, v32, *lam_args)


def _attn_a_sample_kernel(q_ref, ck_ref, cv_ref, kn_ref, vn_ref, lq1_ref, lk1_ref, lq2_ref, lk2_ref, sg_ref,
                          o_ref, qq_ref, s_ref, sn_ref, m_ref, mf_ref, l_ref, acc_ref, *,
                          heads, new_len, lam_init):
    phase = pl.program_id(1)
    kt = pl.program_id(2)
    nkt = pl.num_programs(2)

    def lane_fold(op, x, acc):
        for c in range(x.shape[1] // LANES):
            acc = op(acc, x[:, c * LANES:(c + 1) * LANES])
        return acc

    @pl.when(jnp.logical_and(phase == 0, kt == 0))
    def _():
        for h in range(heads):
            sl = slice(h * HEAD_DIM, (h + 1) * HEAD_DIM)
            qq = _stack_maps(q_ref[:, sl])
            qq_ref[h] = qq
            s_new = _dot_nt(qq, kn_ref[:, sl].astype(BF16))
            sn_ref[h] = s_new
            m_ref[h] = jnp.broadcast_to(jnp.max(s_new, axis=1, keepdims=True), m_ref.shape[1:])

    @pl.when(phase == 0)
    def _():
        for h in range(heads):
            s = _dot_nt(qq_ref[h], _head_rows(ck_ref, h, heads).astype(BF16))
            s_ref[h, kt] = s
            m_ref[h] = lane_fold(jnp.maximum, s, m_ref[h])

    @pl.when(jnp.logical_and(phase == 1, kt == 0))
    def _():
        lane = lax.broadcasted_iota(jnp.int32, l_ref.shape[1:], 1)
        for h in range(heads):
            sl = slice(h * HEAD_DIM, (h + 1) * HEAD_DIM)
            m = jnp.max(m_ref[h], axis=1, keepdims=True)
            mf_ref[h] = m
            p_new = jnp.exp(sn_ref[h] - m)
            l_ref[h] = jnp.where(lane == 0, jnp.sum(p_new, axis=1, keepdims=True), 0.0)
            acc_ref[h] = _dot(p_new.astype(BF16), vn_ref[:, sl].astype(BF16))

    @pl.when(phase == 1)
    def _():
        for h in range(heads):
            p = jnp.exp(s_ref[h, kt] - mf_ref[h])
            l_ref[h] = lane_fold(jnp.add, p, l_ref[h])
            acc_ref[h] += _dot(p.astype(BF16), _head_rows(cv_ref, h, heads).astype(BF16))

    @pl.when(jnp.logical_and(phase == 1, kt == nkt - 1))
    def _():
        lam = _diff_lambda(lq1_ref, lk1_ref, lq2_ref, lk2_ref, lam_init)
        for h in range(heads):
            l = jnp.sum(l_ref[h], axis=1, keepdims=True)
            y = _diff_finish(l, acc_ref[h], new_len, lam, sg_ref[...], lam_init)
            o_ref[:, h * HEAD_DIM:(h + 1) * HEAD_DIM] = y.astype(o_ref.dtype)


def _attn_a_sample(q16, cache_k, cache_v, k32, v32, lam_args, *, layer, streams, new_len, heads):
    past = cache_k.shape[2] // heads
    tk = SAMPLE_A_TK
    assert past % tk == 0
    nkt = past // tk
    width = heads * HEAD_DIM
    q_spec = pl.BlockSpec((new_len, width), lambda b, p, kt: (b, 0))
    new_spec = pl.BlockSpec((None, new_len, width), lambda b, p, kt: (layer, b, 0))
    k_spec = pl.BlockSpec((None, None, tk * heads, HEAD_DIM),
                          lambda b, p, kt: (layer, b, jnp.where(p == 0, kt, nkt - 1), 0))
    v_spec = pl.BlockSpec((None, None, tk * heads, HEAD_DIM),
                          lambda b, p, kt: (layer, b, jnp.where(p == 0, 0, kt), 0))
    rows = 2 * new_len
    return pl.pallas_call(
        functools.partial(_attn_a_sample_kernel, heads=heads, new_len=new_len,
                          lam_init=_lambda_init(layer)),
        grid=(streams, 2, nkt),
        in_specs=[q_spec, k_spec, v_spec, new_spec, new_spec] + _lam_specs(),
        out_specs=q_spec,
        out_shape=jax.ShapeDtypeStruct(q16.shape, BF16),
        scratch_shapes=[pltpu.VMEM((heads, rows, HEAD_DIM), BF16),
                        pltpu.VMEM((heads, nkt, rows, tk), F32),
                        pltpu.VMEM((heads, rows, new_len), F32),
                        pltpu.VMEM((heads, rows, LANES), F32),
                        pltpu.VMEM((heads, rows, 1), F32),
                        pltpu.VMEM((heads, rows, LANES), F32),
                        pltpu.VMEM((heads, rows, HEAD_DIM), F32)],
        compiler_params=_params(("arbitrary", "arbitrary", "arbitrary")),
        name="attn_a_sample",
    )(q16, cache_k, cache_v, k32, v32, *lam_args)


def _upper_ones(t):
    r = lax.broadcasted_iota(jnp.int32, (t, t), 0)
    c = lax.broadcasted_iota(jnp.int32, (t, t), 1)
    return (r > c).astype(BF16)


def _sb_block(q, kblk, vblk, ones_u, c, acc, scale, strict):
    z = _dot_nt(q, kblk) * scale
    sp = _softplus(z)
    log_stay = -sp
    if strict is not None:
        log_stay = jnp.where(strict, log_stay, 0.0)
    between = _dot_f32acc(log_stay, ones_u)
    w = jnp.exp((z - sp) + between + c)
    if strict is not None:
        w = jnp.where(strict, w, 0.0)
    acc = acc + _dot(w.astype(BF16), vblk)
    c = c + jnp.sum(log_stay, axis=1, keepdims=True)
    return c, acc


def _attn_b_prompt_kernel(q_ref, k_ref, v_ref, o_ref, c_ref, acc_ref, *, tq, tk):
    qi = pl.program_id(2)
    q = q_ref[...]
    scale = HEAD_DIM ** -0.5
    ones_u = _upper_ones(tk)
    per_q = tq // tk
    c_ref[...] = jnp.zeros(c_ref.shape, F32)
    acc_ref[...] = jnp.zeros(acc_ref.shape, F32)

    def block(kb, masked):
        start = pl.multiple_of(kb * tk, tk)
        strict = None
        if masked:
            row = lax.broadcasted_iota(jnp.int32, (tq, tk), 0)
            col = lax.broadcasted_iota(jnp.int32, (tq, tk), 1)
            strict = start + col < qi * tq + row
        c, acc = _sb_block(q, k_ref[pl.ds(start, tk), :].astype(BF16), v_ref[pl.ds(start, tk), :].astype(BF16),
                           ones_u, c_ref[...], acc_ref[...], scale, strict)
        c_ref[...] = c
        acc_ref[...] = acc

    for d in reversed(range(per_q)):
        block(qi * per_q + d, True)

    def body(t, carry):
        block(qi * per_q - 1 - t, False)
        return carry

    lax.fori_loop(0, qi * per_q, body, 0)
    o_ref[...] = acc_ref[...].astype(o_ref.dtype)


def _attn_b_prompt(q16, k32, v32, *, layer, batch, seq, heads):
    n = q16.shape[0]
    tq, tk = ATTN_B_TQ, SB_SUB
    assert seq % tq == 0 and tq % tk == 0
    nq = seq // tq
    kv_spec = pl.BlockSpec((None, seq, HEAD_DIM), lambda b, h, qi: (layer, b, h))
    return pl.pallas_call(
        functools.partial(_attn_b_prompt_kernel, tq=tq, tk=tk),
        grid=(batch, heads, nq),
        in_specs=[pl.BlockSpec((tq, HEAD_DIM), lambda b, h, qi: (b * nq + qi, h)), kv_spec, kv_spec],
        out_specs=pl.BlockSpec((tq, HEAD_DIM), lambda b, h, qi: (b * nq + qi, h)),
        out_shape=jax.ShapeDtypeStruct((n, heads * HEAD_DIM), BF16),
        scratch_shapes=[pltpu.VMEM((tq, 1), F32), pltpu.VMEM((tq, HEAD_DIM), F32)],
        compiler_params=_params(("arbitrary", "arbitrary", "arbitrary")),
        name="attn_b_prompt",
    )(q16, k32, v32)


def _attn_b_sample_kernel(q_ref, ck_ref, cv_ref, kn_ref, vn_ref, o_ref, u_ref, c_ref, acc_ref, *,
                          heads, new_len):
    b = pl.program_id(0)
    kt = pl.program_id(1)
    nkt = pl.num_programs(1)
    scale = HEAD_DIM ** -0.5
    tk = ck_ref.shape[0] // heads
    head_cols = lambda h: slice(h * HEAD_DIM, (h + 1) * HEAD_DIM)
    head_rows = lambda h: slice(h * new_len, (h + 1) * new_len)

    @pl.when(jnp.logical_and(b == 0, kt == 0))
    def _():
        u_ref[...] = _upper_ones(tk)

    @pl.when(kt == 0)
    def _():
        ones_new = _upper_ones(new_len)
        row = lax.broadcasted_iota(jnp.int32, (new_len, new_len), 0)
        col = lax.broadcasted_iota(jnp.int32, (new_len, new_len), 1)
        for h in range(heads):
            sl = head_cols(h)
            c, acc = _sb_block(q_ref[:, sl], kn_ref[:, sl].astype(BF16), vn_ref[:, sl].astype(BF16), ones_new,
                               jnp.zeros((new_len, 1), F32), jnp.zeros((new_len, HEAD_DIM), F32),
                               scale, col < row)
            c_ref[head_rows(h), :] = c
            acc_ref[h] = acc

    z = jnp.concatenate([_dot_nt(q_ref[:, head_cols(h)], _head_rows(ck_ref, h, heads).astype(BF16))
                         for h in range(heads)], axis=0) * scale
    sp = _softplus(z)
    log_stay = -sp
    between = _dot_f32acc(log_stay, u_ref[...])
    w = jnp.exp((z - sp) + between + c_ref[...]).astype(BF16)
    for h in range(heads):
        acc_ref[h] += _dot(w[head_rows(h), :], _head_rows(cv_ref, h, heads).astype(BF16))
    c_ref[...] += jnp.sum(log_stay, axis=1, keepdims=True)

    @pl.when(kt == nkt - 1)
    def _():
        for h in range(heads):
            o_ref[:, head_cols(h)] = acc_ref[h].astype(o_ref.dtype)


def _attn_b_sample(q16, cache_k, cache_v, k32, v32, *, layer, streams, new_len, heads):
    past = cache_k.shape[2] // heads
    tk = SAMPLE_TK
    nkt = past // tk
    assert past % tk == 0
    width = heads * HEAD_DIM
    q_spec = pl.BlockSpec((new_len, width), lambda b, kt: (b, 0))
    new_spec = pl.BlockSpec((None, new_len, width), lambda b, kt: (layer, b, 0))
    cache_spec = pl.BlockSpec((None, None, tk * heads, HEAD_DIM), lambda b, kt: (layer, b, nkt - 1 - kt, 0))
    return pl.pallas_call(
        functools.partial(_attn_b_sample_kernel, heads=heads, new_len=new_len),
        grid=(streams, nkt),
        in_specs=[q_spec, cache_spec, cache_spec, new_spec, new_spec],
        out_specs=q_spec,
        out_shape=jax.ShapeDtypeStruct(q16.shape, BF16),
        scratch_shapes=[pltpu.VMEM((tk, tk), BF16),
                        pltpu.VMEM((heads * new_len, 1), F32),
                        pltpu.VMEM((heads, new_len, HEAD_DIM), F32)],
        compiler_params=_params(("arbitrary", "arbitrary")),
        name="attn_b_sample",
    )(q16, cache_k, cache_v, k32, v32)


def _attn_c_prompt_kernel(q_ref, k_ref, v_ref, bgrp_ref, o_ref, k16_ref, v16_ref, *, seq):
    k16_ref[0:C_WIN, :] = jnp.zeros((C_WIN, HEAD_DIM), BF16)
    v16_ref[0:C_WIN, :] = jnp.zeros((C_WIN, HEAD_DIM), BF16)
    k16_ref[C_WIN:, :] = k_ref[...].astype(BF16)
    v16_ref[C_WIN:, :] = v_ref[...].astype(BF16)
    scale = HEAD_DIM ** -0.5
    key_chunk = lax.broadcasted_iota(jnp.int32, (BAND_ROWS, BAND_KEYS), 1) // CHUNK

    def group(gi):
        r0 = pl.multiple_of(gi * BAND_ROWS, BAND_ROWS)
        s = _dot_nt(q_ref[pl.ds(r0, BAND_ROWS), :], k16_ref[pl.ds(r0, BAND_KEYS), :]) * scale + bgrp_ref[0]
        s = jnp.where(key_chunk + gi * BAND_GROUP >= C_BAND_CHUNKS, s, -jnp.inf)
        m = jnp.max(s, axis=1, keepdims=True)
        p = jnp.exp(s - m)
        l = jnp.sum(p, axis=1, keepdims=True)
        o = _dot(p.astype(BF16), v16_ref[pl.ds(r0, BAND_KEYS), :]) / l
        o_ref[pl.ds(r0, BAND_ROWS), :] = o.astype(o_ref.dtype)

    def body(t, carry):
        group(2 * t)
        group(2 * t + 1)
        return carry

    lax.fori_loop(0, seq // BAND_ROWS // 2, body, 0)


def _attn_c_prompt(q16, k32, v32, bias_grp, *, layer, batch, seq, heads):
    n = q16.shape[0]
    assert seq % (2 * BAND_ROWS) == 0
    kv_spec = pl.BlockSpec((None, seq, HEAD_DIM), lambda b, h: (layer, b, h))
    return pl.pallas_call(
        functools.partial(_attn_c_prompt_kernel, seq=seq),
        grid=(batch, heads),
        in_specs=[pl.BlockSpec((seq, HEAD_DIM), lambda b, h: (b, h)), kv_spec, kv_spec,
                  pl.BlockSpec((1, BAND_ROWS, BAND_KEYS), lambda b, h: (layer * heads + h, 0, 0))],
        out_specs=pl.BlockSpec((seq, HEAD_DIM), lambda b, h: (b, h)),
        out_shape=jax.ShapeDtypeStruct((n, heads * HEAD_DIM), BF16),
        scratch_shapes=[pltpu.VMEM((seq + C_WIN, HEAD_DIM), BF16),
                        pltpu.VMEM((seq + C_WIN, HEAD_DIM), BF16)],
        compiler_params=_params(("arbitrary", "arbitrary")),
        name="attn_c_prompt",
    )(q16, k32, v32, bias_grp)


def _attn_c_sample_kernel(q_ref, ck_ref, cv_ref, kn_ref, vn_ref, bfar_ref, bown_ref, o_ref, *, heads):
    scale = HEAD_DIM ** -0.5
    for h in range(heads):
        sl = slice(h * HEAD_DIM, (h + 1) * HEAD_DIM)
        q = q_ref[:, sl]
        s_far = _dot_nt(q, _head_rows(ck_ref, h, heads).astype(BF16)) * scale + bfar_ref[h]
        s_own = _dot_nt(q, kn_ref[:, sl].astype(BF16)) * scale + bown_ref[h]
        m = jnp.maximum(jnp.max(s_far, axis=1, keepdims=True), jnp.max(s_own, axis=1, keepdims=True))
        p_far = jnp.exp(s_far - m)
        p_own = jnp.exp(s_own - m)
        l = jnp.sum(p_far, axis=1, keepdims=True) + jnp.sum(p_own, axis=1, keepdims=True)
        o = (_dot(p_far.astype(BF16), _head_rows(cv_ref, h, heads).astype(BF16))
             + _dot(p_own.astype(BF16), vn_ref[:, sl].astype(BF16))) / l
        o_ref[:, sl] = o.astype(o_ref.dtype)


def _attn_c_sample(q16, cache_k, cache_v, k32, v32, bias_far, bias_own, *, layer, streams, new_len, heads):
    assert cache_k.shape[2] == C_WIN * heads and new_len == CHUNK
    width = heads * HEAD_DIM
    q_spec = pl.BlockSpec((new_len, width), lambda b: (b, 0))
    new_spec = pl.BlockSpec((None, new_len, width), lambda b: (layer, b, 0))
    cache_spec = pl.BlockSpec((None, None, C_WIN * heads, HEAD_DIM), lambda b: (layer, b, 0, 0))
    return pl.pallas_call(
        functools.partial(_attn_c_sample_kernel, heads=heads),
        grid=(streams,),
        in_specs=[q_spec, cache_spec, cache_spec, new_spec, new_spec,
                  pl.BlockSpec((heads, CHUNK, C_WIN), lambda b: (layer, 0, 0)),
                  pl.BlockSpec((heads, CHUNK, CHUNK), lambda b: (layer, 0, 0))],
        out_specs=q_spec,
        out_shape=jax.ShapeDtypeStruct(q16.shape, BF16),
        compiler_params=_params(("arbitrary",)),
        name="attn_c_sample",
    )(q16, cache_k, cache_v, k32, v32, bias_far, bias_own)


def _merge_kernel(x_ref, oa_ref, ob_ref, oc_ref, g_ref, wb_ref, wo_ref, o_ref, *, w_a, w_b, d_model):
    ya = _dot(oa_ref[...], wb_ref[0:w_a, :])
    yb = _dot(ob_ref[...], wb_ref[w_a:w_a + w_b, :])
    yc = _dot(oc_ref[...], wb_ref[w_a + w_b:, :])
    m = (g_ref[:, 0:d_model].astype(F32) * ya + g_ref[:, d_model:2 * d_model].astype(F32) * yb
         + g_ref[:, 2 * d_model:].astype(F32) * yc)
    o_ref[...] = x_ref[...] + _dot(m.astype(BF16), wo_ref[...])


def _merge(x, oa, ob, oc, g, wb16, wo16):
    n, d_model = x.shape
    tm = MERGE_TM
    assert n % tm == 0
    w_a, w_b, w_c = oa.shape[1], ob.shape[1], oc.shape[1]
    rows = lambda width: pl.BlockSpec((tm, width), lambda i: (i, 0))
    resident = lambda shape: pl.BlockSpec(shape, lambda i: (0, 0), pipeline_mode=pl.Buffered(1))
    return pl.pallas_call(
        functools.partial(_merge_kernel, w_a=w_a, w_b=w_b, d_model=d_model),
        grid=(n // tm,),
        in_specs=[rows(d_model), rows(w_a), rows(w_b), rows(w_c), rows(N_BRANCH * d_model),
                  resident(wb16.shape), resident(wo16.shape)],
        out_specs=rows(d_model),
        out_shape=jax.ShapeDtypeStruct((n, d_model), F32),
        compiler_params=_params(("arbitrary",)),
        name="merge",
    )(x, oa, ob, oc, g, wb16, wo16)


def _moe_kernel(x_ref, ng_ref, wr_ref, br_ref, wg_ref, wu_ref, wd_ref, o_ref, xn_ref, cw_ref):
    e = pl.program_id(1)
    tm = x_ref.shape[0]
    lane = lax.broadcasted_iota(jnp.int32, (tm, LANES), 1)
    lane_f = lane.astype(F32)

    @pl.when(e == 0)
    def _():
        x = x_ref[...]
        ms = jnp.mean(x * x, axis=-1, keepdims=True)
        xn = x * lax.rsqrt(ms + EPS) * ng_ref[...]
        xn_ref[...] = xn.astype(BF16)
        o_ref[...] = x

        x_hi, x_lo = _split_bf16(xn)
        w_hi, w_lo = _split_bf16(wr_ref[...])
        logits = _dot(x_hi, w_hi) + _dot(x_lo, w_hi) + _dot(x_hi, w_lo) + br_ref[...]
        is_group = lane < N_GROUPS
        gl = jnp.where(is_group, logits, -jnp.inf)
        g_max = jnp.max(gl, axis=1, keepdims=True)
        g_idx = jnp.min(jnp.where(gl == g_max, lane_f, float(LANES)), axis=1, keepdims=True)
        p_group = 1.0 / jnp.sum(jnp.where(is_group, jnp.exp(logits - g_max), 0.0), axis=1, keepdims=True)
        in_group = jnp.logical_and(lane >= N_GROUPS, lane < N_GROUPS + N_EXP)
        in_group = jnp.logical_and(in_group, ((lane - N_GROUPS) // E_PER_GROUP).astype(F32) == g_idx)
        el = jnp.where(in_group, logits, -jnp.inf)
        v1 = jnp.max(el, axis=1, keepdims=True)
        i1 = jnp.min(jnp.where(el == v1, lane_f, float(LANES)), axis=1, keepdims=True)
        el2 = jnp.where(lane_f == i1, -jnp.inf, el)
        v2 = jnp.max(el2, axis=1, keepdims=True)
        i2 = jnp.min(jnp.where(el2 == v2, lane_f, float(LANES)), axis=1, keepdims=True)
        t = jnp.exp(v2 - v1)
        s1 = 1.0 / (1.0 + t)
        s2 = t / (1.0 + t)
        cw_ref[...] = (jnp.where(lane_f == i1, p_group * s1, 0.0)
                       + jnp.where(lane_f == i2, p_group * s2, 0.0))

    ce = jnp.sum(jnp.where(lane == e + N_GROUPS, cw_ref[...], 0.0), axis=1, keepdims=True)
    xn = xn_ref[...]
    h = jax.nn.silu(_dot(xn, wg_ref[...])) * _dot(xn, wu_ref[...]) * ce
    o_ref[...] += _dot(h.astype(BF16), wd_ref[...])


def _moe(x, norm_g, w_router, b_router, wg16, wu16, wd16):
    n, d_model = x.shape
    tm = MOE_TM
    assert n % tm == 0
    n_exp, _, d_exp = wg16.shape
    return pl.pallas_call(
        _moe_kernel,
        grid=(n // tm, n_exp),
        in_specs=[pl.BlockSpec((tm, d_model), lambda i, e: (i, 0)),
                  pl.BlockSpec((1, d_model), lambda i, e: (0, 0)),
                  pl.BlockSpec((d_model, LANES), lambda i, e: (0, 0)),
                  pl.BlockSpec((1, LANES), lambda i, e: (0, 0)),
                  pl.BlockSpec((None, d_model, d_exp), lambda i, e: (e, 0, 0)),
                  pl.BlockSpec((None, d_model, d_exp), lambda i, e: (e, 0, 0)),
                  pl.BlockSpec((None, d_exp, d_model), lambda i, e: (e, 0, 0))],
        out_specs=pl.BlockSpec((tm, d_model), lambda i, e: (i, 0)),
        out_shape=jax.ShapeDtypeStruct((n, d_model), F32),
        scratch_shapes=[pltpu.VMEM((tm, d_model), BF16), pltpu.VMEM((tm, LANES), F32)],
        compiler_params=_params(("arbitrary", "arbitrary")),
        name="moe",
    )(x, norm_g.reshape(1, -1), w_router, b_router, wg16, wu16, wd16)


def _rope_tables(pos):
    half = D_HALF // 2
    inv = ROPE_THETA ** (-2.0 * jnp.arange(half, dtype=F32) / D_HALF)
    ang = pos.astype(F32)[:, None] * inv[None, :]
    cos, sin = jnp.cos(ang), jnp.sin(ang)
    cos_full = jnp.tile(cos, (1, HEAD_DIM // half))
    sin_signed = jnp.tile(jnp.concatenate([-sin, sin], axis=1), (1, HEAD_DIM // D_HALF))
    return cos_full, sin_signed


def kernel(x_prompt, x_sample, cache_a_k, cache_a_v, cache_b_k, cache_b_v, cache_c_k, cache_c_v, norm_mix, w_in, b_gate, q_norm_a, k_norm_a, lam_q1, lam_k1, lam_q2, lam_k2, subln_a, q_norm_c, k_norm_c, rel_bias_c, w_branch, w_out, norm_ffn, w_group, b_group, w_expert_router, b_expert_router, w_gate_e, w_up_e, w_down_e):
    batch, seq, d_model = x_prompt.shape
    streams, new_len, _ = x_sample.shape
    depth, _, past, h_a, _ = cache_a_k.shape
    h_b, h_c = cache_b_k.shape[3], cache_c_k.shape[3]
    w_a, w_b, w_c = h_a * HEAD_DIM, h_b * HEAD_DIM, h_c * HEAD_DIM
    n_prompt, n_sample = batch * seq, streams * new_len
    tm = PROJ_TM
    assert seq % tm == 0 and n_sample % tm == 0 and tm % new_len == 0
    c_rows = min(C_WIN, seq)

    xp = x_prompt.reshape(n_prompt, d_model)
    xs = x_sample.reshape(n_sample, d_model)
    tab_p = _rope_tables(jnp.arange(seq))
    tab_s = _rope_tables(jnp.tile(past + jnp.arange(new_len), tm // new_len))
    bias_far, bias_own, bias_grp = _expand_bias(rel_bias_c.reshape(depth * h_c, N_REL))
    rows_view = lambda c: c.reshape(c.shape[0], c.shape[1], c.shape[2] * c.shape[3], c.shape[4])
    ca_k, ca_v, cb_k, cb_v, cc_k, cc_v = map(rows_view, (cache_a_k, cache_a_v, cache_b_k, cache_b_v,
                                                          cache_c_k, cache_c_v))

    kv_p = kv_s = None
    for l in range(depth):
        proj_w = (norm_mix[l], w_in[l].astype(BF16), b_gate[l], q_norm_a[l], k_norm_a[l], q_norm_c[l],
                  k_norm_c[l])
        proj_kw = dict(layer=l, depth=depth, w_a=w_a, w_b=w_b, w_c=w_c, d_model=d_model)
        qa_p, qb_p, qc_p, g_p, kv_p = _proj(xp, *proj_w, *tab_p, kv_p, **proj_kw)
        qa_s, qb_s, qc_s, g_s, kv_s = _proj(xs, *proj_w, *tab_s, kv_s, **proj_kw)
        lam_args = _lam_args(lam_q1[l], lam_k1[l], lam_q2[l], lam_k2[l], subln_a[l])

        oa_p = _attn_a_prompt(qa_p, kv_p[0], kv_p[1], lam_args, layer=l, batch=batch, seq=seq, heads=h_a)
        ob_p = _attn_b_prompt(qb_p, kv_p[2], kv_p[3], layer=l, batch=batch, seq=seq, heads=h_b)
        oc_p = _attn_c_prompt(qc_p, kv_p[4], kv_p[5], bias_grp, layer=l, batch=batch, seq=seq, heads=h_c)
        sample_kw = dict(layer=l, streams=streams, new_len=new_len)
        oa_s = _attn_a_sample(qa_s, ca_k, ca_v, kv_s[0], kv_s[1], lam_args, heads=h_a, **sample_kw)
        ob_s = _attn_b_sample(qb_s, cb_k, cb_v, kv_s[2], kv_s[3], heads=h_b, **sample_kw)
        oc_s = _attn_c_sample(qc_s, cc_k, cc_v, kv_s[4], kv_s[5], bias_far, bias_own, heads=h_c, **sample_kw)

        wb16, wo16 = w_branch[l].astype(BF16), w_out[l].astype(BF16)
        xp = _merge(xp, oa_p, ob_p, oc_p, g_p, wb16, wo16)
        xs = _merge(xs, oa_s, ob_s, oc_s, g_s, wb16, wo16)

        w_router = jnp.concatenate(
            [w_group[l], jnp.transpose(w_expert_router[l], (1, 0, 2)).reshape(d_model, N_EXP)], axis=1)
        w_router = jnp.pad(w_router, ((0, 0), (0, LANES - w_router.shape[1])))
        b_router = jnp.pad(jnp.concatenate([b_group[l], b_expert_router[l].reshape(-1)]),
                           (0, LANES - N_GROUPS - N_EXP)).reshape(1, LANES)
        moe_w = (norm_ffn[l], w_router, b_router, w_gate_e[l].astype(BF16), w_up_e[l].astype(BF16),
                 w_down_e[l].astype(BF16))
        xp = _moe(xp, *moe_w)
        xs = _moe(xs, *moe_w)

    heads_of = [h_a, h_a, h_b, h_b, h_c, h_c]
    prompt_kv = [t.reshape(depth, batch, seq, h, HEAD_DIM) for t, h in zip(kv_p, heads_of)]
    prompt_kv[4] = prompt_kv[4][:, :, seq - c_rows:]
    prompt_kv[5] = prompt_kv[5][:, :, seq - c_rows:]
    sample_kv = [t.reshape(depth, streams, new_len, h, HEAD_DIM) for t, h in zip(kv_s, heads_of)]
    return (xp.reshape(batch, seq, d_model), xs.reshape(streams, new_len, d_model), *prompt_kv, *sample_kv)
```

```python
import functools
import math

import jax
import jax.numpy as jnp
from jax import lax
from jax.experimental import pallas as pl
from jax.experimental.pallas import tpu as pltpu

F32 = jnp.float32
BF16 = jnp.bfloat16

CHUNK = 64
HEAD_DIM = 128
D_HALF = HEAD_DIM // 2
C_BAND_CHUNKS = 8
C_WIN = C_BAND_CHUNKS * CHUNK
MAX_REL = 128
N_REL = 2 * MAX_REL + 1
ROPE_THETA = 10000.0
N_GROUPS = 4
E_PER_GROUP = 4
N_EXP = N_GROUPS * E_PER_GROUP
EPS = 1e-6
N_BRANCH = 3

LANES = 128
VMEM_LIMIT = 56 * 1024 * 1024

PROJ_TM = 512
PROJ_TN = 512
ATTN_A_TQ = 512
ATTN_B_TQ = 512
SB_SUB = 256
SAMPLE_TK = 512
SAMPLE_A_TK = 1024
BAND_GROUP = 4
BAND_ROWS = BAND_GROUP * CHUNK
BAND_KEYS = (BAND_GROUP + C_BAND_CHUNKS) * CHUNK
MERGE_TM = 256
MOE_ROUTE_TM = 512
MOE_ROW_TILE = 256
MOE_TOK_TILE = 256


def _lambda_init(layer_idx):
    return 0.8 - 0.6 * math.exp(-0.3 * layer_idx)


def _dot(a, b):
    return jnp.dot(a, b, preferred_element_type=F32)


def _dot_nt(a, b):
    return lax.dot_general(a, b, (((1,), (1,)), ((), ())), preferred_element_type=F32)


def _split_bf16(x):
    hi = x.astype(BF16)
    lo = (x - hi.astype(F32)).astype(BF16)
    return hi, lo


def _dot_f32acc(x, w_bf16):
    hi, lo = _split_bf16(x)
    return _dot(hi, w_bf16) + _dot(lo, w_bf16)


def _softplus(z):
    return jnp.maximum(z, 0.0) + jnp.log(1.0 + jnp.exp(-jnp.abs(z)))


def _params(sem, vmem=VMEM_LIMIT):
    return pltpu.CompilerParams(dimension_semantics=sem, vmem_limit_bytes=vmem)


def _bias_kernel(tab_ref, far_ref, own_ref, grp_ref):
    r = pl.program_id(0)
    half = BAND_KEYS - 2 * CHUNK

    def rel_index(shape, key_shift):
        qi = lax.broadcasted_iota(jnp.int32, shape, 0)
        kj = lax.broadcasted_iota(jnp.int32, shape, 1) - key_shift
        return jnp.clip(qi + C_WIN - kj, -MAX_REL, MAX_REL) + MAX_REL, kj

    idx_even, kj_even = rel_index((CHUNK, half), 0)
    idx_odd, kj_odd = rel_index((CHUNK, half), CHUNK)

    def body(t, carry):
        val = tab_ref[r, t]
        return tuple(jnp.where(idx == t, val, acc) for idx, acc in zip((idx_even, idx_odd), carry))

    zeros = jnp.zeros((CHUNK, half), F32)
    even, odd = lax.fori_loop(0, N_REL, body, (zeros, zeros))
    far_ref[0] = even[:, :C_WIN]
    own_ref[0] = even[:, C_WIN:C_WIN + CHUNK]
    band = lambda kj: jnp.logical_and(kj >= 0, kj < C_WIN + CHUNK)
    even = jnp.where(band(kj_even), even, -jnp.inf)
    odd = jnp.where(band(kj_odd), odd, -jnp.inf)
    for a in range(BAND_GROUP):
        lead = (a // 2) * 2 * CHUNK
        tail = BAND_KEYS - lead - half
        pieces = [even if a % 2 == 0 else odd]
        if lead:
            pieces.insert(0, jnp.full((CHUNK, lead), -jnp.inf, F32))
        if tail:
            pieces.append(jnp.full((CHUNK, tail), -jnp.inf, F32))
        grp_ref[0, a * CHUNK:(a + 1) * CHUNK, :] = jnp.concatenate(pieces, axis=1)


def _expand_bias(table):
    rows = table.shape[0]
    shapes = [(CHUNK, C_WIN), (CHUNK, CHUNK), (BAND_ROWS, BAND_KEYS)]
    return pl.pallas_call(
        _bias_kernel,
        grid=(rows,),
        in_specs=[pl.BlockSpec(memory_space=pltpu.SMEM)],
        out_specs=[pl.BlockSpec((1,) + s, lambda r: (r, 0, 0)) for s in shapes],
        out_shape=[jax.ShapeDtypeStruct((rows,) + s, F32) for s in shapes],
        compiler_params=_params(("arbitrary",)),
        name="bias_expand",
    )(table)


_PROJ_NAMES = ["qa", "ka", "va", "qb", "kb", "vb", "qc", "kc", "vc", "g"]
_PROJ_KV = [1, 2, 4, 5, 7, 8]


def _proj_kernel(*refs, bounds, n_prev):
    (x_ref, ng_ref, w_ref, bg_ref, qna_ref, kna_ref, qnc_ref, knc_ref, cos_ref, sin_ref) = refs[:10]
    (qa_ref, ka_ref, va_ref, qb_ref, kb_ref, vb_ref, qc_ref, kc_ref, vc_ref, g_ref,
     xn_ref) = refs[10 + n_prev:]
    j = pl.program_id(1)

    @pl.when(j == 0)
    def _():
        x = x_ref[...]
        ms = jnp.mean(x * x, axis=-1, keepdims=True)
        xn_ref[...] = (x * lax.rsqrt(ms + EPS) * ng_ref[...]).astype(BF16)

    z = _dot(xn_ref[...], w_ref[...])
    tm, tn = z.shape
    heads = tn // HEAD_DIM

    r_i = lax.broadcasted_iota(jnp.int32, (HEAD_DIM, HEAD_DIM), 0)
    c_i = lax.broadcasted_iota(jnp.int32, (HEAD_DIM, HEAD_DIM), 1)
    ones_map = ((r_i // D_HALF) == (c_i // D_HALF)).astype(BF16)
    ones_head = jnp.ones((HEAD_DIM, HEAD_DIM), BF16)
    lane = lax.broadcasted_iota(jnp.int32, (tm, HEAD_DIM), 1)
    first_half = (lane % D_HALF) < (D_HALF // 2)

    def map_norm_rope(zh, gain):
        ms = _dot_f32acc(zh * zh, ones_map) * (1.0 / D_HALF)
        y = zh * lax.rsqrt(ms + EPS) * gain
        partner = jnp.where(first_half,
                            pltpu.roll(y, HEAD_DIM - D_HALF // 2, 1),
                            pltpu.roll(y, D_HALF // 2, 1))
        return y * cos_ref[...] + partner * sin_ref[...]

    def head_norm(zh, gain):
        ms = _dot_f32acc(zh * zh, ones_head) * (1.0 / HEAD_DIM)
        return zh * lax.rsqrt(ms + EPS) * gain

    def per_head(fn, out_ref, scale=None):
        for hh in range(heads):
            sl = slice(hh * HEAD_DIM, (hh + 1) * HEAD_DIM)
            y = fn(z[:, sl])
            if scale is not None:
                y = y * scale
            out_ref[:, sl] = y.astype(out_ref.dtype)

    def in_range(name):
        lo, hi = bounds[name]
        return jnp.logical_and(j >= lo, j < hi)

    @pl.when(in_range("qa"))
    def _():
        per_head(lambda zh: map_norm_rope(zh, qna_ref[...]), qa_ref, scale=D_HALF ** -0.5)

    @pl.when(in_range("ka"))
    def _():
        per_head(lambda zh: map_norm_rope(zh, kna_ref[...]), ka_ref)

    @pl.when(in_range("va"))
    def _():
        va_ref[...] = z

    @pl.when(in_range("qb"))
    def _():
        qb_ref[...] = z.astype(BF16)

    @pl.when(in_range("kb"))
    def _():
        kb_ref[...] = z

    @pl.when(in_range("vb"))
    def _():
        vb_ref[...] = z

    @pl.when(in_range("qc"))
    def _():
        per_head(lambda zh: head_norm(zh, qnc_ref[...]), qc_ref)

    @pl.when(in_range("kc"))
    def _():
        per_head(lambda zh: head_norm(zh, knc_ref[...]), kc_ref)

    @pl.when(in_range("vc"))
    def _():
        vc_ref[...] = z

    @pl.when(in_range("g"))
    def _():
        g_ref[...] = jax.nn.sigmoid(z + bg_ref[...]).astype(BF16)


def _proj(x, norm_g, w16, b_gate, qn_a, kn_a, qn_c, kn_c, cos_tab, sin_tab, prev_kv, *, layer, depth,
          w_a, w_b, w_c, d_model):
    n = x.shape[0]
    tm, tn = PROJ_TM, PROJ_TN
    n_in = w16.shape[2]
    assert n % tm == 0 and n_in % tn == 0 and w_a % tn == 0 and w_b == tn and w_c == tn
    assert cos_tab.shape[0] % tm == 0
    tab_tiles = cos_tab.shape[0] // tm
    na = w_a // tn
    widths = [na, na, na, 1, 1, 1, 1, 1, 1, N_BRANCH * d_model // tn]
    bounds, start = {}, 0
    for name, wd in zip(_PROJ_NAMES, widths):
        bounds[name] = (start, start + wd)
        start += wd
    assert start == n_in // tn

    def col_block(name):
        lo, hi = bounds[name]
        return lambda j: jnp.clip(j - lo, 0, hi - lo - 1)

    tab_map = lambda i, j: (i % tab_tiles, 0)
    g_col = col_block("g")
    vec = lambda width: pl.BlockSpec((1, width), lambda i, j: (0, 0))
    in_specs = [
        pl.BlockSpec((tm, d_model), lambda i, j: (i, 0)),
        vec(d_model),
        pl.BlockSpec((None, d_model, tn), lambda i, j: (layer, 0, j)),
        pl.BlockSpec((1, tn), lambda i, j: (0, g_col(j))),
        vec(HEAD_DIM), vec(HEAD_DIM), vec(HEAD_DIM), vec(HEAD_DIM),
        pl.BlockSpec((tm, HEAD_DIM), tab_map),
        pl.BlockSpec((tm, HEAD_DIM), tab_map),
    ]
    out_widths = [w_a, w_a, w_a, w_b, w_b, w_b, w_c, w_c, w_c, N_BRANCH * d_model]
    out_specs, out_shape = [], []
    for idx, (name, wd) in enumerate(zip(_PROJ_NAMES, out_widths)):
        cb = col_block(name)
        if idx in _PROJ_KV:
            out_specs.append(pl.BlockSpec((None, tm, tn), lambda i, j, cb=cb: (layer, i, cb(j))))
            out_shape.append(jax.ShapeDtypeStruct((depth, n, wd), F32))
        else:
            out_specs.append(pl.BlockSpec((tm, tn), lambda i, j, cb=cb: (i, cb(j))))
            out_shape.append(jax.ShapeDtypeStruct((n, wd), BF16))
    prev = [] if prev_kv is None else list(prev_kv)
    aliases = {len(in_specs) + k: _PROJ_KV[k] for k in range(len(prev))}
    in_specs = in_specs + [pl.BlockSpec(memory_space=pl.ANY)] * len(prev)
    tile2 = lambda v: jnp.concatenate([v, v]).reshape(1, HEAD_DIM)
    outs = pl.pallas_call(
        functools.partial(_proj_kernel, bounds=bounds, n_prev=len(prev)),
        grid=(n // tm, n_in // tn),
        in_specs=in_specs,
        out_specs=out_specs,
        out_shape=out_shape,
        scratch_shapes=[pltpu.VMEM((tm, d_model), BF16)],
        input_output_aliases=aliases,
        compiler_params=_params(("arbitrary", "arbitrary")),
        name="proj",
    )(x, norm_g.reshape(1, -1), w16, b_gate.reshape(1, -1), tile2(qn_a), tile2(kn_a),
      qn_c.reshape(1, -1), kn_c.reshape(1, -1), cos_tab, sin_tab, *prev)
    return outs[0], outs[3], outs[6], outs[9], [outs[k] for k in _PROJ_KV]


def _head_rows(ref, h, heads, start=0, size=None):
    size = ref.shape[0] // heads if size is None else size
    return ref[pl.ds(start * heads + h, size, stride=heads), :]


def _stack_maps(q):
    lane = lax.broadcasted_iota(jnp.int32, q.shape, 1)
    zero = jnp.zeros_like(q)
    return jnp.concatenate([jnp.where(lane < D_HALF, q, zero), jnp.where(lane >= D_HALF, q, zero)], axis=0)


def _diff_lambda(lq1_ref, lk1_ref, lq2_ref, lk2_ref, lam_init):
    a = jnp.exp(jnp.sum(lq1_ref[...] * lk1_ref[...], axis=1, keepdims=True))
    b = jnp.exp(jnp.sum(lq2_ref[...] * lk2_ref[...], axis=1, keepdims=True))
    return a - b + lam_init


def _diff_finish(l, acc, t, lam, subln, lam_init):
    o1 = acc[:t] / l[:t]
    o2 = acc[t:] / l[t:]
    o = o1 - lam * o2
    ms = jnp.mean(o * o, axis=-1, keepdims=True)
    return o * lax.rsqrt(ms + EPS) * subln * (1.0 - lam_init)


def _attn_a_prompt_kernel(q_ref, k_ref, v_ref, lq1_ref, lk1_ref, lq2_ref, lk2_ref, sg_ref, o_ref,
                          s_ref, m_ref, l_ref, acc_ref, *, tq, lam_init):
    qi = pl.program_id(2)
    qq = _stack_maps(q_ref[...])

    def lane_fold(op, x, acc):
        for c in range(x.shape[1] // LANES):
            acc = op(acc, x[:, c * LANES:(c + 1) * LANES])
        return acc

    def score_block(kb, masked):
        start = pl.multiple_of(kb * tq, tq)
        s = _dot_nt(qq, k_ref[pl.ds(start, tq), :].astype(BF16))
        if masked:
            row = lax.broadcasted_iota(jnp.int32, s.shape, 0)
            col = lax.broadcasted_iota(jnp.int32, s.shape, 1)
            s = jnp.where(col // CHUNK <= (row % tq) // CHUNK, s, -jnp.inf)
        s_ref[kb] = s
        m_ref[...] = lane_fold(jnp.maximum, s, m_ref[...])

    def score_body(kb, carry):
        score_block(kb, False)
        return carry

    m_ref[...] = jnp.full(m_ref.shape, -jnp.inf, F32)
    lax.fori_loop(0, qi, score_body, 0)
    score_block(qi, True)
    m = jnp.max(m_ref[...], axis=1, keepdims=True)

    def value_body(kb, carry):
        start = pl.multiple_of(kb * tq, tq)
        p = jnp.exp(s_ref[kb] - m)
        l_ref[...] = lane_fold(jnp.add, p, l_ref[...])
        acc_ref[...] += _dot(p.astype(BF16), v_ref[pl.ds(start, tq), :].astype(BF16))
        return carry

    l_ref[...] = jnp.zeros(l_ref.shape, F32)
    acc_ref[...] = jnp.zeros(acc_ref.shape, F32)
    lax.fori_loop(0, qi + 1, value_body, 0)
    l = jnp.sum(l_ref[...], axis=1, keepdims=True)
    lam = _diff_lambda(lq1_ref, lk1_ref, lq2_ref, lk2_ref, lam_init)
    o_ref[...] = _diff_finish(l, acc_ref[...], tq, lam, sg_ref[...], lam_init).astype(o_ref.dtype)


def _lam_specs():
    spec = pl.BlockSpec((1, D_HALF), lambda *_: (0, 0))
    return [spec, spec, spec, spec, pl.BlockSpec((1, HEAD_DIM), lambda *_: (0, 0))]


def _lam_args(lq1, lk1, lq2, lk2, subln):
    return (lq1.reshape(1, -1), lk1.reshape(1, -1), lq2.reshape(1, -1), lk2.reshape(1, -1),
            subln.reshape(1, -1))


def _attn_a_prompt(q16, k32, v32, lam_args, *, layer, batch, seq, heads):
    n = q16.shape[0]
    tq = ATTN_A_TQ
    assert seq % tq == 0 and tq % CHUNK == 0
    nq = seq // tq
    kv_spec = pl.BlockSpec((None, seq, HEAD_DIM), lambda b, h, qi: (layer, b, h))
    return pl.pallas_call(
        functools.partial(_attn_a_prompt_kernel, tq=tq, lam_init=_lambda_init(layer)),
        grid=(batch, heads, nq),
        in_specs=[pl.BlockSpec((tq, HEAD_DIM), lambda b, h, qi: (b * nq + qi, h)), kv_spec, kv_spec]
                 + _lam_specs(),
        out_specs=pl.BlockSpec((tq, HEAD_DIM), lambda b, h, qi: (b * nq + qi, h)),
        out_shape=jax.ShapeDtypeStruct((n, heads * HEAD_DIM), BF16),
        scratch_shapes=[pltpu.VMEM((nq, 2 * tq, tq), F32),
                        pltpu.VMEM((2 * tq, LANES), F32), pltpu.VMEM((2 * tq, LANES), F32),
                        pltpu.VMEM((2 * tq, HEAD_DIM), F32)],
        compiler_params=_params(("arbitrary", "arbitrary", "arbitrary")),
        name="attn_a_prompt",
    )(q16, k32, v32, *lam_args)


def _attn_a_sample_kernel(q_ref, ck_ref, cv_ref, kn_ref, vn_ref, lq1_ref, lk1_ref, lq2_ref, lk2_ref, sg_ref,
                          o_ref, qq_ref, s_ref, sn_ref, m_ref, mf_ref, l_ref, acc_ref, *,
                          heads, new_len, lam_init):
    phase = pl.program_id(1)
    kt = pl.program_id(2)
    nkt = pl.num_programs(2)

    def lane_fold(op, x, acc):
        for c in range(x.shape[1] // LANES):
            acc = op(acc, x[:, c * LANES:(c + 1) * LANES])
        return acc

    @pl.when(jnp.logical_and(phase == 0, kt == 0))
    def _():
        for h in range(heads):
            sl = slice(h * HEAD_DIM, (h + 1) * HEAD_DIM)
            qq = _stack_maps(q_ref[:, sl])
            qq_ref[h] = qq
            s_new = _dot_nt(qq, kn_ref[:, sl].astype(BF16))
            sn_ref[h] = s_new
            m_ref[h] = jnp.broadcast_to(jnp.max(s_new, axis=1, keepdims=True), m_ref.shape[1:])

    @pl.when(phase == 0)
    def _():
        for h in range(heads):
            s = _dot_nt(qq_ref[h], _head_rows(ck_ref, h, heads).astype(BF16))
            s_ref[h, kt] = s
            m_ref[h] = lane_fold(jnp.maximum, s, m_ref[h])

    @pl.when(jnp.logical_and(phase == 1, kt == 0))
    def _():
        lane = lax.broadcasted_iota(jnp.int32, l_ref.shape[1:], 1)
        for h in range(heads):
            sl = slice(h * HEAD_DIM, (h + 1) * HEAD_DIM)
            m = jnp.max(m_ref[h], axis=1, keepdims=True)
            mf_ref[h] = m
            p_new = jnp.exp(sn_ref[h] - m)
            l_ref[h] = jnp.where(lane == 0, jnp.sum(p_new, axis=1, keepdims=True), 0.0)
            acc_ref[h] = _dot(p_new.astype(BF16), vn_ref[:, sl].astype(BF16))

    @pl.when(phase == 1)
    def _():
        for h in range(heads):
            p = jnp.exp(s_ref[h, kt] - mf_ref[h])
            l_ref[h] = lane_fold(jnp.add, p, l_ref[h])
            acc_ref[h] += _dot(p.astype(BF16), _head_rows(cv_ref, h, heads).astype(BF16))

    @pl.when(jnp.logical_and(phase == 1, kt == nkt - 1))
    def _():
        lam = _diff_lambda(lq1_ref, lk1_ref, lq2_ref, lk2_ref, lam_init)
        for h in range(heads):
            l = jnp.sum(l_ref[h], axis=1, keepdims=True)
            y = _diff_finish(l, acc_ref[h], new_len, lam, sg_ref[...], lam_init)
            o_ref[:, h * HEAD_DIM:(h + 1) * HEAD_DIM] = y.astype(o_ref.dtype)


def _attn_a_sample(q16, cache_k, cache_v, k32, v32, lam_args, *, layer, streams, new_len, heads):
    past = cache_k.shape[2] // heads
    tk = SAMPLE_A_TK
    assert past % tk == 0
    nkt = past // tk
    width = heads * HEAD_DIM
    q_spec = pl.BlockSpec((new_len, width), lambda b, p, kt: (b, 0))
    new_spec = pl.BlockSpec((None, new_len, width), lambda b, p, kt: (layer, b, 0))
    k_spec = pl.BlockSpec((None, None, tk * heads, HEAD_DIM),
                          lambda b, p, kt: (layer, b, jnp.where(p == 0, kt, nkt - 1), 0))
    v_spec = pl.BlockSpec((None, None, tk * heads, HEAD_DIM),
                          lambda b, p, kt: (layer, b, jnp.where(p == 0, 0, kt), 0))
    rows = 2 * new_len
    return pl.pallas_call(
        functools.partial(_attn_a_sample_kernel, heads=heads, new_len=new_len,
                          lam_init=_lambda_init(layer)),
        grid=(streams, 2, nkt),
        in_specs=[q_spec, k_spec, v_spec, new_spec, new_spec] + _lam_specs(),
        out_specs=q_spec,
        out_shape=jax.ShapeDtypeStruct(q16.shape, BF16),
        scratch_shapes=[pltpu.VMEM((heads, rows, HEAD_DIM), BF16),
                        pltpu.VMEM((heads, nkt, rows, tk), F32),
                        pltpu.VMEM((heads, rows, new_len), F32),
                        pltpu.VMEM((heads, rows, LANES), F32),
                        pltpu.VMEM((heads, rows, 1), F32),
                        pltpu.VMEM((heads, rows, LANES), F32),
                        pltpu.VMEM((heads, rows, HEAD_DIM), F32)],
        compiler_params=_params(("arbitrary", "arbitrary", "arbitrary")),
        name="attn_a_sample",
    )(q16, cache_k, cache_v, k32, v32, *lam_args)


def _upper_ones(t):
    r = lax.broadcasted_iota(jnp.int32, (t, t), 0)
    c = lax.broadcasted_iota(jnp.int32, (t, t), 1)
    return (r > c).astype(BF16)


def _sb_block(q, kblk, vblk, ones_u, c, acc, scale, strict):
    z = _dot_nt(q, kblk) * scale
    sp = _softplus(z)
    log_stay = -sp
    if strict is not None:
        log_stay = jnp.where(strict, log_stay, 0.0)
    between = _dot_f32acc(log_stay, ones_u)
    w = jnp.exp((z - sp) + between + c)
    if strict is not None:
        w = jnp.where(strict, w, 0.0)
    acc = acc + _dot(w.astype(BF16), vblk)
    c = c + jnp.sum(log_stay, axis=1, keepdims=True)
    return c, acc


def _attn_b_prompt_kernel(q_ref, k_ref, v_ref, o_ref, c_ref, acc_ref, *, tq, tk):
    qi = pl.program_id(2)
    q = q_ref[...]
    scale = HEAD_DIM ** -0.5
    ones_u = _upper_ones(tk)
    per_q = tq // tk
    c_ref[...] = jnp.zeros(c_ref.shape, F32)
    acc_ref[...] = jnp.zeros(acc_ref.shape, F32)

    def block(kb, masked):
        start = pl.multiple_of(kb * tk, tk)
        strict = None
        if masked:
            row = lax.broadcasted_iota(jnp.int32, (tq, tk), 0)
            col = lax.broadcasted_iota(jnp.int32, (tq, tk), 1)
            strict = start + col < qi * tq + row
        c, acc = _sb_block(q, k_ref[pl.ds(start, tk), :].astype(BF16), v_ref[pl.ds(start, tk), :].astype(BF16),
                           ones_u, c_ref[...], acc_ref[...], scale, strict)
        c_ref[...] = c
        acc_ref[...] = acc

    for d in reversed(range(per_q)):
        block(qi * per_q + d, True)

    def body(t, carry):
        block(qi * per_q - 1 - t, False)
        return carry

    lax.fori_loop(0, qi * per_q, body, 0)
    o_ref[...] = acc_ref[...].astype(o_ref.dtype)


def _attn_b_prompt(q16, k32, v32, *, layer, batch, seq, heads):
    n = q16.shape[0]
    tq, tk = ATTN_B_TQ, SB_SUB
    assert seq % tq == 0 and tq % tk == 0
    nq = seq // tq
    kv_spec = pl.BlockSpec((None, seq, HEAD_DIM), lambda b, h, qi: (layer, b, h))
    return pl.pallas_call(
        functools.partial(_attn_b_prompt_kernel, tq=tq, tk=tk),
        grid=(batch, heads, nq),
        in_specs=[pl.BlockSpec((tq, HEAD_DIM), lambda b, h, qi: (b * nq + qi, h)), kv_spec, kv_spec],
        out_specs=pl.BlockSpec((tq, HEAD_DIM), lambda b, h, qi: (b * nq + qi, h)),
        out_shape=jax.ShapeDtypeStruct((n, heads * HEAD_DIM), BF16),
        scratch_shapes=[pltpu.VMEM((tq, 1), F32), pltpu.VMEM((tq, HEAD_DIM), F32)],
        compiler_params=_params(("arbitrary", "arbitrary", "arbitrary")),
        name="attn_b_prompt",
    )(q16, k32, v32)


def _attn_b_sample_kernel(q_ref, ck_ref, cv_ref, kn_ref, vn_ref, o_ref, u_ref, c_ref, acc_ref, *,
                          heads, new_len):
    b = pl.program_id(0)
    kt = pl.program_id(1)
    nkt = pl.num_programs(1)
    scale = HEAD_DIM ** -0.5
    tk = ck_ref.shape[0] // heads
    head_cols = lambda h: slice(h * HEAD_DIM, (h + 1) * HEAD_DIM)
    head_rows = lambda h: slice(h * new_len, (h + 1) * new_len)

    @pl.when(jnp.logical_and(b == 0, kt == 0))
    def _():
        u_ref[...] = _upper_ones(tk)

    @pl.when(kt == 0)
    def _():
        ones_new = _upper_ones(new_len)
        row = lax.broadcasted_iota(jnp.int32, (new_len, new_len), 0)
        col = lax.broadcasted_iota(jnp.int32, (new_len, new_len), 1)
        for h in range(heads):
            sl = head_cols(h)
            c, acc = _sb_block(q_ref[:, sl], kn_ref[:, sl].astype(BF16), vn_ref[:, sl].astype(BF16), ones_new,
                               jnp.zeros((new_len, 1), F32), jnp.zeros((new_len, HEAD_DIM), F32),
                               scale, col < row)
            c_ref[head_rows(h), :] = c
            acc_ref[h] = acc

    z = jnp.concatenate([_dot_nt(q_ref[:, head_cols(h)], _head_rows(ck_ref, h, heads).astype(BF16))
                         for h in range(heads)], axis=0) * scale
    sp = _softplus(z)
    log_stay = -sp
    between = _dot_f32acc(log_stay, u_ref[...])
    w = jnp.exp((z - sp) + between + c_ref[...]).astype(BF16)
    for h in range(heads):
        acc_ref[h] += _dot(w[head_rows(h), :], _head_rows(cv_ref, h, heads).astype(BF16))
    c_ref[...] += jnp.sum(log_stay, axis=1, keepdims=True)

    @pl.when(kt == nkt - 1)
    def _():
        for h in range(heads):
            o_ref[:, head_cols(h)] = acc_ref[h].astype(o_ref.dtype)


def _attn_b_sample(q16, cache_k, cache_v, k32, v32, *, layer, streams, new_len, heads):
    past = cache_k.shape[2] // heads
    tk = SAMPLE_TK
    nkt = past // tk
    assert past % tk == 0
    width = heads * HEAD_DIM
    q_spec = pl.BlockSpec((new_len, width), lambda b, kt: (b, 0))
    new_spec = pl.BlockSpec((None, new_len, width), lambda b, kt: (layer, b, 0))
    cache_spec = pl.BlockSpec((None, None, tk * heads, HEAD_DIM), lambda b, kt: (layer, b, nkt - 1 - kt, 0))
    return pl.pallas_call(
        functools.partial(_attn_b_sample_kernel, heads=heads, new_len=new_len),
        grid=(streams, nkt),
        in_specs=[q_spec, cache_spec, cache_spec, new_spec, new_spec],
        out_specs=q_spec,
        out_shape=jax.ShapeDtypeStruct(q16.shape, BF16),
        scratch_shapes=[pltpu.VMEM((tk, tk), BF16),
                        pltpu.VMEM((heads * new_len, 1), F32),
                        pltpu.VMEM((heads, new_len, HEAD_DIM), F32)],
        compiler_params=_params(("arbitrary", "arbitrary")),
        name="attn_b_sample",
    )(q16, cache_k, cache_v, k32, v32)


def _attn_c_prompt_kernel(q_ref, k_ref, v_ref, bgrp_ref, o_ref, k16_ref, v16_ref, *, seq):
    k16_ref[0:C_WIN, :] = jnp.zeros((C_WIN, HEAD_DIM), BF16)
    v16_ref[0:C_WIN, :] = jnp.zeros((C_WIN, HEAD_DIM), BF16)
    k16_ref[C_WIN:, :] = k_ref[...].astype(BF16)
    v16_ref[C_WIN:, :] = v_ref[...].astype(BF16)
    scale = HEAD_DIM ** -0.5
    key_chunk = lax.broadcasted_iota(jnp.int32, (BAND_ROWS, BAND_KEYS), 1) // CHUNK

    def group(gi):
        r0 = pl.multiple_of(gi * BAND_ROWS, BAND_ROWS)
        s = _dot_nt(q_ref[pl.ds(r0, BAND_ROWS), :], k16_ref[pl.ds(r0, BAND_KEYS), :]) * scale + bgrp_ref[0]
        s = jnp.where(key_chunk + gi * BAND_GROUP >= C_BAND_CHUNKS, s, -jnp.inf)
        m = jnp.max(s, axis=1, keepdims=True)
        p = jnp.exp(s - m)
        l = jnp.sum(p, axis=1, keepdims=True)
        o = _dot(p.astype(BF16), v16_ref[pl.ds(r0, BAND_KEYS), :]) / l
        o_ref[pl.ds(r0, BAND_ROWS), :] = o.astype(o_ref.dtype)

    def body(t, carry):
        group(2 * t)
        group(2 * t + 1)
        return carry

    lax.fori_loop(0, seq // BAND_ROWS // 2, body, 0)


def _attn_c_prompt(q16, k32, v32, bias_grp, *, layer, batch, seq, heads):
    n = q16.shape[0]
    assert seq % (2 * BAND_ROWS) == 0
    kv_spec = pl.BlockSpec((None, seq, HEAD_DIM), lambda b, h: (layer, b, h))
    return pl.pallas_call(
        functools.partial(_attn_c_prompt_kernel, seq=seq),
        grid=(batch, heads),
        in_specs=[pl.BlockSpec((seq, HEAD_DIM), lambda b, h: (b, h)), kv_spec, kv_spec,
                  pl.BlockSpec((1, BAND_ROWS, BAND_KEYS), lambda b, h: (layer * heads + h, 0, 0))],
        out_specs=pl.BlockSpec((seq, HEAD_DIM), lambda b, h: (b, h)),
        out_shape=jax.ShapeDtypeStruct((n, heads * HEAD_DIM), BF16),
        scratch_shapes=[pltpu.VMEM((seq + C_WIN, HEAD_DIM), BF16),
                        pltpu.VMEM((seq + C_WIN, HEAD_DIM), BF16)],
        compiler_params=_params(("arbitrary", "arbitrary")),
        name="attn_c_prompt",
    )(q16, k32, v32, bias_grp)


def _attn_c_sample_kernel(q_ref, ck_ref, cv_ref, kn_ref, vn_ref, bfar_ref, bown_ref, o_ref, *, heads):
    scale = HEAD_DIM ** -0.5
    for h in range(heads):
        sl = slice(h * HEAD_DIM, (h + 1) * HEAD_DIM)
        q = q_ref[:, sl]
        s_far = _dot_nt(q, _head_rows(ck_ref, h, heads).astype(BF16)) * scale + bfar_ref[h]
        s_own = _dot_nt(q, kn_ref[:, sl].astype(BF16)) * scale + bown_ref[h]
        m = jnp.maximum(jnp.max(s_far, axis=1, keepdims=True), jnp.max(s_own, axis=1, keepdims=True))
        p_far = jnp.exp(s_far - m)
        p_own = jnp.exp(s_own - m)
        l = jnp.sum(p_far, axis=1, keepdims=True) + jnp.sum(p_own, axis=1, keepdims=True)
        o = (_dot(p_far.astype(BF16), _head_rows(cv_ref, h, heads).astype(BF16))
             + _dot(p_own.astype(BF16), vn_ref[:, sl].astype(BF16))) / l
        o_ref[:, sl] = o.astype(o_ref.dtype)


def _attn_c_sample(q16, cache_k, cache_v, k32, v32, bias_far, bias_own, *, layer, streams, new_len, heads):
    assert cache_k.shape[2] == C_WIN * heads and new_len == CHUNK
    width = heads * HEAD_DIM
    q_spec = pl.BlockSpec((new_len, width), lambda b: (b, 0))
    new_spec = pl.BlockSpec((None, new_len, width), lambda b: (layer, b, 0))
    cache_spec = pl.BlockSpec((None, None, C_WIN * heads, HEAD_DIM), lambda b: (layer, b, 0, 0))
    return pl.pallas_call(
        functools.partial(_attn_c_sample_kernel, heads=heads),
        grid=(streams,),
        in_specs=[q_spec, cache_spec, cache_spec, new_spec, new_spec,
                  pl.BlockSpec((heads, CHUNK, C_WIN), lambda b: (layer, 0, 0)),
                  pl.BlockSpec((heads, CHUNK, CHUNK), lambda b: (layer, 0, 0))],
        out_specs=q_spec,
        out_shape=jax.ShapeDtypeStruct(q16.shape, BF16),
        compiler_params=_params(("arbitrary",)),
        name="attn_c_sample",
    )(q16, cache_k, cache_v, k32, v32, bias_far, bias_own)


def _merge_kernel(x_ref, oa_ref, ob_ref, oc_ref, g_ref, wb_ref, wo_ref, o_ref, *, w_a, w_b, d_model):
    ya = _dot(oa_ref[...], wb_ref[0:w_a, :])
    yb = _dot(ob_ref[...], wb_ref[w_a:w_a + w_b, :])
    yc = _dot(oc_ref[...], wb_ref[w_a + w_b:, :])
    m = (g_ref[:, 0:d_model].astype(F32) * ya + g_ref[:, d_model:2 * d_model].astype(F32) * yb
         + g_ref[:, 2 * d_model:].astype(F32) * yc)
    o_ref[...] = x_ref[...] + _dot(m.astype(BF16), wo_ref[...])


def _merge(x, oa, ob, oc, g, wb16, wo16, *, layer):
    n, d_model = x.shape
    tm = MERGE_TM
    assert n % tm == 0
    w_a, w_b, w_c = oa.shape[1], ob.shape[1], oc.shape[1]
    rows = lambda width: pl.BlockSpec((tm, width), lambda i: (i, 0))
    resident = lambda shape: pl.BlockSpec((None,) + shape[1:], lambda i: (layer, 0, 0),
                                          pipeline_mode=pl.Buffered(1))
    return pl.pallas_call(
        functools.partial(_merge_kernel, w_a=w_a, w_b=w_b, d_model=d_model),
        grid=(n // tm,),
        in_specs=[rows(d_model), rows(w_a), rows(w_b), rows(w_c), rows(N_BRANCH * d_model),
                  resident(wb16.shape), resident(wo16.shape)],
        out_specs=rows(d_model),
        out_shape=jax.ShapeDtypeStruct((n, d_model), F32),
        compiler_params=_params(("arbitrary",)),
        name="merge",
    )(x, oa, ob, oc, g, wb16, wo16)


_R_E1, _R_E2, _R_C1, _R_C2, _R_RANK1, _R_RANK2 = range(6)


def _route_kernel(xp_ref, xs_ref, ng_ref, wr_ref, br_ref, xn_ref, route_ref, cnt_ref, carry_ref, *,
                  prompt_tiles):
    i = pl.program_id(0)
    tm = xp_ref.shape[0]
    lane = lax.broadcasted_iota(jnp.int32, (tm, LANES), 1)
    lane_f = lane.astype(F32)

    @pl.when(i == 0)
    def _():
        carry_ref[...] = jnp.zeros(carry_ref.shape, F32)

    x = jnp.where(i < prompt_tiles, xp_ref[...], xs_ref[...])
    ms = jnp.mean(x * x, axis=-1, keepdims=True)
    xn = x * lax.rsqrt(ms + EPS) * ng_ref[...]
    xn_ref[...] = xn

    x_hi, x_lo = _split_bf16(xn)
    w_hi, w_lo = _split_bf16(wr_ref[...])
    logits = _dot(x_hi, w_hi) + _dot(x_lo, w_hi) + _dot(x_hi, w_lo) + br_ref[...]
    is_group = lane < N_GROUPS
    gl = jnp.where(is_group, logits, -jnp.inf)
    g_max = jnp.max(gl, axis=1, keepdims=True)
    g_idx = jnp.min(jnp.where(gl == g_max, lane_f, float(LANES)), axis=1, keepdims=True)
    p_group = 1.0 / jnp.sum(jnp.where(is_group, jnp.exp(logits - g_max), 0.0), axis=1, keepdims=True)
    in_group = jnp.logical_and(lane >= N_GROUPS, lane < N_GROUPS + N_EXP)
    in_group = jnp.logical_and(in_group, ((lane - N_GROUPS) // E_PER_GROUP).astype(F32) == g_idx)
    el = jnp.where(in_group, logits, -jnp.inf)
    v1 = jnp.max(el, axis=1, keepdims=True)
    i1 = jnp.min(jnp.where(el == v1, lane_f, float(LANES)), axis=1, keepdims=True)
    el2 = jnp.where(lane_f == i1, -jnp.inf, el)
    v2 = jnp.max(el2, axis=1, keepdims=True)
    i2 = jnp.min(jnp.where(el2 == v2, lane_f, float(LANES)), axis=1, keepdims=True)
    t = jnp.exp(v2 - v1)
    c1 = p_group / (1.0 + t)
    c2 = p_group * t / (1.0 + t)
    e1 = i1 - N_GROUPS
    e2 = i2 - N_GROUPS

    a1 = (lane_f == e1).astype(F32)
    a2 = (lane_f == e2).astype(F32)
    a = a1 + a2
    row = lax.broadcasted_iota(jnp.int32, (tm, tm), 0)
    col = lax.broadcasted_iota(jnp.int32, (tm, tm), 1)
    earlier = (col < row).astype(BF16)
    before = _dot(earlier, a.astype(BF16)) + carry_ref[...]
    r1 = jnp.sum(a1 * before, axis=1, keepdims=True)
    r2 = jnp.sum(a2 * before, axis=1, keepdims=True)
    carry_ref[...] += jnp.sum(a, axis=0, keepdims=True)
    cnt_ref[...] = carry_ref[...]

    record = jnp.zeros((tm, LANES), F32)
    for slot, val in ((_R_E1, e1), (_R_E2, e2), (_R_C1, c1), (_R_C2, c2), (_R_RANK1, r1), (_R_RANK2, r2)):
        record = jnp.where(lane == slot, val, record)
    route_ref[...] = record


def _route(xp, xs, norm_g, w_router, b_router):
    tm = MOE_ROUTE_TM
    (n_p, d_model), n_s = xp.shape, xs.shape[0]
    assert n_p % tm == 0 and n_s % tm == 0
    pt, st = n_p // tm, n_s // tm
    n = n_p + n_s
    const = lambda shape: pl.BlockSpec(shape, lambda i: (0, 0))
    return pl.pallas_call(
        functools.partial(_route_kernel, prompt_tiles=pt),
        grid=(pt + st,),
        in_specs=[pl.BlockSpec((tm, d_model), lambda i: (jnp.minimum(i, pt - 1), 0)),
                  pl.BlockSpec((tm, d_model), lambda i: (jnp.maximum(i - pt, 0), 0)),
                  const((1, d_model)), const((d_model, LANES)), const((1, LANES))],
        out_specs=[pl.BlockSpec((tm, d_model), lambda i: (i, 0)),
                   pl.BlockSpec((tm, LANES), lambda i: (i, 0)),
                   const((1, LANES))],
        out_shape=[jax.ShapeDtypeStruct((n, d_model), F32), jax.ShapeDtypeStruct((n, LANES), F32),
                   jax.ShapeDtypeStruct((1, LANES), F32)],
        scratch_shapes=[pltpu.VMEM((1, LANES), F32)],
        compiler_params=_params(("arbitrary",)),
        name="moe_route",
    )(xp, xs, norm_g.reshape(1, -1), w_router, b_router)


def _row_copy(src, src_row, dst, dst_row, sem):
    return pltpu.make_async_copy(src.at[pl.ds(src_row, 1), :], dst.at[pl.ds(dst_row, 1), :], sem)


def _dispatch_kernel(pos1_ref, pos2_ref, xn_hbm, init_hbm, sorted_hbm, sem, *, tile):
    del init_hbm
    base = pl.program_id(0) * tile

    def start(r, carry):
        tok = base + r
        _row_copy(xn_hbm, tok, sorted_hbm, pos1_ref[tok], sem).start()
        _row_copy(xn_hbm, tok, sorted_hbm, pos2_ref[tok], sem).start()
        return carry

    def wait(r, carry):
        _row_copy(xn_hbm, 0, sorted_hbm, 0, sem).wait()
        _row_copy(xn_hbm, 0, sorted_hbm, 0, sem).wait()
        return carry

    lax.fori_loop(0, tile, start, 0, unroll=8)
    lax.fori_loop(0, tile, wait, 0, unroll=8)


def _dispatch(pos1, pos2, xn, sorted_init):
    tile = MOE_TOK_TILE
    n = xn.shape[0]
    assert n % tile == 0
    any_spec = pl.BlockSpec(memory_space=pl.ANY)
    return pl.pallas_call(
        functools.partial(_dispatch_kernel, tile=tile),
        grid_spec=pltpu.PrefetchScalarGridSpec(
            num_scalar_prefetch=2, grid=(n // tile,),
            in_specs=[any_spec, any_spec], out_specs=any_spec,
            scratch_shapes=[pltpu.SemaphoreType.DMA(())]),
        out_shape=jax.ShapeDtypeStruct(sorted_init.shape, sorted_init.dtype),
        input_output_aliases={3: 0},
        compiler_params=_params(("arbitrary",)),
        name="moe_dispatch",
    )(pos1, pos2, xn, sorted_init)


def _expert_kernel(tile_expert_ref, n_tiles_ref, x_ref, wg_ref, wu_ref, wd_ref, y_ref):
    del tile_expert_ref
    t = pl.program_id(0)

    @pl.when(t < n_tiles_ref[0])
    def _():
        x = x_ref[...].astype(BF16)
        h = jax.nn.silu(_dot(x, wg_ref[...])) * _dot(x, wu_ref[...])
        y_ref[...] = _dot(h.astype(BF16), wd_ref[...])

    @pl.when(t >= n_tiles_ref[0])
    def _():
        y_ref[...] = jnp.zeros(y_ref.shape, y_ref.dtype)


def _experts(tile_expert, n_tiles, x_sorted, wg16, wu16, wd16, *, layer):
    tile = MOE_ROW_TILE
    rows, d_model = x_sorted.shape
    d_exp = wg16.shape[-1]
    w_in = pl.BlockSpec((None, None, d_model, d_exp), lambda t, te, nt: (layer, te[t], 0, 0))
    w_out = pl.BlockSpec((None, None, d_exp, d_model), lambda t, te, nt: (layer, te[t], 0, 0))
    row_spec = pl.BlockSpec((tile, d_model), lambda t, te, nt: (t, 0))
    return pl.pallas_call(
        _expert_kernel,
        grid_spec=pltpu.PrefetchScalarGridSpec(
            num_scalar_prefetch=2, grid=(rows // tile,),
            in_specs=[row_spec, w_in, w_in, w_out], out_specs=row_spec),
        out_shape=jax.ShapeDtypeStruct((rows, d_model), F32),
        compiler_params=_params(("arbitrary",)),
        name="moe_experts",
    )(tile_expert, n_tiles, x_sorted, wg16, wu16, wd16)


def _combine_kernel(pos1_ref, pos2_ref, xp_ref, xs_ref, route_ref, y_hbm, op_ref, os_ref, ybuf, sem, *,
                    tile, prompt_tiles):
    i = pl.program_id(0)
    base = i * tile

    def start(r, carry):
        tok = base + r
        _row_copy(y_hbm, pos1_ref[tok], ybuf.at[0], r, sem).start()
        _row_copy(y_hbm, pos2_ref[tok], ybuf.at[1], r, sem).start()
        return carry

    def wait(r, carry):
        _row_copy(y_hbm, 0, ybuf.at[0], 0, sem).wait()
        _row_copy(y_hbm, 0, ybuf.at[1], 0, sem).wait()
        return carry

    lax.fori_loop(0, tile, start, 0, unroll=8)
    lax.fori_loop(0, tile, wait, 0, unroll=8)
    y = route_ref[:, _R_C1:_R_C1 + 1] * ybuf[0] + route_ref[:, _R_C2:_R_C2 + 1] * ybuf[1]

    @pl.when(i < prompt_tiles)
    def _():
        op_ref[...] = xp_ref[...] + y

    @pl.when(i >= prompt_tiles)
    def _():
        os_ref[...] = xs_ref[...] + y


def _combine(pos1, pos2, xp, xs, route, y_sorted):
    tile = MOE_TOK_TILE
    (n_p, d_model), n_s = xp.shape, xs.shape[0]
    assert n_p % tile == 0 and n_s % tile == 0
    pt, st = n_p // tile, n_s // tile
    p_spec = pl.BlockSpec((tile, d_model), lambda i, p1, p2: (jnp.minimum(i, pt - 1), 0))
    s_spec = pl.BlockSpec((tile, d_model), lambda i, p1, p2: (jnp.maximum(i - pt, 0), 0))
    return pl.pallas_call(
        functools.partial(_combine_kernel, tile=tile, prompt_tiles=pt),
        grid_spec=pltpu.PrefetchScalarGridSpec(
            num_scalar_prefetch=2, grid=(pt + st,),
            in_specs=[p_spec, s_spec, pl.BlockSpec((tile, LANES), lambda i, p1, p2: (i, 0)),
                      pl.BlockSpec(memory_space=pl.ANY)],
            out_specs=[p_spec, s_spec],
            scratch_shapes=[pltpu.VMEM((2, tile, d_model), F32), pltpu.SemaphoreType.DMA(())]),
        out_shape=[jax.ShapeDtypeStruct(xp.shape, F32), jax.ShapeDtypeStruct(xs.shape, F32)],
        compiler_params=_params(("arbitrary",)),
        name="moe_combine",
    )(pos1, pos2, xp, xs, route, y_sorted)


def _moe(xp, xs, norm_g, w_router, b_router, wg16, wu16, wd16, *, layer):
    n = xp.shape[0] + xs.shape[0]
    tile = MOE_ROW_TILE
    n_exp = wg16.shape[1]
    row_tiles = (2 * n) // tile + n_exp
    xn, route, counts = _route(xp, xs, norm_g, w_router, b_router)

    counts = counts[0, :n_exp].astype(jnp.int32)
    padded = ((counts + tile - 1) // tile) * tile
    ends = jnp.cumsum(padded)
    offsets = ends - padded
    as_int = lambda lane_idx: route[:, lane_idx].astype(jnp.int32)
    pos1 = offsets[as_int(_R_E1)] + as_int(_R_RANK1)
    pos2 = offsets[as_int(_R_E2)] + as_int(_R_RANK2)
    tile_expert = jnp.minimum(jnp.searchsorted(ends, jnp.arange(row_tiles, dtype=jnp.int32) * tile,
                                               side="right"), n_exp - 1).astype(jnp.int32)
    n_tiles = (ends[-1:] // tile).astype(jnp.int32)

    x_sorted = _dispatch(pos1, pos2, xn, jnp.zeros((row_tiles * tile, xn.shape[1]), F32))
    y_sorted = _experts(tile_expert, n_tiles, x_sorted, wg16, wu16, wd16, layer=layer)
    return _combine(pos1, pos2, xp, xs, route, y_sorted)


def _rope_tables(pos):
    half = D_HALF // 2
    inv = ROPE_THETA ** (-2.0 * jnp.arange(half, dtype=F32) / D_HALF)
    ang = pos.astype(F32)[:, None] * inv[None, :]
    cos, sin = jnp.cos(ang), jnp.sin(ang)
    cos_full = jnp.tile(cos, (1, HEAD_DIM // half))
    sin_signed = jnp.tile(jnp.concatenate([-sin, sin], axis=1), (1, HEAD_DIM // D_HALF))
    return cos_full, sin_signed


def kernel(x_prompt, x_sample, cache_a_k, cache_a_v, cache_b_k, cache_b_v, cache_c_k, cache_c_v, norm_mix, w_in, b_gate, q_norm_a, k_norm_a, lam_q1, lam_k1, lam_q2, lam_k2, subln_a, q_norm_c, k_norm_c, rel_bias_c, w_branch, w_out, norm_ffn, w_group, b_group, w_expert_router, b_expert_router, w_gate_e, w_up_e, w_down_e):
    batch, seq, d_model = x_prompt.shape
    streams, new_len, _ = x_sample.shape
    depth, _, past, h_a, _ = cache_a_k.shape
    h_b, h_c = cache_b_k.shape[3], cache_c_k.shape[3]
    w_a, w_b, w_c = h_a * HEAD_DIM, h_b * HEAD_DIM, h_c * HEAD_DIM
    n_prompt, n_sample = batch * seq, streams * new_len
    tm = PROJ_TM
    assert seq % tm == 0 and n_sample % tm == 0 and tm % new_len == 0
    c_rows = min(C_WIN, seq)

    xp = x_prompt.reshape(n_prompt, d_model)
    xs = x_sample.reshape(n_sample, d_model)
    tab_p = _rope_tables(jnp.arange(seq))
    tab_s = _rope_tables(jnp.tile(past + jnp.arange(new_len), tm // new_len))
    bias_far, bias_own, bias_grp = _expand_bias(rel_bias_c.reshape(depth * h_c, N_REL))
    rows_view = lambda c: c.reshape(c.shape[0], c.shape[1], c.shape[2] * c.shape[3], c.shape[4])
    ca_k, ca_v, cb_k, cb_v, cc_k, cc_v = map(rows_view, (cache_a_k, cache_a_v, cache_b_k, cache_b_v,
                                                          cache_c_k, cache_c_v))

    w_in16, wb16, wo16 = w_in.astype(BF16), w_branch.astype(BF16), w_out.astype(BF16)
    wg16, wu16, wd16 = w_gate_e.astype(BF16), w_up_e.astype(BF16), w_down_e.astype(BF16)
    kv_p = kv_s = None
    for l in range(depth):
        proj_w = (norm_mix[l], w_in16, b_gate[l], q_norm_a[l], k_norm_a[l], q_norm_c[l],
                  k_norm_c[l])
        proj_kw = dict(layer=l, depth=depth, w_a=w_a, w_b=w_b, w_c=w_c, d_model=d_model)
        qa_p, qb_p, qc_p, g_p, kv_p = _proj(xp, *proj_w, *tab_p, kv_p, **proj_kw)
        qa_s, qb_s, qc_s, g_s, kv_s = _proj(xs, *proj_w, *tab_s, kv_s, **proj_kw)
        lam_args = _lam_args(lam_q1[l], lam_k1[l], lam_q2[l], lam_k2[l], subln_a[l])

        oa_p = _attn_a_prompt(qa_p, kv_p[0], kv_p[1], lam_args, layer=l, batch=batch, seq=seq, heads=h_a)
        ob_p = _attn_b_prompt(qb_p, kv_p[2], kv_p[3], layer=l, batch=batch, seq=seq, heads=h_b)
        oc_p = _attn_c_prompt(qc_p, kv_p[4], kv_p[5], bias_grp, layer=l, batch=batch, seq=seq, heads=h_c)
        sample_kw = dict(layer=l, streams=streams, new_len=new_len)
        oa_s = _attn_a_sample(qa_s, ca_k, ca_v, kv_s[0], kv_s[1], lam_args, heads=h_a, **sample_kw)
        ob_s = _attn_b_sample(qb_s, cb_k, cb_v, kv_s[2], kv_s[3], heads=h_b, **sample_kw)
        oc_s = _attn_c_sample(qc_s, cc_k, cc_v, kv_s[4], kv_s[5], bias_far, bias_own, heads=h_c, **sample_kw)

        xp = _merge(xp, oa_p, ob_p, oc_p, g_p, wb16, wo16, layer=l)
        xs = _merge(xs, oa_s, ob_s, oc_s, g_s, wb16, wo16, layer=l)

        w_router = jnp.concatenate(
            [w_group[l], jnp.transpose(w_expert_router[l], (1, 0, 2)).reshape(d_model, N_EXP)], axis=1)
        w_router = jnp.pad(w_router, ((0, 0), (0, LANES - w_router.shape[1])))
        b_router = jnp.pad(jnp.concatenate([b_group[l], b_expert_router[l].reshape(-1)]),
                           (0, LANES - N_GROUPS - N_EXP)).reshape(1, LANES)
        xp, xs = _moe(xp, xs, norm_ffn[l], w_router, b_router, wg16, wu16, wd16, layer=l)

    heads_of = [h_a, h_a, h_b, h_b, h_c, h_c]
    prompt_kv = [t.reshape(depth, batch, seq, h, HEAD_DIM) for t, h in zip(kv_p, heads_of)]
    prompt_kv[4] = prompt_kv[4][:, :, seq - c_rows:]
    prompt_kv[5] = prompt_kv[5][:, :, seq - c_rows:]
    sample_kv = [t.reshape(depth, streams, new_len, h, HEAD_DIM) for t, h in zip(kv_s, heads_of)]
    return (xp.reshape(batch, seq, d_model), xs.reshape(streams, new_len, d_model), *prompt_kv, *sample_kv)
```

```python
import functools
import math

import jax
import jax.numpy as jnp
from jax import lax
from jax.experimental import pallas as pl
from jax.experimental.pallas import tpu as pltpu

F32 = jnp.float32
BF16 = jnp.bfloat16

CHUNK = 64
HEAD_DIM = 128
D_HALF = HEAD_DIM // 2
C_BAND_CHUNKS = 8
C_WIN = C_BAND_CHUNKS * CHUNK
MAX_REL = 128
N_REL = 2 * MAX_REL + 1
ROPE_THETA = 10000.0
N_GROUPS = 4
E_PER_GROUP = 4
N_EXP = N_GROUPS * E_PER_GROUP
EPS = 1e-6
N_BRANCH = 3

LANES = 128
VMEM_LIMIT = 56 * 1024 * 1024

PROJ_TM = 512
PROJ_TN = 512
ATTN_A_TQ = 512
ATTN_B_TQ = 512
SB_SUB = 256
SAMPLE_TK = 512
SAMPLE_A_TK = 1024
BAND_GROUP = 4
BAND_ROWS = BAND_GROUP * CHUNK
BAND_KEYS = (BAND_GROUP + C_BAND_CHUNKS) * CHUNK
MERGE_TM = 256
MOE_ROUTE_TM = 512
MOE_ROW_TILE = 256
MOE_TOK_TILE = 256


def _lambda_init(layer_idx):
    return 0.8 - 0.6 * math.exp(-0.3 * layer_idx)


def _dot(a, b):
    return jnp.dot(a, b, preferred_element_type=F32)


def _dot_nt(a, b):
    return lax.dot_general(a, b, (((1,), (1,)), ((), ())), preferred_element_type=F32)


def _split_bf16(x):
    hi = x.astype(BF16)
    lo = (x - hi.astype(F32)).astype(BF16)
    return hi, lo


def _dot_f32acc(x, w_bf16):
    hi, lo = _split_bf16(x)
    return _dot(hi, w_bf16) + _dot(lo, w_bf16)


def _softplus(z):
    return jnp.maximum(z, 0.0) + jnp.log(1.0 + jnp.exp(-jnp.abs(z)))


def _params(sem, vmem=VMEM_LIMIT):
    return pltpu.CompilerParams(dimension_semantics=sem, vmem_limit_bytes=vmem)


def _bias_kernel(tab_ref, far_ref, own_ref, grp_ref):
    r = pl.program_id(0)
    half = BAND_KEYS - 2 * CHUNK

    def rel_index(shape, key_shift):
        qi = lax.broadcasted_iota(jnp.int32, shape, 0)
        kj = lax.broadcasted_iota(jnp.int32, shape, 1) - key_shift
        return jnp.clip(qi + C_WIN - kj, -MAX_REL, MAX_REL) + MAX_REL, kj

    idx_even, kj_even = rel_index((CHUNK, half), 0)
    idx_odd, kj_odd = rel_index((CHUNK, half), CHUNK)

    def body(t, carry):
        val = tab_ref[r, t]
        return tuple(jnp.where(idx == t, val, acc) for idx, acc in zip((idx_even, idx_odd), carry))

    zeros = jnp.zeros((CHUNK, half), F32)
    even, odd = lax.fori_loop(0, N_REL, body, (zeros, zeros))
    far_ref[0] = even[:, :C_WIN]
    own_ref[0] = even[:, C_WIN:C_WIN + CHUNK]
    band = lambda kj: jnp.logical_and(kj >= 0, kj < C_WIN + CHUNK)
    even = jnp.where(band(kj_even), even, -jnp.inf)
    odd = jnp.where(band(kj_odd), odd, -jnp.inf)
    for a in range(BAND_GROUP):
        lead = (a // 2) * 2 * CHUNK
        tail = BAND_KEYS - lead - half
        pieces = [even if a % 2 == 0 else odd]
        if lead:
            pieces.insert(0, jnp.full((CHUNK, lead), -jnp.inf, F32))
        if tail:
            pieces.append(jnp.full((CHUNK, tail), -jnp.inf, F32))
        grp_ref[0, a * CHUNK:(a + 1) * CHUNK, :] = jnp.concatenate(pieces, axis=1)


def _expand_bias(table):
    rows = table.shape[0]
    shapes = [(CHUNK, C_WIN), (CHUNK, CHUNK), (BAND_ROWS, BAND_KEYS)]
    return pl.pallas_call(
        _bias_kernel,
        grid=(rows,),
        in_specs=[pl.BlockSpec(memory_space=pltpu.SMEM)],
        out_specs=[pl.BlockSpec((1,) + s, lambda r: (r, 0, 0)) for s in shapes],
        out_shape=[jax.ShapeDtypeStruct((rows,) + s, F32) for s in shapes],
        compiler_params=_params(("arbitrary",)),
        name="bias_expand",
    )(table)


_PROJ_NAMES = ["qa", "ka", "va", "qb", "kb", "vb", "qc", "kc", "vc", "g"]
_PROJ_KV = [1, 2, 4, 5, 7, 8]


def _proj_kernel(*refs, bounds, n_prev):
    (x_ref, ng_ref, w_ref, bg_ref, qna_ref, kna_ref, qnc_ref, knc_ref, cos_ref, sin_ref) = refs[:10]
    (qa_ref, ka_ref, va_ref, qb_ref, kb_ref, vb_ref, qc_ref, kc_ref, vc_ref, g_ref,
     xn_ref) = refs[10 + n_prev:]
    j = pl.program_id(1)

    @pl.when(j == 0)
    def _():
        x = x_ref[...]
        ms = jnp.mean(x * x, axis=-1, keepdims=True)
        xn_ref[...] = (x * lax.rsqrt(ms + EPS) * ng_ref[...]).astype(BF16)

    z = _dot(xn_ref[...], w_ref[...])
    tm, tn = z.shape
    heads = tn // HEAD_DIM

    r_i = lax.broadcasted_iota(jnp.int32, (HEAD_DIM, HEAD_DIM), 0)
    c_i = lax.broadcasted_iota(jnp.int32, (HEAD_DIM, HEAD_DIM), 1)
    ones_map = ((r_i // D_HALF) == (c_i // D_HALF)).astype(BF16)
    ones_head = jnp.ones((HEAD_DIM, HEAD_DIM), BF16)
    lane = lax.broadcasted_iota(jnp.int32, (tm, HEAD_DIM), 1)
    first_half = (lane % D_HALF) < (D_HALF // 2)

    def map_norm_rope(zh, gain):
        ms = _dot_f32acc(zh * zh, ones_map) * (1.0 / D_HALF)
        y = zh * lax.rsqrt(ms + EPS) * gain
        partner = jnp.where(first_half,
                            pltpu.roll(y, HEAD_DIM - D_HALF // 2, 1),
                            pltpu.roll(y, D_HALF // 2, 1))
        return y * cos_ref[...] + partner * sin_ref[...]

    def head_norm(zh, gain):
        ms = _dot_f32acc(zh * zh, ones_head) * (1.0 / HEAD_DIM)
        return zh * lax.rsqrt(ms + EPS) * gain

    def per_head(fn, out_ref, scale=None):
        for hh in range(heads):
            sl = slice(hh * HEAD_DIM, (hh + 1) * HEAD_DIM)
            y = fn(z[:, sl])
            if scale is not None:
                y = y * scale
            out_ref[:, sl] = y.astype(out_ref.dtype)

    def in_range(name):
        lo, hi = bounds[name]
        return jnp.logical_and(j >= lo, j < hi)

    @pl.when(in_range("qa"))
    def _():
        per_head(lambda zh: map_norm_rope(zh, qna_ref[...]), qa_ref, scale=D_HALF ** -0.5)

    @pl.when(in_range("ka"))
    def _():
        per_head(lambda zh: map_norm_rope(zh, kna_ref[...]), ka_ref)

    @pl.when(in_range("va"))
    def _():
        va_ref[...] = z

    @pl.when(in_range("qb"))
    def _():
        qb_ref[...] = z.astype(BF16)

    @pl.when(in_range("kb"))
    def _():
        kb_ref[...] = z

    @pl.when(in_range("vb"))
    def _():
        vb_ref[...] = z

    @pl.when(in_range("qc"))
    def _():
        per_head(lambda zh: head_norm(zh, qnc_ref[...]), qc_ref)

    @pl.when(in_range("kc"))
    def _():
        per_head(lambda zh: head_norm(zh, knc_ref[...]), kc_ref)

    @pl.when(in_range("vc"))
    def _():
        vc_ref[...] = z

    @pl.when(in_range("g"))
    def _():
        g_ref[...] = jax.nn.sigmoid(z + bg_ref[...]).astype(BF16)


def _proj(x, norm_g, w16, b_gate, qn_a, kn_a, qn_c, kn_c, cos_tab, sin_tab, prev_kv, *, layer, depth,
          w_a, w_b, w_c, d_model):
    n = x.shape[0]
    tm, tn = PROJ_TM, PROJ_TN
    n_in = w16.shape[2]
    assert n % tm == 0 and n_in % tn == 0 and w_a % tn == 0 and w_b == tn and w_c == tn
    assert cos_tab.shape[0] % tm == 0
    tab_tiles = cos_tab.shape[0] // tm
    na = w_a // tn
    widths = [na, na, na, 1, 1, 1, 1, 1, 1, N_BRANCH * d_model // tn]
    bounds, start = {}, 0
    for name, wd in zip(_PROJ_NAMES, widths):
        bounds[name] = (start, start + wd)
        start += wd
    assert start == n_in // tn

    def col_block(name):
        lo, hi = bounds[name]
        return lambda j: jnp.clip(j - lo, 0, hi - lo - 1)

    tab_map = lambda i, j: (i % tab_tiles, 0)
    g_col = col_block("g")
    vec = lambda width: pl.BlockSpec((1, width), lambda i, j: (0, 0))
    in_specs = [
        pl.BlockSpec((tm, d_model), lambda i, j: (i, 0)),
        vec(d_model),
        pl.BlockSpec((None, d_model, tn), lambda i, j: (layer, 0, j)),
        pl.BlockSpec((1, tn), lambda i, j: (0, g_col(j))),
        vec(HEAD_DIM), vec(HEAD_DIM), vec(HEAD_DIM), vec(HEAD_DIM),
        pl.BlockSpec((tm, HEAD_DIM), tab_map),
        pl.BlockSpec((tm, HEAD_DIM), tab_map),
    ]
    out_widths = [w_a, w_a, w_a, w_b, w_b, w_b, w_c, w_c, w_c, N_BRANCH * d_model]
    out_specs, out_shape = [], []
    for idx, (name, wd) in enumerate(zip(_PROJ_NAMES, out_widths)):
        cb = col_block(name)
        if idx in _PROJ_KV:
            out_specs.append(pl.BlockSpec((None, tm, tn), lambda i, j, cb=cb: (layer, i, cb(j))))
            out_shape.append(jax.ShapeDtypeStruct((depth, n, wd), F32))
        else:
            out_specs.append(pl.BlockSpec((tm, tn), lambda i, j, cb=cb: (i, cb(j))))
            out_shape.append(jax.ShapeDtypeStruct((n, wd), BF16))
    prev = [] if prev_kv is None else list(prev_kv)
    aliases = {len(in_specs) + k: _PROJ_KV[k] for k in range(len(prev))}
    in_specs = in_specs + [pl.BlockSpec(memory_space=pl.ANY)] * len(prev)
    tile2 = lambda v: jnp.concatenate([v, v]).reshape(1, HEAD_DIM)
    outs = pl.pallas_call(
        functools.partial(_proj_kernel, bounds=bounds, n_prev=len(prev)),
        grid=(n // tm, n_in // tn),
        in_specs=in_specs,
        out_specs=out_specs,
        out_shape=out_shape,
        scratch_shapes=[pltpu.VMEM((tm, d_model), BF16)],
        input_output_aliases=aliases,
        compiler_params=_params(("arbitrary", "arbitrary")),
        name="proj",
    )(x, norm_g.reshape(1, -1), w16, b_gate.reshape(1, -1), tile2(qn_a), tile2(kn_a),
      qn_c.reshape(1, -1), kn_c.reshape(1, -1), cos_tab, sin_tab, *prev)
    return outs[0], outs[3], outs[6], outs[9], [outs[k] for k in _PROJ_KV]


def _head_rows(ref, h, heads, start=0, size=None):
    size = ref.shape[0] // heads if size is None else size
    return ref[pl.ds(start * heads + h, size, stride=heads), :]


def _stack_maps(q):
    lane = lax.broadcasted_iota(jnp.int32, q.shape, 1)
    zero = jnp.zeros_like(q)
    return jnp.concatenate([jnp.where(lane < D_HALF, q, zero), jnp.where(lane >= D_HALF, q, zero)], axis=0)


def _diff_lambda(lq1_ref, lk1_ref, lq2_ref, lk2_ref, lam_init):
    a = jnp.exp(jnp.sum(lq1_ref[...] * lk1_ref[...], axis=1, keepdims=True))
    b = jnp.exp(jnp.sum(lq2_ref[...] * lk2_ref[...], axis=1, keepdims=True))
    return a - b + lam_init


def _diff_finish(l, acc, t, lam, subln, lam_init):
    o1 = acc[:t] / l[:t]
    o2 = acc[t:] / l[t:]
    o = o1 - lam * o2
    ms = jnp.mean(o * o, axis=-1, keepdims=True)
    return o * lax.rsqrt(ms + EPS) * subln * (1.0 - lam_init)


def _attn_a_prompt_kernel(q_ref, k_ref, v_ref, lq1_ref, lk1_ref, lq2_ref, lk2_ref, sg_ref, o_ref,
                          s_ref, m_ref, l_ref, acc_ref, *, tq, lam_init):
    qi = pl.program_id(2)
    qq = _stack_maps(q_ref[...])

    def lane_fold(op, x, acc):
        for c in range(x.shape[1] // LANES):
            acc = op(acc, x[:, c * LANES:(c + 1) * LANES])
        return acc

    def score_block(kb, masked):
        start = pl.multiple_of(kb * tq, tq)
        s = _dot_nt(qq, k_ref[pl.ds(start, tq), :].astype(BF16))
        if masked:
            row = lax.broadcasted_iota(jnp.int32, s.shape, 0)
            col = lax.broadcasted_iota(jnp.int32, s.shape, 1)
            s = jnp.where(col // CHUNK <= (row % tq) // CHUNK, s, -jnp.inf)
        s_ref[kb] = s
        m_ref[...] = lane_fold(jnp.maximum, s, m_ref[...])

    def score_body(kb, carry):
        score_block(kb, False)
        return carry

    m_ref[...] = jnp.full(m_ref.shape, -jnp.inf, F32)
    lax.fori_loop(0, qi, score_body, 0)
    score_block(qi, True)
    m = jnp.max(m_ref[...], axis=1, keepdims=True)

    def value_body(kb, carry):
        start = pl.multiple_of(kb * tq, tq)
        p = jnp.exp(s_ref[kb] - m)
        l_ref[...] = lane_fold(jnp.add, p, l_ref[...])
        acc_ref[...] += _dot(p.astype(BF16), v_ref[pl.ds(start, tq), :].astype(BF16))
        return carry

    l_ref[...] = jnp.zeros(l_ref.shape, F32)
    acc_ref[...] = jnp.zeros(acc_ref.shape, F32)
    lax.fori_loop(0, qi + 1, value_body, 0)
    l = jnp.sum(l_ref[...], axis=1, keepdims=True)
    lam = _diff_lambda(lq1_ref, lk1_ref, lq2_ref, lk2_ref, lam_init)
    o_ref[...] = _diff_finish(l, acc_ref[...], tq, lam, sg_ref[...], lam_init).astype(o_ref.dtype)


def _lam_specs():
    spec = pl.BlockSpec((1, D_HALF), lambda *_: (0, 0))
    return [spec, spec, spec, spec, pl.BlockSpec((1, HEAD_DIM), lambda *_: (0, 0))]


def _lam_args(lq1, lk1, lq2, lk2, subln):
    return (lq1.reshape(1, -1), lk1.reshape(1, -1), lq2.reshape(1, -1), lk2.reshape(1, -1),
            subln.reshape(1, -1))


def _attn_a_prompt(q16, k32, v32, lam_args, *, layer, batch, seq, heads):
    n = q16.shape[0]
    tq = ATTN_A_TQ
    assert seq % tq == 0 and tq % CHUNK == 0
    nq = seq // tq
    kv_spec = pl.BlockSpec((None, seq, HEAD_DIM), lambda b, h, qi: (layer, b, h))
    return pl.pallas_call(
        functools.partial(_attn_a_prompt_kernel, tq=tq, lam_init=_lambda_init(layer)),
        grid=(batch, heads, nq),
        in_specs=[pl.BlockSpec((tq, HEAD_DIM), lambda b, h, qi: (b * nq + qi, h)), kv_spec, kv_spec]
                 + _lam_specs(),
        out_specs=pl.BlockSpec((tq, HEAD_DIM), lambda b, h, qi: (b * nq + qi, h)),
        out_shape=jax.ShapeDtypeStruct((n, heads * HEAD_DIM), BF16),
        scratch_shapes=[pltpu.VMEM((nq, 2 * tq, tq), F32),
                        pltpu.VMEM((2 * tq, LANES), F32), pltpu.VMEM((2 * tq, LANES), F32),
                        pltpu.VMEM((2 * tq, HEAD_DIM), F32)],
        compiler_params=_params(("arbitrary", "arbitrary", "arbitrary")),
        name="attn_a_prompt",
    )(q16, k32, v32, *lam_args)


def _attn_a_sample_kernel(q_ref, ck_ref, cv_ref, kn_ref, vn_ref, lq1_ref, lk1_ref, lq2_ref, lk2_ref, sg_ref,
                          o_ref, qq_ref, s_ref, sn_ref, m_ref, mf_ref, l_ref, acc_ref, *,
                          heads, new_len, lam_init):
    phase = pl.program_id(1)
    kt = pl.program_id(2)
    nkt = pl.num_programs(2)

    def lane_fold(op, x, acc):
        for c in range(x.shape[1] // LANES):
            acc = op(acc, x[:, c * LANES:(c + 1) * LANES])
        return acc

    @pl.when(jnp.logical_and(phase == 0, kt == 0))
    def _():
        for h in range(heads):
            sl = slice(h * HEAD_DIM, (h + 1) * HEAD_DIM)
            qq = _stack_maps(q_ref[:, sl])
            qq_ref[h] = qq
            s_new = _dot_nt(qq, kn_ref[:, sl].astype(BF16))
            sn_ref[h] = s_new
            m_ref[h] = jnp.broadcast_to(jnp.max(s_new, axis=1, keepdims=True), m_ref.shape[1:])

    @pl.when(phase == 0)
    def _():
        for h in range(heads):
            s = _dot_nt(qq_ref[h], _head_rows(ck_ref, h, heads).astype(BF16))
            s_ref[h, kt] = s
            m_ref[h] = lane_fold(jnp.maximum, s, m_ref[h])

    @pl.when(jnp.logical_and(phase == 1, kt == 0))
    def _():
        lane = lax.broadcasted_iota(jnp.int32, l_ref.shape[1:], 1)
        for h in range(heads):
            sl = slice(h * HEAD_DIM, (h + 1) * HEAD_DIM)
            m = jnp.max(m_ref[h], axis=1, keepdims=True)
            mf_ref[h] = m
            p_new = jnp.exp(sn_ref[h] - m)
            l_ref[h] = jnp.where(lane == 0, jnp.sum(p_new, axis=1, keepdims=True), 0.0)
            acc_ref[h] = _dot(p_new.astype(BF16), vn_ref[:, sl].astype(BF16))

    @pl.when(phase == 1)
    def _():
        for h in range(heads):
            p = jnp.exp(s_ref[h, kt] - mf_ref[h])
            l_ref[h] = lane_fold(jnp.add, p, l_ref[h])
            acc_ref[h] += _dot(p.astype(BF16), _head_rows(cv_ref, h, heads).astype(BF16))

    @pl.when(jnp.logical_and(phase == 1, kt == nkt - 1))
    def _():
        lam = _diff_lambda(lq1_ref, lk1_ref, lq2_ref, lk2_ref, lam_init)
        for h in range(heads):
            l = jnp.sum(l_ref[h], axis=1, keepdims=True)
            y = _diff_finish(l, acc_ref[h], new_len, lam, sg_ref[...], lam_init)
            o_ref[:, h * HEAD_DIM:(h + 1) * HEAD_DIM] = y.astype(o_ref.dtype)


def _attn_a_sample(q16, cache_k, cache_v, k32, v32, lam_args, *, layer, streams, new_len, heads):
    past = cache_k.shape[2] // heads
    tk = SAMPLE_A_TK
    assert past % tk == 0
    nkt = past // tk
    width = heads * HEAD_DIM
    q_spec = pl.BlockSpec((new_len, width), lambda b, p, kt: (b, 0))
    new_spec = pl.BlockSpec((None, new_len, width), lambda b, p, kt: (layer, b, 0))
    k_spec = pl.BlockSpec((None, None, tk * heads, HEAD_DIM),
                          lambda b, p, kt: (layer, b, jnp.where(p == 0, kt, nkt - 1), 0))
    v_spec = pl.BlockSpec((None, None, tk * heads, HEAD_DIM),
                          lambda b, p, kt: (layer, b, jnp.where(p == 0, 0, kt), 0))
    rows = 2 * new_len
    return pl.pallas_call(
        functools.partial(_attn_a_sample_kernel, heads=heads, new_len=new_len,
                          lam_init=_lambda_init(layer)),
        grid=(streams, 2, nkt),
        in_specs=[q_spec, k_spec, v_spec, new_spec, new_spec] + _lam_specs(),
        out_specs=q_spec,
        out_shape=jax.ShapeDtypeStruct(q16.shape, BF16),
        scratch_shapes=[pltpu.VMEM((heads, rows, HEAD_DIM), BF16),
                        pltpu.VMEM((heads, nkt, rows, tk), F32),
                        pltpu.VMEM((heads, rows, new_len), F32),
                        pltpu.VMEM((heads, rows, LANES), F32),
                        pltpu.VMEM((heads, rows, 1), F32),
                        pltpu.VMEM((heads, rows, LANES), F32),
                        pltpu.VMEM((heads, rows, HEAD_DIM), F32)],
        compiler_params=_params(("arbitrary", "arbitrary", "arbitrary")),
        name="attn_a_sample",
    )(q16, cache_k, cache_v, k32, v32, *lam_args)


def _upper_ones(t):
    r = lax.broadcasted_iota(jnp.int32, (t, t), 0)
    c = lax.broadcasted_iota(jnp.int32, (t, t), 1)
    return (r > c).astype(BF16)


def _sb_block(q, kblk, vblk, ones_u, c, acc, scale, strict):
    z = _dot_nt(q, kblk) * scale
    sp = _softplus(z)
    log_stay = -sp
    if strict is not None:
        log_stay = jnp.where(strict, log_stay, 0.0)
    between = _dot_f32acc(log_stay, ones_u)
    w = jnp.exp((z - sp) + between + c)
    if strict is not None:
        w = jnp.where(strict, w, 0.0)
    acc = acc + _dot(w.astype(BF16), vblk)
    c = c + jnp.sum(log_stay, axis=1, keepdims=True)
    return c, acc


def _attn_b_prompt_kernel(q_ref, k_ref, v_ref, o_ref, c_ref, acc_ref, *, tq, tk):
    qi = pl.program_id(2)
    q = q_ref[...]
    scale = HEAD_DIM ** -0.5
    ones_u = _upper_ones(tk)
    per_q = tq // tk
    c_ref[...] = jnp.zeros(c_ref.shape, F32)
    acc_ref[...] = jnp.zeros(acc_ref.shape, F32)

    def block(kb, masked):
        start = pl.multiple_of(kb * tk, tk)
        strict = None
        if masked:
            row = lax.broadcasted_iota(jnp.int32, (tq, tk), 0)
            col = lax.broadcasted_iota(jnp.int32, (tq, tk), 1)
            strict = start + col < qi * tq + row
        c, acc = _sb_block(q, k_ref[pl.ds(start, tk), :].astype(BF16), v_ref[pl.ds(start, tk), :].astype(BF16),
                           ones_u, c_ref[...], acc_ref[...], scale, strict)
        c_ref[...] = c
        acc_ref[...] = acc

    for d in reversed(range(per_q)):
        block(qi * per_q + d, True)

    def body(t, carry):
        block(qi * per_q - 1 - t, False)
        return carry

    lax.fori_loop(0, qi * per_q, body, 0)
    o_ref[...] = acc_ref[...].astype(o_ref.dtype)


def _attn_b_prompt(q16, k32, v32, *, layer, batch, seq, heads):
    n = q16.shape[0]
    tq, tk = ATTN_B_TQ, SB_SUB
    assert seq % tq == 0 and tq % tk == 0
    nq = seq // tq
    kv_spec = pl.BlockSpec((None, seq, HEAD_DIM), lambda b, h, qi: (layer, b, h))
    return pl.pallas_call(
        functools.partial(_attn_b_prompt_kernel, tq=tq, tk=tk),
        grid=(batch, heads, nq),
        in_specs=[pl.BlockSpec((tq, HEAD_DIM), lambda b, h, qi: (b * nq + qi, h)), kv_spec, kv_spec],
        out_specs=pl.BlockSpec((tq, HEAD_DIM), lambda b, h, qi: (b * nq + qi, h)),
        out_shape=jax.ShapeDtypeStruct((n, heads * HEAD_DIM), BF16),
        scratch_shapes=[pltpu.VMEM((tq, 1), F32), pltpu.VMEM((tq, HEAD_DIM), F32)],
        compiler_params=_params(("arbitrary", "arbitrary", "arbitrary")),
        name="attn_b_prompt",
    )(q16, k32, v32)


def _attn_b_sample_kernel(q_ref, ck_ref, cv_ref, kn_ref, vn_ref, o_ref, u_ref, c_ref, acc_ref, *,
                          heads, new_len):
    b = pl.program_id(0)
    kt = pl.program_id(1)
    nkt = pl.num_programs(1)
    scale = HEAD_DIM ** -0.5
    tk = ck_ref.shape[0] // heads
    head_cols = lambda h: slice(h * HEAD_DIM, (h + 1) * HEAD_DIM)
    head_rows = lambda h: slice(h * new_len, (h + 1) * new_len)

    @pl.when(jnp.logical_and(b == 0, kt == 0))
    def _():
        u_ref[...] = _upper_ones(tk)

    @pl.when(kt == 0)
    def _():
        ones_new = _upper_ones(new_len)
        row = lax.broadcasted_iota(jnp.int32, (new_len, new_len), 0)
        col = lax.broadcasted_iota(jnp.int32, (new_len, new_len), 1)
        for h in range(heads):
            sl = head_cols(h)
            c, acc = _sb_block(q_ref[:, sl], kn_ref[:, sl].astype(BF16), vn_ref[:, sl].astype(BF16), ones_new,
                               jnp.zeros((new_len, 1), F32), jnp.zeros((new_len, HEAD_DIM), F32),
                               scale, col < row)
            c_ref[head_rows(h), :] = c
            acc_ref[h] = acc

    z = jnp.concatenate([_dot_nt(q_ref[:, head_cols(h)], _head_rows(ck_ref, h, heads).astype(BF16))
                         for h in range(heads)], axis=0) * scale
    sp = _softplus(z)
    log_stay = -sp
    between = _dot_f32acc(log_stay, u_ref[...])
    w = jnp.exp((z - sp) + between + c_ref[...]).astype(BF16)
    for h in range(heads):
        acc_ref[h] += _dot(w[head_rows(h), :], _head_rows(cv_ref, h, heads).astype(BF16))
    c_ref[...] += jnp.sum(log_stay, axis=1, keepdims=True)

    @pl.when(kt == nkt - 1)
    def _():
        for h in range(heads):
            o_ref[:, head_cols(h)] = acc_ref[h].astype(o_ref.dtype)


def _attn_b_sample(q16, cache_k, cache_v, k32, v32, *, layer, streams, new_len, heads):
    past = cache_k.shape[2] // heads
    tk = SAMPLE_TK
    nkt = past // tk
    assert past % tk == 0
    width = heads * HEAD_DIM
    q_spec = pl.BlockSpec((new_len, width), lambda b, kt: (b, 0))
    new_spec = pl.BlockSpec((None, new_len, width), lambda b, kt: (layer, b, 0))
    cache_spec = pl.BlockSpec((None, None, tk * heads, HEAD_DIM), lambda b, kt: (layer, b, nkt - 1 - kt, 0))
    return pl.pallas_call(
        functools.partial(_attn_b_sample_kernel, heads=heads, new_len=new_len),
        grid=(streams, nkt),
        in_specs=[q_spec, cache_spec, cache_spec, new_spec, new_spec],
        out_specs=q_spec,
        out_shape=jax.ShapeDtypeStruct(q16.shape, BF16),
        scratch_shapes=[pltpu.VMEM((tk, tk), BF16),
                        pltpu.VMEM((heads * new_len, 1), F32),
                        pltpu.VMEM((heads, new_len, HEAD_DIM), F32)],
        compiler_params=_params(("arbitrary", "arbitrary")),
        name="attn_b_sample",
    )(q16, cache_k, cache_v, k32, v32)


def _attn_c_prompt_kernel(q_ref, k_ref, v_ref, bgrp_ref, o_ref, k16_ref, v16_ref, *, seq):
    k16_ref[0:C_WIN, :] = jnp.zeros((C_WIN, HEAD_DIM), BF16)
    v16_ref[0:C_WIN, :] = jnp.zeros((C_WIN, HEAD_DIM), BF16)
    k16_ref[C_WIN:, :] = k_ref[...].astype(BF16)
    v16_ref[C_WIN:, :] = v_ref[...].astype(BF16)
    scale = HEAD_DIM ** -0.5
    key_chunk = lax.broadcasted_iota(jnp.int32, (BAND_ROWS, BAND_KEYS), 1) // CHUNK

    def group(gi):
        r0 = pl.multiple_of(gi * BAND_ROWS, BAND_ROWS)
        s = _dot_nt(q_ref[pl.ds(r0, BAND_ROWS), :], k16_ref[pl.ds(r0, BAND_KEYS), :]) * scale + bgrp_ref[0]
        s = jnp.where(key_chunk + gi * BAND_GROUP >= C_BAND_CHUNKS, s, -jnp.inf)
        m = jnp.max(s, axis=1, keepdims=True)
        p = jnp.exp(s - m)
        l = jnp.sum(p, axis=1, keepdims=True)
        o = _dot(p.astype(BF16), v16_ref[pl.ds(r0, BAND_KEYS), :]) / l
        o_ref[pl.ds(r0, BAND_ROWS), :] = o.astype(o_ref.dtype)

    def body(t, carry):
        group(2 * t)
        group(2 * t + 1)
        return carry

    lax.fori_loop(0, seq // BAND_ROWS // 2, body, 0)


def _attn_c_prompt(q16, k32, v32, bias_grp, *, layer, batch, seq, heads):
    n = q16.shape[0]
    assert seq % (2 * BAND_ROWS) == 0
    kv_spec = pl.BlockSpec((None, seq, HEAD_DIM), lambda b, h: (layer, b, h))
    return pl.pallas_call(
        functools.partial(_attn_c_prompt_kernel, seq=seq),
        grid=(batch, heads),
        in_specs=[pl.BlockSpec((seq, HEAD_DIM), lambda b, h: (b, h)), kv_spec, kv_spec,
                  pl.BlockSpec((1, BAND_ROWS, BAND_KEYS), lambda b, h: (layer * heads + h, 0, 0))],
        out_specs=pl.BlockSpec((seq, HEAD_DIM), lambda b, h: (b, h)),
        out_shape=jax.ShapeDtypeStruct((n, heads * HEAD_DIM), BF16),
        scratch_shapes=[pltpu.VMEM((seq + C_WIN, HEAD_DIM), BF16),
                        pltpu.VMEM((seq + C_WIN, HEAD_DIM), BF16)],
        compiler_params=_params(("arbitrary", "arbitrary")),
        name="attn_c_prompt",
    )(q16, k32, v32, bias_grp)


def _attn_c_sample_kernel(q_ref, ck_ref, cv_ref, kn_ref, vn_ref, bfar_ref, bown_ref, o_ref, *, heads):
    scale = HEAD_DIM ** -0.5
    for h in range(heads):
        sl = slice(h * HEAD_DIM, (h + 1) * HEAD_DIM)
        q = q_ref[:, sl]
        s_far = _dot_nt(q, _head_rows(ck_ref, h, heads).astype(BF16)) * scale + bfar_ref[h]
        s_own = _dot_nt(q, kn_ref[:, sl].astype(BF16)) * scale + bown_ref[h]
        m = jnp.maximum(jnp.max(s_far, axis=1, keepdims=True), jnp.max(s_own, axis=1, keepdims=True))
        p_far = jnp.exp(s_far - m)
        p_own = jnp.exp(s_own - m)
        l = jnp.sum(p_far, axis=1, keepdims=True) + jnp.sum(p_own, axis=1, keepdims=True)
        o = (_dot(p_far.astype(BF16), _head_rows(cv_ref, h, heads).astype(BF16))
             + _dot(p_own.astype(BF16), vn_ref[:, sl].astype(BF16))) / l
        o_ref[:, sl] = o.astype(o_ref.dtype)


def _attn_c_sample(q16, cache_k, cache_v, k32, v32, bias_far, bias_own, *, layer, streams, new_len, heads):
    assert cache_k.shape[2] == C_WIN * heads and new_len == CHUNK
    width = heads * HEAD_DIM
    q_spec = pl.BlockSpec((new_len, width), lambda b: (b, 0))
    new_spec = pl.BlockSpec((None, new_len, width), lambda b: (layer, b, 0))
    cache_spec = pl.BlockSpec((None, None, C_WIN * heads, HEAD_DIM), lambda b: (layer, b, 0, 0))
    return pl.pallas_call(
        functools.partial(_attn_c_sample_kernel, heads=heads),
        grid=(streams,),
        in_specs=[q_spec, cache_spec, cache_spec, new_spec, new_spec,
                  pl.BlockSpec((heads, CHUNK, C_WIN), lambda b: (layer, 0, 0)),
                  pl.BlockSpec((heads, CHUNK, CHUNK), lambda b: (layer, 0, 0))],
        out_specs=q_spec,
        out_shape=jax.ShapeDtypeStruct(q16.shape, BF16),
        compiler_params=_params(("arbitrary",)),
        name="attn_c_sample",
    )(q16, cache_k, cache_v, k32, v32, bias_far, bias_own)


def _merge_kernel(x_ref, oa_ref, ob_ref, oc_ref, g_ref, wb_ref, wo_ref, o_ref, *, w_a, w_b, d_model):
    ya = _dot(oa_ref[...], wb_ref[0:w_a, :])
    yb = _dot(ob_ref[...], wb_ref[w_a:w_a + w_b, :])
    yc = _dot(oc_ref[...], wb_ref[w_a + w_b:, :])
    m = (g_ref[:, 0:d_model].astype(F32) * ya + g_ref[:, d_model:2 * d_model].astype(F32) * yb
         + g_ref[:, 2 * d_model:].astype(F32) * yc)
    o_ref[...] = x_ref[...] + _dot(m.astype(BF16), wo_ref[...])


def _merge(x, oa, ob, oc, g, wb16, wo16, *, layer):
    n, d_model = x.shape
    tm = MERGE_TM
    assert n % tm == 0
    w_a, w_b, w_c = oa.shape[1], ob.shape[1], oc.shape[1]
    rows = lambda width: pl.BlockSpec((tm, width), lambda i: (i, 0))
    resident = lambda shape: pl.BlockSpec((None,) + shape[1:], lambda i: (layer, 0, 0),
                                          pipeline_mode=pl.Buffered(1))
    return pl.pallas_call(
        functools.partial(_merge_kernel, w_a=w_a, w_b=w_b, d_model=d_model),
        grid=(n // tm,),
        in_specs=[rows(d_model), rows(w_a), rows(w_b), rows(w_c), rows(N_BRANCH * d_model),
                  resident(wb16.shape), resident(wo16.shape)],
        out_specs=rows(d_model),
        out_shape=jax.ShapeDtypeStruct((n, d_model), F32),
        compiler_params=_params(("arbitrary",)),
        name="merge",
    )(x, oa, ob, oc, g, wb16, wo16)


_R_E1, _R_E2, _R_C1, _R_C2, _R_RANK1, _R_RANK2 = range(6)


def _route_kernel(xp_ref, xs_ref, ng_ref, wr_ref, br_ref, xn_ref, route_ref, cnt_ref, carry_ref, *,
                  prompt_tiles):
    i = pl.program_id(0)
    tm = xp_ref.shape[0]
    lane = lax.broadcasted_iota(jnp.int32, (tm, LANES), 1)
    lane_f = lane.astype(F32)

    @pl.when(i == 0)
    def _():
        carry_ref[...] = jnp.zeros(carry_ref.shape, F32)

    x = jnp.where(i < prompt_tiles, xp_ref[...], xs_ref[...])
    ms = jnp.mean(x * x, axis=-1, keepdims=True)
    xn = x * lax.rsqrt(ms + EPS) * ng_ref[...]
    xn_ref[...] = xn

    x_hi, x_lo = _split_bf16(xn)
    w_hi, w_lo = _split_bf16(wr_ref[...])
    logits = _dot(x_hi, w_hi) + _dot(x_lo, w_hi) + _dot(x_hi, w_lo) + br_ref[...]
    is_group = lane < N_GROUPS
    gl = jnp.where(is_group, logits, -jnp.inf)
    g_max = jnp.max(gl, axis=1, keepdims=True)
    g_idx = jnp.min(jnp.where(gl == g_max, lane_f, float(LANES)), axis=1, keepdims=True)
    p_group = 1.0 / jnp.sum(jnp.where(is_group, jnp.exp(logits - g_max), 0.0), axis=1, keepdims=True)
    in_group = jnp.logical_and(lane >= N_GROUPS, lane < N_GROUPS + N_EXP)
    in_group = jnp.logical_and(in_group, ((lane - N_GROUPS) // E_PER_GROUP).astype(F32) == g_idx)
    el = jnp.where(in_group, logits, -jnp.inf)
    v1 = jnp.max(el, axis=1, keepdims=True)
    i1 = jnp.min(jnp.where(el == v1, lane_f, float(LANES)), axis=1, keepdims=True)
    el2 = jnp.where(lane_f == i1, -jnp.inf, el)
    v2 = jnp.max(el2, axis=1, keepdims=True)
    i2 = jnp.min(jnp.where(el2 == v2, lane_f, float(LANES)), axis=1, keepdims=True)
    t = jnp.exp(v2 - v1)
    c1 = p_group / (1.0 + t)
    c2 = p_group * t / (1.0 + t)
    e1 = i1 - N_GROUPS
    e2 = i2 - N_GROUPS

    a1 = (lane_f == e1).astype(F32)
    a2 = (lane_f == e2).astype(F32)
    a = a1 + a2
    row = lax.broadcasted_iota(jnp.int32, (tm, tm), 0)
    col = lax.broadcasted_iota(jnp.int32, (tm, tm), 1)
    earlier = (col < row).astype(BF16)
    before = _dot(earlier, a.astype(BF16)) + carry_ref[...]
    r1 = jnp.sum(a1 * before, axis=1, keepdims=True)
    r2 = jnp.sum(a2 * before, axis=1, keepdims=True)
    carry_ref[...] += jnp.sum(a, axis=0, keepdims=True)
    cnt_ref[...] = carry_ref[...]

    record = jnp.zeros((tm, LANES), F32)
    for slot, val in ((_R_E1, e1), (_R_E2, e2), (_R_C1, c1), (_R_C2, c2), (_R_RANK1, r1), (_R_RANK2, r2)):
        record = jnp.where(lane == slot, val, record)
    route_ref[...] = record


def _route(xp, xs, norm_g, w_router, b_router):
    tm = MOE_ROUTE_TM
    (n_p, d_model), n_s = xp.shape, xs.shape[0]
    assert n_p % tm == 0 and n_s % tm == 0
    pt, st = n_p // tm, n_s // tm
    n = n_p + n_s
    const = lambda shape: pl.BlockSpec(shape, lambda i: (0, 0))
    return pl.pallas_call(
        functools.partial(_route_kernel, prompt_tiles=pt),
        grid=(pt + st,),
        in_specs=[pl.BlockSpec((tm, d_model), lambda i: (jnp.minimum(i, pt - 1), 0)),
                  pl.BlockSpec((tm, d_model), lambda i: (jnp.maximum(i - pt, 0), 0)),
                  const((1, d_model)), const((d_model, LANES)), const((1, LANES))],
        out_specs=[pl.BlockSpec((tm, d_model), lambda i: (i, 0)),
                   pl.BlockSpec((tm, LANES), lambda i: (i, 0)),
                   const((1, LANES))],
        out_shape=[jax.ShapeDtypeStruct((n, d_model), F32), jax.ShapeDtypeStruct((n, LANES), F32),
                   jax.ShapeDtypeStruct((1, LANES), F32)],
        scratch_shapes=[pltpu.VMEM((1, LANES), F32)],
        compiler_params=_params(("arbitrary",)),
        name="moe_route",
    )(xp, xs, norm_g.reshape(1, -1), w_router, b_router)


def _row_copy(src, src_row, dst, dst_row, sem):
    return pltpu.make_async_copy(src.at[pl.ds(src_row, 1), :], dst.at[pl.ds(dst_row, 1), :], sem)


def _row_token_kernel(pos1_ref, pos2_ref, out_ref, *, n_tokens):
    def init(r, carry):
        out_ref[r] = 0
        return carry

    def place(tok, carry):
        out_ref[pos1_ref[tok]] = tok
        out_ref[pos2_ref[tok]] = tok
        return carry

    lax.fori_loop(0, out_ref.shape[0], init, 0, unroll=8)
    lax.fori_loop(0, n_tokens, place, 0, unroll=8)


def _row_tokens(pos1, pos2, rows):
    return pl.pallas_call(
        functools.partial(_row_token_kernel, n_tokens=pos1.shape[0]),
        grid_spec=pltpu.PrefetchScalarGridSpec(
            num_scalar_prefetch=2, grid=(1,), in_specs=[],
            out_specs=pl.BlockSpec(memory_space=pltpu.SMEM)),
        out_shape=jax.ShapeDtypeStruct((rows,), jnp.int32),
        compiler_params=_params(("arbitrary",)),
        name="moe_row_tokens",
    )(pos1, pos2)


def _expert_kernel(tile_expert_ref, n_tiles_ref, row_token_ref, xn_hbm, wg_ref, wu_ref, wd_ref, y_ref,
                   xbuf, sem, *, tile):
    del tile_expert_ref
    t = pl.program_id(0)
    n_valid = n_tiles_ref[0]
    slot = t % 2

    def gather(tile_idx, into):
        base = tile_idx * tile

        def body(r, carry):
            _row_copy(xn_hbm, row_token_ref[base + r], xbuf.at[into], r, sem.at[into]).start()
            return carry

        lax.fori_loop(0, tile, body, 0, unroll=8)

    def wait_rows(into):
        def body(r, carry):
            _row_copy(xn_hbm, 0, xbuf.at[into], 0, sem.at[into]).wait()
            return carry

        lax.fori_loop(0, tile, body, 0, unroll=8)

    @pl.when(jnp.logical_and(t == 0, n_valid > 0))
    def _():
        gather(0, 0)

    @pl.when(t + 1 < n_valid)
    def _():
        gather(t + 1, 1 - slot)

    @pl.when(t < n_valid)
    def _():
        wait_rows(slot)
        x = xbuf[slot].astype(BF16)
        h = jax.nn.silu(_dot(x, wg_ref[...])) * _dot(x, wu_ref[...])
        y_ref[...] = _dot(h.astype(BF16), wd_ref[...])

    @pl.when(t >= n_valid)
    def _():
        y_ref[...] = jnp.zeros(y_ref.shape, y_ref.dtype)


def _experts(tile_expert, n_tiles, row_token, xn, wg16, wu16, wd16, *, layer):
    tile = MOE_ROW_TILE
    rows, d_model = row_token.shape[0], xn.shape[1]
    d_exp = wg16.shape[-1]
    w_in = pl.BlockSpec((None, None, d_model, d_exp), lambda t, te, nt, rt: (layer, te[t], 0, 0))
    w_out = pl.BlockSpec((None, None, d_exp, d_model), lambda t, te, nt, rt: (layer, te[t], 0, 0))
    return pl.pallas_call(
        functools.partial(_expert_kernel, tile=tile),
        grid_spec=pltpu.PrefetchScalarGridSpec(
            num_scalar_prefetch=3, grid=(rows // tile,),
            in_specs=[pl.BlockSpec(memory_space=pl.ANY), w_in, w_in, w_out],
            out_specs=pl.BlockSpec((tile, d_model), lambda t, te, nt, rt: (t, 0)),
            scratch_shapes=[pltpu.VMEM((2, tile, d_model), F32), pltpu.SemaphoreType.DMA((2,))]),
        out_shape=jax.ShapeDtypeStruct((rows, d_model), F32),
        compiler_params=_params(("arbitrary",)),
        name="moe_experts",
    )(tile_expert, n_tiles, row_token, xn, wg16, wu16, wd16)


def _combine_kernel(pos1_ref, pos2_ref, xp_ref, xs_ref, route_ref, y_hbm, op_ref, os_ref, ybuf, sem, *,
                    tile, prompt_tiles):
    i = pl.program_id(0)
    base = i * tile

    def start(r, carry):
        tok = base + r
        _row_copy(y_hbm, pos1_ref[tok], ybuf.at[0], r, sem).start()
        _row_copy(y_hbm, pos2_ref[tok], ybuf.at[1], r, sem).start()
        return carry

    def wait(r, carry):
        _row_copy(y_hbm, 0, ybuf.at[0], 0, sem).wait()
        _row_copy(y_hbm, 0, ybuf.at[1], 0, sem).wait()
        return carry

    lax.fori_loop(0, tile, start, 0, unroll=8)
    lax.fori_loop(0, tile, wait, 0, unroll=8)
    y = route_ref[:, _R_C1:_R_C1 + 1] * ybuf[0] + route_ref[:, _R_C2:_R_C2 + 1] * ybuf[1]

    @pl.when(i < prompt_tiles)
    def _():
        op_ref[...] = xp_ref[...] + y

    @pl.when(i >= prompt_tiles)
    def _():
        os_ref[...] = xs_ref[...] + y


def _combine(pos1, pos2, xp, xs, route, y_sorted):
    tile = MOE_TOK_TILE
    (n_p, d_model), n_s = xp.shape, xs.shape[0]
    assert n_p % tile == 0 and n_s % tile == 0
    pt, st = n_p // tile, n_s // tile
    p_spec = pl.BlockSpec((tile, d_model), lambda i, p1, p2: (jnp.minimum(i, pt - 1), 0))
    s_spec = pl.BlockSpec((tile, d_model), lambda i, p1, p2: (jnp.maximum(i - pt, 0), 0))
    return pl.pallas_call(
        functools.partial(_combine_kernel, tile=tile, prompt_tiles=pt),
        grid_spec=pltpu.PrefetchScalarGridSpec(
            num_scalar_prefetch=2, grid=(pt + st,),
            in_specs=[p_spec, s_spec, pl.BlockSpec((tile, LANES), lambda i, p1, p2: (i, 0)),
                      pl.BlockSpec(memory_space=pl.ANY)],
            out_specs=[p_spec, s_spec],
            scratch_shapes=[pltpu.VMEM((2, tile, d_model), F32), pltpu.SemaphoreType.DMA(())]),
        out_shape=[jax.ShapeDtypeStruct(xp.shape, F32), jax.ShapeDtypeStruct(xs.shape, F32)],
        compiler_params=_params(("arbitrary",)),
        name="moe_combine",
    )(pos1, pos2, xp, xs, route, y_sorted)


def _moe(xp, xs, norm_g, w_router, b_router, wg16, wu16, wd16, *, layer):
    n = xp.shape[0] + xs.shape[0]
    tile = MOE_ROW_TILE
    n_exp = wg16.shape[1]
    row_tiles = (2 * n) // tile + n_exp
    xn, route, counts = _route(xp, xs, norm_g, w_router, b_router)

    counts = counts[0, :n_exp].astype(jnp.int32)
    padded = ((counts + tile - 1) // tile) * tile
    upto = jnp.arange(n_exp)[None, :] <= jnp.arange(n_exp)[:, None]
    ends = jnp.sum(jnp.where(upto, padded[None, :], 0), axis=1)
    offsets = ends - padded
    as_int = lambda lane_idx: route[:, lane_idx].astype(jnp.int32)
    pos1 = offsets[as_int(_R_E1)] + as_int(_R_RANK1)
    pos2 = offsets[as_int(_R_E2)] + as_int(_R_RANK2)
    tile_start = jnp.arange(row_tiles, dtype=jnp.int32) * tile
    tile_expert = jnp.minimum(jnp.sum((tile_start[:, None] >= ends[None, :]).astype(jnp.int32), axis=1),
                              n_exp - 1)
    n_tiles = ends[-1:] // tile

    row_token = _row_tokens(pos1, pos2, row_tiles * tile)
    y_sorted = _experts(tile_expert, n_tiles, row_token, xn, wg16, wu16, wd16, layer=layer)
    return _combine(pos1, pos2, xp, xs, route, y_sorted)


def _rope_tables(pos):
    half = D_HALF // 2
    inv = ROPE_THETA ** (-2.0 * jnp.arange(half, dtype=F32) / D_HALF)
    ang = pos.astype(F32)[:, None] * inv[None, :]
    cos, sin = jnp.cos(ang), jnp.sin(ang)
    cos_full = jnp.tile(cos, (1, HEAD_DIM // half))
    sin_signed = jnp.tile(jnp.concatenate([-sin, sin], axis=1), (1, HEAD_DIM // D_HALF))
    return cos_full, sin_signed


def kernel(x_prompt, x_sample, cache_a_k, cache_a_v, cache_b_k, cache_b_v, cache_c_k, cache_c_v, norm_mix, w_in, b_gate, q_norm_a, k_norm_a, lam_q1, lam_k1, lam_q2, lam_k2, subln_a, q_norm_c, k_norm_c, rel_bias_c, w_branch, w_out, norm_ffn, w_group, b_group, w_expert_router, b_expert_router, w_gate_e, w_up_e, w_down_e):
    batch, seq, d_model = x_prompt.shape
    streams, new_len, _ = x_sample.shape
    depth, _, past, h_a, _ = cache_a_k.shape
    h_b, h_c = cache_b_k.shape[3], cache_c_k.shape[3]
    w_a, w_b, w_c = h_a * HEAD_DIM, h_b * HEAD_DIM, h_c * HEAD_DIM
    n_prompt, n_sample = batch * seq, streams * new_len
    tm = PROJ_TM
    assert seq % tm == 0 and n_sample % tm == 0 and tm % new_len == 0
    c_rows = min(C_WIN, seq)

    xp = x_prompt.reshape(n_prompt, d_model)
    xs = x_sample.reshape(n_sample, d_model)
    tab_p = _rope_tables(jnp.arange(seq))
    tab_s = _rope_tables(jnp.tile(past + jnp.arange(new_len), tm // new_len))
    bias_far, bias_own, bias_grp = _expand_bias(rel_bias_c.reshape(depth * h_c, N_REL))
    rows_view = lambda c: c.reshape(c.shape[0], c.shape[1], c.shape[2] * c.shape[3], c.shape[4])
    ca_k, ca_v, cb_k, cb_v, cc_k, cc_v = map(rows_view, (cache_a_k, cache_a_v, cache_b_k, cache_b_v,
                                                          cache_c_k, cache_c_v))

    w_in16, wb16, wo16 = w_in.astype(BF16), w_branch.astype(BF16), w_out.astype(BF16)
    wg16, wu16, wd16 = w_gate_e.astype(BF16), w_up_e.astype(BF16), w_down_e.astype(BF16)
    kv_p = kv_s = None
    for l in range(depth):
        proj_w = (norm_mix[l], w_in16, b_gate[l], q_norm_a[l], k_norm_a[l], q_norm_c[l],
                  k_norm_c[l])
        proj_kw = dict(layer=l, depth=depth, w_a=w_a, w_b=w_b, w_c=w_c, d_model=d_model)
        qa_p, qb_p, qc_p, g_p, kv_p = _proj(xp, *proj_w, *tab_p, kv_p, **proj_kw)
        qa_s, qb_s, qc_s, g_s, kv_s = _proj(xs, *proj_w, *tab_s, kv_s, **proj_kw)
        lam_args = _lam_args(lam_q1[l], lam_k1[l], lam_q2[l], lam_k2[l], subln_a[l])

        oa_p = _attn_a_prompt(qa_p, kv_p[0], kv_p[1], lam_args, layer=l, batch=batch, seq=seq, heads=h_a)
        ob_p = _attn_b_prompt(qb_p, kv_p[2], kv_p[3], layer=l, batch=batch, seq=seq, heads=h_b)
        oc_p = _attn_c_prompt(qc_p, kv_p[4], kv_p[5], bias_grp, layer=l, batch=batch, seq=seq, heads=h_c)
        sample_kw = dict(layer=l, streams=streams, new_len=new_len)
        oa_s = _attn_a_sample(qa_s, ca_k, ca_v, kv_s[0], kv_s[1], lam_args, heads=h_a, **sample_kw)
        ob_s = _attn_b_sample(qb_s, cb_k, cb_v, kv_s[2], kv_s[3], heads=h_b, **sample_kw)
        oc_s = _attn_c_sample(qc_s, cc_k, cc_v, kv_s[4], kv_s[5], bias_far, bias_own, heads=h_c, **sample_kw)

        xp = _merge(xp, oa_p, ob_p, oc_p, g_p, wb16, wo16, layer=l)
        xs = _merge(xs, oa_s, ob_s, oc_s, g_s, wb16, wo16, layer=l)

        w_router = jnp.concatenate(
            [w_group[l], jnp.transpose(w_expert_router[l], (1, 0, 2)).reshape(d_model, N_EXP)], axis=1)
        w_router = jnp.pad(w_router, ((0, 0), (0, LANES - w_router.shape[1])))
        b_router = jnp.pad(jnp.concatenate([b_group[l], b_expert_router[l].reshape(-1)]),
                           (0, LANES - N_GROUPS - N_EXP)).reshape(1, LANES)
        xp, xs = _moe(xp, xs, norm_ffn[l], w_router, b_router, wg16, wu16, wd16, layer=l)

    heads_of = [h_a, h_a, h_b, h_b, h_c, h_c]
    prompt_kv = [t.reshape(depth, batch, seq, h, HEAD_DIM) for t, h in zip(kv_p, heads_of)]
    prompt_kv[4] = prompt_kv[4][:, :, seq - c_rows:]
    prompt_kv[5] = prompt_kv[5][:, :, seq - c_rows:]
    sample_kv = [t.reshape(depth, streams, new_len, h, HEAD_DIM) for t, h in zip(kv_s, heads_of)]
    return (xp.reshape(batch, seq, d_model), xs.reshape(streams, new_len, d_model), *prompt_kv, *sample_kv)
```

```python
import functools
import math

import jax
import jax.numpy as jnp
from jax import lax
from jax.experimental import pallas as pl
from jax.experimental.pallas import tpu as pltpu

F32 = jnp.float32
BF16 = jnp.bfloat16

CHUNK = 64
HEAD_DIM = 128
D_HALF = HEAD_DIM // 2
C_BAND_CHUNKS = 8
C_WIN = C_BAND_CHUNKS * CHUNK
MAX_REL = 128
N_REL = 2 * MAX_REL + 1
ROPE_THETA = 10000.0
N_GROUPS = 4
E_PER_GROUP = 4
N_EXP = N_GROUPS * E_PER_GROUP
EPS = 1e-6
N_BRANCH = 3

LANES = 128
VMEM_LIMIT = 56 * 1024 * 1024

PROJ_TM = 512
PROJ_TN = 512
ATTN_A_TQ = 512
ATTN_B_TQ = 512
SB_SUB = 256
SAMPLE_TK = 512
SAMPLE_A_TK = 1024
BAND_GROUP = 4
BAND_ROWS = BAND_GROUP * CHUNK
BAND_KEYS = (BAND_GROUP + C_BAND_CHUNKS) * CHUNK
MERGE_TM = 256
MOE_ROUTE_TM = 512
MOE_ROW_TILE = 256
MOE_TOK_TILE = 256
ROW_GROUP = 8


def _lambda_init(layer_idx):
    return 0.8 - 0.6 * math.exp(-0.3 * layer_idx)


def _dot(a, b):
    return jnp.dot(a, b, preferred_element_type=F32)


def _dot_nt(a, b):
    return lax.dot_general(a, b, (((1,), (1,)), ((), ())), preferred_element_type=F32)


def _split_bf16(x):
    hi = x.astype(BF16)
    lo = (x - hi.astype(F32)).astype(BF16)
    return hi, lo


def _dot_f32acc(x, w_bf16):
    hi, lo = _split_bf16(x)
    return _dot(hi, w_bf16) + _dot(lo, w_bf16)


def _softplus(z):
    return jnp.maximum(z, 0.0) + jnp.log(1.0 + jnp.exp(-jnp.abs(z)))


def _params(sem, vmem=VMEM_LIMIT):
    return pltpu.CompilerParams(dimension_semantics=sem, vmem_limit_bytes=vmem)


def _bias_kernel(tab_ref, far_ref, own_ref, grp_ref):
    r = pl.program_id(0)
    half = BAND_KEYS - 2 * CHUNK

    def rel_index(shape, key_shift):
        qi = lax.broadcasted_iota(jnp.int32, shape, 0)
        kj = lax.broadcasted_iota(jnp.int32, shape, 1) - key_shift
        return jnp.clip(qi + C_WIN - kj, -MAX_REL, MAX_REL) + MAX_REL, kj

    idx_even, kj_even = rel_index((CHUNK, half), 0)
    idx_odd, kj_odd = rel_index((CHUNK, half), CHUNK)

    def body(t, carry):
        val = tab_ref[r, t]
        return tuple(jnp.where(idx == t, val, acc) for idx, acc in zip((idx_even, idx_odd), carry))

    zeros = jnp.zeros((CHUNK, half), F32)
    even, odd = lax.fori_loop(0, N_REL, body, (zeros, zeros))
    far_ref[0] = even[:, :C_WIN]
    own_ref[0] = even[:, C_WIN:C_WIN + CHUNK]
    band = lambda kj: jnp.logical_and(kj >= 0, kj < C_WIN + CHUNK)
    even = jnp.where(band(kj_even), even, -jnp.inf)
    odd = jnp.where(band(kj_odd), odd, -jnp.inf)
    for a in range(BAND_GROUP):
        lead = (a // 2) * 2 * CHUNK
        tail = BAND_KEYS - lead - half
        pieces = [even if a % 2 == 0 else odd]
        if lead:
            pieces.insert(0, jnp.full((CHUNK, lead), -jnp.inf, F32))
        if tail:
            pieces.append(jnp.full((CHUNK, tail), -jnp.inf, F32))
        grp_ref[0, a * CHUNK:(a + 1) * CHUNK, :] = jnp.concatenate(pieces, axis=1)


def _expand_bias(table):
    rows = table.shape[0]
    shapes = [(CHUNK, C_WIN), (CHUNK, CHUNK), (BAND_ROWS, BAND_KEYS)]
    return pl.pallas_call(
        _bias_kernel,
        grid=(rows,),
        in_specs=[pl.BlockSpec(memory_space=pltpu.SMEM)],
        out_specs=[pl.BlockSpec((1,) + s, lambda r: (r, 0, 0)) for s in shapes],
        out_shape=[jax.ShapeDtypeStruct((rows,) + s, F32) for s in shapes],
        compiler_params=_params(("arbitrary",)),
        name="bias_expand",
    )(table)


_PROJ_NAMES = ["qa", "ka", "va", "qb", "kb", "vb", "qc", "kc", "vc", "g"]
_PROJ_KV = [1, 2, 4, 5, 7, 8]


def _proj_kernel(*refs, bounds, n_prev):
    (x_ref, ng_ref, w_ref, bg_ref, qna_ref, kna_ref, qnc_ref, knc_ref, cos_ref, sin_ref) = refs[:10]
    (qa_ref, ka_ref, va_ref, qb_ref, kb_ref, vb_ref, qc_ref, kc_ref, vc_ref, g_ref,
     xn_ref) = refs[10 + n_prev:]
    j = pl.program_id(1)

    @pl.when(j == 0)
    def _():
        x = x_ref[...]
        ms = jnp.mean(x * x, axis=-1, keepdims=True)
        xn_ref[...] = (x * lax.rsqrt(ms + EPS) * ng_ref[...]).astype(BF16)

    z = _dot(xn_ref[...], w_ref[...])
    tm, tn = z.shape
    heads = tn // HEAD_DIM

    r_i = lax.broadcasted_iota(jnp.int32, (HEAD_DIM, HEAD_DIM), 0)
    c_i = lax.broadcasted_iota(jnp.int32, (HEAD_DIM, HEAD_DIM), 1)
    ones_map = ((r_i // D_HALF) == (c_i // D_HALF)).astype(BF16)
    ones_head = jnp.ones((HEAD_DIM, HEAD_DIM), BF16)
    lane = lax.broadcasted_iota(jnp.int32, (tm, HEAD_DIM), 1)
    first_half = (lane % D_HALF) < (D_HALF // 2)

    def map_norm_rope(zh, gain):
        ms = _dot_f32acc(zh * zh, ones_map) * (1.0 / D_HALF)
        y = zh * lax.rsqrt(ms + EPS) * gain
        partner = jnp.where(first_half,
                            pltpu.roll(y, HEAD_DIM - D_HALF // 2, 1),
                            pltpu.roll(y, D_HALF // 2, 1))
        return y * cos_ref[...] + partner * sin_ref[...]

    def head_norm(zh, gain):
        ms = _dot_f32acc(zh * zh, ones_head) * (1.0 / HEAD_DIM)
        return zh * lax.rsqrt(ms + EPS) * gain

    def per_head(fn, out_ref, scale=None):
        for hh in range(heads):
            sl = slice(hh * HEAD_DIM, (hh + 1) * HEAD_DIM)
            y = fn(z[:, sl])
            if scale is not None:
                y = y * scale
            out_ref[:, sl] = y.astype(out_ref.dtype)

    def in_range(name):
        lo, hi = bounds[name]
        return jnp.logical_and(j >= lo, j < hi)

    @pl.when(in_range("qa"))
    def _():
        per_head(lambda zh: map_norm_rope(zh, qna_ref[...]), qa_ref, scale=D_HALF ** -0.5)

    @pl.when(in_range("ka"))
    def _():
        per_head(lambda zh: map_norm_rope(zh, kna_ref[...]), ka_ref)

    @pl.when(in_range("va"))
    def _():
        va_ref[...] = z

    @pl.when(in_range("qb"))
    def _():
        qb_ref[...] = z.astype(BF16)

    @pl.when(in_range("kb"))
    def _():
        kb_ref[...] = z

    @pl.when(in_range("vb"))
    def _():
        vb_ref[...] = z

    @pl.when(in_range("qc"))
    def _():
        per_head(lambda zh: head_norm(zh, qnc_ref[...]), qc_ref)

    @pl.when(in_range("kc"))
    def _():
        per_head(lambda zh: head_norm(zh, knc_ref[...]), kc_ref)

    @pl.when(in_range("vc"))
    def _():
        vc_ref[...] = z

    @pl.when(in_range("g"))
    def _():
        g_ref[...] = (0.5 * jnp.tanh(0.5 * (z + bg_ref[...])) + 0.5).astype(BF16)


def _proj(x, norm_g, w16, b_gate, qn_a, kn_a, qn_c, kn_c, cos_tab, sin_tab, prev_kv, *, layer, depth,
          w_a, w_b, w_c, d_model):
    n = x.shape[0]
    tm, tn = PROJ_TM, PROJ_TN
    n_in = w16.shape[2]
    assert n % tm == 0 and n_in % tn == 0 and w_a % tn == 0 and w_b == tn and w_c == tn
    assert cos_tab.shape[0] % tm == 0
    tab_tiles = cos_tab.shape[0] // tm
    na = w_a // tn
    widths = [na, na, na, 1, 1, 1, 1, 1, 1, N_BRANCH * d_model // tn]
    bounds, start = {}, 0
    for name, wd in zip(_PROJ_NAMES, widths):
        bounds[name] = (start, start + wd)
        start += wd
    assert start == n_in // tn

    def col_block(name):
        lo, hi = bounds[name]
        return lambda j: jnp.clip(j - lo, 0, hi - lo - 1)

    tab_map = lambda i, j: (i % tab_tiles, 0)
    g_col = col_block("g")
    vec = lambda width: pl.BlockSpec((1, width), lambda i, j: (0, 0))
    in_specs = [
        pl.BlockSpec((tm, d_model), lambda i, j: (i, 0)),
        vec(d_model),
        pl.BlockSpec((None, d_model, tn), lambda i, j: (layer, 0, j)),
        pl.BlockSpec((1, tn), lambda i, j: (0, g_col(j))),
        vec(HEAD_DIM), vec(HEAD_DIM), vec(HEAD_DIM), vec(HEAD_DIM),
        pl.BlockSpec((tm, HEAD_DIM), tab_map),
        pl.BlockSpec((tm, HEAD_DIM), tab_map),
    ]
    out_widths = [w_a, w_a, w_a, w_b, w_b, w_b, w_c, w_c, w_c, N_BRANCH * d_model]
    out_specs, out_shape = [], []
    for idx, (name, wd) in enumerate(zip(_PROJ_NAMES, out_widths)):
        cb = col_block(name)
        if idx in _PROJ_KV:
            out_specs.append(pl.BlockSpec((None, tm, tn), lambda i, j, cb=cb: (layer, i, cb(j))))
            out_shape.append(jax.ShapeDtypeStruct((depth, n, wd), F32))
        else:
            out_specs.append(pl.BlockSpec((tm, tn), lambda i, j, cb=cb: (i, cb(j))))
            out_shape.append(jax.ShapeDtypeStruct((n, wd), BF16))
    prev = [] if prev_kv is None else list(prev_kv)
    aliases = {len(in_specs) + k: _PROJ_KV[k] for k in range(len(prev))}
    in_specs = in_specs + [pl.BlockSpec(memory_space=pl.ANY)] * len(prev)
    tile2 = lambda v: jnp.concatenate([v, v]).reshape(1, HEAD_DIM)
    outs = pl.pallas_call(
        functools.partial(_proj_kernel, bounds=bounds, n_prev=len(prev)),
        grid=(n // tm, n_in // tn),
        in_specs=in_specs,
        out_specs=out_specs,
        out_shape=out_shape,
        scratch_shapes=[pltpu.VMEM((tm, d_model), BF16)],
        input_output_aliases=aliases,
        compiler_params=_params(("arbitrary", "arbitrary")),
        name="proj",
    )(x, norm_g.reshape(1, -1), w16, b_gate.reshape(1, -1), tile2(qn_a), tile2(kn_a),
      qn_c.reshape(1, -1), kn_c.reshape(1, -1), cos_tab, sin_tab, *prev)
    return outs[0], outs[3], outs[6], outs[9], [outs[k] for k in _PROJ_KV]


def _head_rows(ref, h, heads, start=0, size=None):
    size = ref.shape[0] // heads if size is None else size
    return ref[pl.ds(start * heads + h, size, stride=heads), :]


def _stack_maps(q):
    lane = lax.broadcasted_iota(jnp.int32, q.shape, 1)
    zero = jnp.zeros_like(q)
    return jnp.concatenate([jnp.where(lane < D_HALF, q, zero), jnp.where(lane >= D_HALF, q, zero)], axis=0)


def _diff_lambda(lq1_ref, lk1_ref, lq2_ref, lk2_ref, lam_init):
    a = jnp.exp(jnp.sum(lq1_ref[...] * lk1_ref[...], axis=1, keepdims=True))
    b = jnp.exp(jnp.sum(lq2_ref[...] * lk2_ref[...], axis=1, keepdims=True))
    return a - b + lam_init


def _diff_finish(l, acc, t, lam, subln, lam_init):
    o1 = acc[:t] / l[:t]
    o2 = acc[t:] / l[t:]
    o = o1 - lam * o2
    ms = jnp.mean(o * o, axis=-1, keepdims=True)
    return o * lax.rsqrt(ms + EPS) * subln * (1.0 - lam_init)


def _attn_a_prompt_kernel(q_ref, k_ref, v_ref, lq1_ref, lk1_ref, lq2_ref, lk2_ref, sg_ref, o_ref,
                          s_ref, m_ref, l_ref, acc_ref, *, tq, lam_init):
    qi = pl.program_id(2)
    qq = _stack_maps(q_ref[...])

    def lane_fold(op, x, acc):
        for c in range(x.shape[1] // LANES):
            acc = op(acc, x[:, c * LANES:(c + 1) * LANES])
        return acc

    def score_block(kb, masked):
        start = pl.multiple_of(kb * tq, tq)
        s = _dot_nt(qq, k_ref[pl.ds(start, tq), :].astype(BF16))
        if masked:
            q_chunk = (lax.broadcasted_iota(jnp.int32, (2 * tq, 1), 0) % tq) // CHUNK
            k_chunk = lax.broadcasted_iota(jnp.int32, (1, tq), 1) // CHUNK
            s = jnp.where(k_chunk <= q_chunk, s, -jnp.inf)
        s_ref[kb] = s
        m_ref[...] = lane_fold(jnp.maximum, s, m_ref[...])

    def score_body(kb, carry):
        score_block(kb, False)
        return carry

    m_ref[...] = jnp.full(m_ref.shape, -jnp.inf, F32)
    lax.fori_loop(0, qi, score_body, 0)
    score_block(qi, True)
    m = jnp.max(m_ref[...], axis=1, keepdims=True)

    def value_body(kb, carry):
        start = pl.multiple_of(kb * tq, tq)
        p = jnp.exp(s_ref[kb] - m)
        l_ref[...] = lane_fold(jnp.add, p, l_ref[...])
        acc_ref[...] += _dot(p.astype(BF16), v_ref[pl.ds(start, tq), :].astype(BF16))
        return carry

    l_ref[...] = jnp.zeros(l_ref.shape, F32)
    acc_ref[...] = jnp.zeros(acc_ref.shape, F32)
    lax.fori_loop(0, qi + 1, value_body, 0)
    l = jnp.sum(l_ref[...], axis=1, keepdims=True)
    lam = _diff_lambda(lq1_ref, lk1_ref, lq2_ref, lk2_ref, lam_init)
    o_ref[...] = _diff_finish(l, acc_ref[...], tq, lam, sg_ref[...], lam_init).astype(o_ref.dtype)


def _lam_specs():
    spec = pl.BlockSpec((1, D_HALF), lambda *_: (0, 0))
    return [spec, spec, spec, spec, pl.BlockSpec((1, HEAD_DIM), lambda *_: (0, 0))]


def _lam_args(lq1, lk1, lq2, lk2, subln):
    return (lq1.reshape(1, -1), lk1.reshape(1, -1), lq2.reshape(1, -1), lk2.reshape(1, -1),
            subln.reshape(1, -1))


def _attn_a_prompt(q16, k32, v32, lam_args, *, layer, batch, seq, heads):
    n = q16.shape[0]
    tq = ATTN_A_TQ
    assert seq % tq == 0 and tq % CHUNK == 0
    nq = seq // tq
    kv_spec = pl.BlockSpec((None, seq, HEAD_DIM), lambda b, h, qi: (layer, b, h))
    return pl.pallas_call(
        functools.partial(_attn_a_prompt_kernel, tq=tq, lam_init=_lambda_init(layer)),
        grid=(batch, heads, nq),
        in_specs=[pl.BlockSpec((tq, HEAD_DIM), lambda b, h, qi: (b * nq + qi, h)), kv_spec, kv_spec]
                 + _lam_specs(),
        out_specs=pl.BlockSpec((tq, HEAD_DIM), lambda b, h, qi: (b * nq + qi, h)),
        out_shape=jax.ShapeDtypeStruct((n, heads * HEAD_DIM), BF16),
        scratch_shapes=[pltpu.VMEM((nq, 2 * tq, tq), F32),
                        pltpu.VMEM((2 * tq, LANES), F32), pltpu.VMEM((2 * tq, LANES), F32),
                        pltpu.VMEM((2 * tq, HEAD_DIM), F32)],
        compiler_params=_params(("arbitrary", "arbitrary", "arbitrary")),
        name="attn_a_prompt",
    )(q16, k32, v32, *lam_args)


def _attn_a_sample_kernel(q_ref, ck_ref, cv_ref, kn_ref, vn_ref, lq1_ref, lk1_ref, lq2_ref, lk2_ref, sg_ref,
                          o_ref, qq_ref, s_ref, sn_ref, m_ref, mf_ref, l_ref, acc_ref, *,
                          heads, new_len, lam_init):
    phase = pl.program_id(1)
    kt = pl.program_id(2)
    nkt = pl.num_programs(2)

    def lane_fold(op, x, acc):
        for c in range(x.shape[1] // LANES):
            acc = op(acc, x[:, c * LANES:(c + 1) * LANES])
        return acc

    @pl.when(jnp.logical_and(phase == 0, kt == 0))
    def _():
        for h in range(heads):
            sl = slice(h * HEAD_DIM, (h + 1) * HEAD_DIM)
            qq = _stack_maps(q_ref[:, sl])
            qq_ref[h] = qq
            s_new = _dot_nt(qq, kn_ref[:, sl].astype(BF16))
            sn_ref[h] = s_new
            m_ref[h] = jnp.broadcast_to(jnp.max(s_new, axis=1, keepdims=True), m_ref.shape[1:])

    @pl.when(phase == 0)
    def _():
        for h in range(heads):
            s = _dot_nt(qq_ref[h], _head_rows(ck_ref, h, heads).astype(BF16))
            s_ref[h, kt] = s
            m_ref[h] = lane_fold(jnp.maximum, s, m_ref[h])

    @pl.when(jnp.logical_and(phase == 1, kt == 0))
    def _():
        lane = lax.broadcasted_iota(jnp.int32, l_ref.shape[1:], 1)
        for h in range(heads):
            sl = slice(h * HEAD_DIM, (h + 1) * HEAD_DIM)
            m = jnp.max(m_ref[h], axis=1, keepdims=True)
            mf_ref[h] = m
            p_new = jnp.exp(sn_ref[h] - m)
            l_ref[h] = jnp.where(lane == 0, jnp.sum(p_new, axis=1, keepdims=True), 0.0)
            acc_ref[h] = _dot(p_new.astype(BF16), vn_ref[:, sl].astype(BF16))

    @pl.when(phase == 1)
    def _():
        for h in range(heads):
            p = jnp.exp(s_ref[h, kt] - mf_ref[h])
            l_ref[h] = lane_fold(jnp.add, p, l_ref[h])
            acc_ref[h] += _dot(p.astype(BF16), _head_rows(cv_ref, h, heads).astype(BF16))

    @pl.when(jnp.logical_and(phase == 1, kt == nkt - 1))
    def _():
        lam = _diff_lambda(lq1_ref, lk1_ref, lq2_ref, lk2_ref, lam_init)
        for h in range(heads):
            l = jnp.sum(l_ref[h], axis=1, keepdims=True)
            y = _diff_finish(l, acc_ref[h], new_len, lam, sg_ref[...], lam_init)
            o_ref[:, h * HEAD_DIM:(h + 1) * HEAD_DIM] = y.astype(o_ref.dtype)


def _attn_a_sample(q16, cache_k, cache_v, k32, v32, lam_args, *, layer, streams, new_len, heads):
    past = cache_k.shape[2] // heads
    tk = SAMPLE_A_TK
    assert past % tk == 0
    nkt = past // tk
    width = heads * HEAD_DIM
    q_spec = pl.BlockSpec((new_len, width), lambda b, p, kt: (b, 0))
    new_spec = pl.BlockSpec((None, new_len, width), lambda b, p, kt: (layer, b, 0))
    k_spec = pl.BlockSpec((None, None, tk * heads, HEAD_DIM),
                          lambda b, p, kt: (layer, b, jnp.where(p == 0, kt, nkt - 1), 0))
    v_spec = pl.BlockSpec((None, None, tk * heads, HEAD_DIM),
                          lambda b, p, kt: (layer, b, jnp.where(p == 0, 0, kt), 0))
    rows = 2 * new_len
    return pl.pallas_call(
        functools.partial(_attn_a_sample_kernel, heads=heads, new_len=new_len,
                          lam_init=_lambda_init(layer)),
        grid=(streams, 2, nkt),
        in_specs=[q_spec, k_spec, v_spec, new_spec, new_spec] + _lam_specs(),
        out_specs=q_spec,
        out_shape=jax.ShapeDtypeStruct(q16.shape, BF16),
        scratch_shapes=[pltpu.VMEM((heads, rows, HEAD_DIM), BF16),
                        pltpu.VMEM((heads, nkt, rows, tk), F32),
                        pltpu.VMEM((heads, rows, new_len), F32),
                        pltpu.VMEM((heads, rows, LANES), F32),
                        pltpu.VMEM((heads, rows, 1), F32),
                        pltpu.VMEM((heads, rows, LANES), F32),
                        pltpu.VMEM((heads, rows, HEAD_DIM), F32)],
        compiler_params=_params(("arbitrary", "arbitrary", "arbitrary")),
        name="attn_a_sample",
    )(q16, cache_k, cache_v, k32, v32, *lam_args)


def _upper_ones(t):
    r = lax.broadcasted_iota(jnp.int32, (t, t), 0)
    c = lax.broadcasted_iota(jnp.int32, (t, t), 1)
    return (r > c).astype(BF16)


def _sb_block(q, kblk, vblk, ones_u, c, acc, scale, strict):
    z = _dot_nt(q, kblk) * scale
    sp = _softplus(z)
    log_stay = -sp
    if strict is not None:
        log_stay = jnp.where(strict, log_stay, 0.0)
    between = _dot_f32acc(log_stay, ones_u)
    w = jnp.exp((z - sp) + between + c)
    if strict is not None:
        w = jnp.where(strict, w, 0.0)
    acc = acc + _dot(w.astype(BF16), vblk)
    c = c + jnp.sum(log_stay, axis=1, keepdims=True)
    return c, acc


def _attn_b_prompt_kernel(q_ref, k_ref, v_ref, o_ref, c_ref, acc_ref, *, tq, tk):
    qi = pl.program_id(2)
    q = q_ref[...]
    scale = HEAD_DIM ** -0.5
    ones_u = _upper_ones(tk)
    per_q = tq // tk
    c_ref[...] = jnp.zeros(c_ref.shape, F32)
    acc_ref[...] = jnp.zeros(acc_ref.shape, F32)

    def block(kb, masked):
        start = pl.multiple_of(kb * tk, tk)
        strict = None
        if masked:
            row = lax.broadcasted_iota(jnp.int32, (tq, 1), 0)
            col = lax.broadcasted_iota(jnp.int32, (1, tk), 1)
            strict = start + col < qi * tq + row
        c, acc = _sb_block(q, k_ref[pl.ds(start, tk), :].astype(BF16), v_ref[pl.ds(start, tk), :].astype(BF16),
                           ones_u, c_ref[...], acc_ref[...], scale, strict)
        c_ref[...] = c
        acc_ref[...] = acc

    for d in reversed(range(per_q)):
        block(qi * per_q + d, True)

    def body(t, carry):
        block(qi * per_q - 1 - t, False)
        return carry

    lax.fori_loop(0, qi * per_q, body, 0)
    o_ref[...] = acc_ref[...].astype(o_ref.dtype)


def _attn_b_prompt(q16, k32, v32, *, layer, batch, seq, heads):
    n = q16.shape[0]
    tq, tk = ATTN_B_TQ, SB_SUB
    assert seq % tq == 0 and tq % tk == 0
    nq = seq // tq
    kv_spec = pl.BlockSpec((None, seq, HEAD_DIM), lambda b, h, qi: (layer, b, h))
    return pl.pallas_call(
        functools.partial(_attn_b_prompt_kernel, tq=tq, tk=tk),
        grid=(batch, heads, nq),
        in_specs=[pl.BlockSpec((tq, HEAD_DIM), lambda b, h, qi: (b * nq + qi, h)), kv_spec, kv_spec],
        out_specs=pl.BlockSpec((tq, HEAD_DIM), lambda b, h, qi: (b * nq + qi, h)),
        out_shape=jax.ShapeDtypeStruct((n, heads * HEAD_DIM), BF16),
        scratch_shapes=[pltpu.VMEM((tq, 1), F32), pltpu.VMEM((tq, HEAD_DIM), F32)],
        compiler_params=_params(("arbitrary", "arbitrary", "arbitrary")),
        name="attn_b_prompt",
    )(q16, k32, v32)


def _attn_b_sample_kernel(q_ref, ck_ref, cv_ref, kn_ref, vn_ref, o_ref, u_ref, c_ref, acc_ref, *,
                          heads, new_len):
    b = pl.program_id(0)
    kt = pl.program_id(1)
    nkt = pl.num_programs(1)
    scale = HEAD_DIM ** -0.5
    tk = ck_ref.shape[0] // heads
    head_cols = lambda h: slice(h * HEAD_DIM, (h + 1) * HEAD_DIM)
    head_rows = lambda h: slice(h * new_len, (h + 1) * new_len)

    @pl.when(jnp.logical_and(b == 0, kt == 0))
    def _():
        u_ref[...] = _upper_ones(tk)

    @pl.when(kt == 0)
    def _():
        ones_new = _upper_ones(new_len)
        row = lax.broadcasted_iota(jnp.int32, (new_len, new_len), 0)
        col = lax.broadcasted_iota(jnp.int32, (new_len, new_len), 1)
        for h in range(heads):
            sl = head_cols(h)
            c, acc = _sb_block(q_ref[:, sl], kn_ref[:, sl].astype(BF16), vn_ref[:, sl].astype(BF16), ones_new,
                               jnp.zeros((new_len, 1), F32), jnp.zeros((new_len, HEAD_DIM), F32),
                               scale, col < row)
            c_ref[head_rows(h), :] = c
            acc_ref[h] = acc

    z = jnp.concatenate([_dot_nt(q_ref[:, head_cols(h)], _head_rows(ck_ref, h, heads).astype(BF16))
                         for h in range(heads)], axis=0) * scale
    sp = _softplus(z)
    log_stay = -sp
    between = _dot_f32acc(log_stay, u_ref[...])
    w = jnp.exp((z - sp) + between + c_ref[...]).astype(BF16)
    for h in range(heads):
        acc_ref[h] += _dot(w[head_rows(h), :], _head_rows(cv_ref, h, heads).astype(BF16))
    c_ref[...] += jnp.sum(log_stay, axis=1, keepdims=True)

    @pl.when(kt == nkt - 1)
    def _():
        for h in range(heads):
            o_ref[:, head_cols(h)] = acc_ref[h].astype(o_ref.dtype)


def _attn_b_sample(q16, cache_k, cache_v, k32, v32, *, layer, streams, new_len, heads):
    past = cache_k.shape[2] // heads
    tk = SAMPLE_TK
    nkt = past // tk
    assert past % tk == 0
    width = heads * HEAD_DIM
    q_spec = pl.BlockSpec((new_len, width), lambda b, kt: (b, 0))
    new_spec = pl.BlockSpec((None, new_len, width), lambda b, kt: (layer, b, 0))
    cache_spec = pl.BlockSpec((None, None, tk * heads, HEAD_DIM), lambda b, kt: (layer, b, nkt - 1 - kt, 0))
    return pl.pallas_call(
        functools.partial(_attn_b_sample_kernel, heads=heads, new_len=new_len),
        grid=(streams, nkt),
        in_specs=[q_spec, cache_spec, cache_spec, new_spec, new_spec],
        out_specs=q_spec,
        out_shape=jax.ShapeDtypeStruct(q16.shape, BF16),
        scratch_shapes=[pltpu.VMEM((tk, tk), BF16),
                        pltpu.VMEM((heads * new_len, 1), F32),
                        pltpu.VMEM((heads, new_len, HEAD_DIM), F32)],
        compiler_params=_params(("arbitrary", "arbitrary")),
        name="attn_b_sample",
    )(q16, cache_k, cache_v, k32, v32)


def _attn_c_prompt_kernel(q_ref, k_ref, v_ref, bgrp_ref, o_ref, k16_ref, v16_ref, *, seq):
    k16_ref[0:C_WIN, :] = jnp.zeros((C_WIN, HEAD_DIM), BF16)
    v16_ref[0:C_WIN, :] = jnp.zeros((C_WIN, HEAD_DIM), BF16)
    k16_ref[C_WIN:, :] = k_ref[...].astype(BF16)
    v16_ref[C_WIN:, :] = v_ref[...].astype(BF16)
    scale = HEAD_DIM ** -0.5
    key_chunk = lax.broadcasted_iota(jnp.int32, (1, BAND_KEYS), 1) // CHUNK

    def group(gi):
        r0 = pl.multiple_of(gi * BAND_ROWS, BAND_ROWS)
        s = _dot_nt(q_ref[pl.ds(r0, BAND_ROWS), :], k16_ref[pl.ds(r0, BAND_KEYS), :]) * scale + bgrp_ref[0]
        s = jnp.where(key_chunk + gi * BAND_GROUP >= C_BAND_CHUNKS, s, -jnp.inf)
        m = jnp.max(s, axis=1, keepdims=True)
        p = jnp.exp(s - m)
        l = jnp.sum(p, axis=1, keepdims=True)
        o = _dot(p.astype(BF16), v16_ref[pl.ds(r0, BAND_KEYS), :]) / l
        o_ref[pl.ds(r0, BAND_ROWS), :] = o.astype(o_ref.dtype)

    def body(t, carry):
        group(2 * t)
        group(2 * t + 1)
        return carry

    lax.fori_loop(0, seq // BAND_ROWS // 2, body, 0)


def _attn_c_prompt(q16, k32, v32, bias_grp, *, layer, batch, seq, heads):
    n = q16.shape[0]
    assert seq % (2 * BAND_ROWS) == 0
    kv_spec = pl.BlockSpec((None, seq, HEAD_DIM), lambda b, h: (layer, b, h))
    return pl.pallas_call(
        functools.partial(_attn_c_prompt_kernel, seq=seq),
        grid=(batch, heads),
        in_specs=[pl.BlockSpec((seq, HEAD_DIM), lambda b, h: (b, h)), kv_spec, kv_spec,
                  pl.BlockSpec((1, BAND_ROWS, BAND_KEYS), lambda b, h: (layer * heads + h, 0, 0))],
        out_specs=pl.BlockSpec((seq, HEAD_DIM), lambda b, h: (b, h)),
        out_shape=jax.ShapeDtypeStruct((n, heads * HEAD_DIM), BF16),
        scratch_shapes=[pltpu.VMEM((seq + C_WIN, HEAD_DIM), BF16),
                        pltpu.VMEM((seq + C_WIN, HEAD_DIM), BF16)],
        compiler_params=_params(("arbitrary", "arbitrary")),
        name="attn_c_prompt",
    )(q16, k32, v32, bias_grp)


def _attn_c_sample_kernel(q_ref, ck_ref, cv_ref, kn_ref, vn_ref, bfar_ref, bown_ref, o_ref, *, heads):
    scale = HEAD_DIM ** -0.5
    for h in range(heads):
        sl = slice(h * HEAD_DIM, (h + 1) * HEAD_DIM)
        q = q_ref[:, sl]
        s_far = _dot_nt(q, _head_rows(ck_ref, h, heads).astype(BF16)) * scale + bfar_ref[h]
        s_own = _dot_nt(q, kn_ref[:, sl].astype(BF16)) * scale + bown_ref[h]
        m = jnp.maximum(jnp.max(s_far, axis=1, keepdims=True), jnp.max(s_own, axis=1, keepdims=True))
        p_far = jnp.exp(s_far - m)
        p_own = jnp.exp(s_own - m)
        l = jnp.sum(p_far, axis=1, keepdims=True) + jnp.sum(p_own, axis=1, keepdims=True)
        o = (_dot(p_far.astype(BF16), _head_rows(cv_ref, h, heads).astype(BF16))
             + _dot(p_own.astype(BF16), vn_ref[:, sl].astype(BF16))) / l
        o_ref[:, sl] = o.astype(o_ref.dtype)


def _attn_c_sample(q16, cache_k, cache_v, k32, v32, bias_far, bias_own, *, layer, streams, new_len, heads):
    assert cache_k.shape[2] == C_WIN * heads and new_len == CHUNK
    width = heads * HEAD_DIM
    q_spec = pl.BlockSpec((new_len, width), lambda b: (b, 0))
    new_spec = pl.BlockSpec((None, new_len, width), lambda b: (layer, b, 0))
    cache_spec = pl.BlockSpec((None, None, C_WIN * heads, HEAD_DIM), lambda b: (layer, b, 0, 0))
    return pl.pallas_call(
        functools.partial(_attn_c_sample_kernel, heads=heads),
        grid=(streams,),
        in_specs=[q_spec, cache_spec, cache_spec, new_spec, new_spec,
                  pl.BlockSpec((heads, CHUNK, C_WIN), lambda b: (layer, 0, 0)),
                  pl.BlockSpec((heads, CHUNK, CHUNK), lambda b: (layer, 0, 0))],
        out_specs=q_spec,
        out_shape=jax.ShapeDtypeStruct(q16.shape, BF16),
        compiler_params=_params(("arbitrary",)),
        name="attn_c_sample",
    )(q16, cache_k, cache_v, k32, v32, bias_far, bias_own)


def _merge_kernel(x_ref, oa_ref, ob_ref, oc_ref, g_ref, wb_ref, wo_ref, o_ref, *, w_a, w_b, d_model):
    ya = _dot(oa_ref[...], wb_ref[0:w_a, :])
    yb = _dot(ob_ref[...], wb_ref[w_a:w_a + w_b, :])
    yc = _dot(oc_ref[...], wb_ref[w_a + w_b:, :])
    m = (g_ref[:, 0:d_model].astype(F32) * ya + g_ref[:, d_model:2 * d_model].astype(F32) * yb
         + g_ref[:, 2 * d_model:].astype(F32) * yc)
    o_ref[...] = x_ref[...] + _dot(m.astype(BF16), wo_ref[...])


def _merge(x, oa, ob, oc, g, wb16, wo16, *, layer):
    n, d_model = x.shape
    tm = MERGE_TM
    assert n % tm == 0
    w_a, w_b, w_c = oa.shape[1], ob.shape[1], oc.shape[1]
    rows = lambda width: pl.BlockSpec((tm, width), lambda i: (i, 0))
    resident = lambda shape: pl.BlockSpec((None,) + shape[1:], lambda i: (layer, 0, 0),
                                          pipeline_mode=pl.Buffered(1))
    return pl.pallas_call(
        functools.partial(_merge_kernel, w_a=w_a, w_b=w_b, d_model=d_model),
        grid=(n // tm,),
        in_specs=[rows(d_model), rows(w_a), rows(w_b), rows(w_c), rows(N_BRANCH * d_model),
                  resident(wb16.shape), resident(wo16.shape)],
        out_specs=rows(d_model),
        out_shape=jax.ShapeDtypeStruct((n, d_model), F32),
        compiler_params=_params(("arbitrary",)),
        name="merge",
    )(x, oa, ob, oc, g, wb16, wo16)


_R_E1, _R_E2, _R_C1, _R_C2, _R_RANK1, _R_RANK2 = range(6)


def _route_kernel(xp_ref, xs_ref, ng_ref, wr_ref, br_ref, xn_ref, route_ref, cnt_ref, carry_ref, *,
                  prompt_tiles):
    i = pl.program_id(0)
    tm = xp_ref.shape[0]
    lane = lax.broadcasted_iota(jnp.int32, (tm, LANES), 1)
    lane_f = lane.astype(F32)

    @pl.when(i == 0)
    def _():
        carry_ref[...] = jnp.zeros(carry_ref.shape, F32)

    x = jnp.where(i < prompt_tiles, xp_ref[...], xs_ref[...])
    ms = jnp.mean(x * x, axis=-1, keepdims=True)
    xn = x * lax.rsqrt(ms + EPS) * ng_ref[...]
    xn_ref[...] = xn

    x_hi, x_lo = _split_bf16(xn)
    w_hi, w_lo = _split_bf16(wr_ref[...])
    logits = _dot(x_hi, w_hi) + _dot(x_lo, w_hi) + _dot(x_hi, w_lo) + br_ref[...]
    is_group = lane < N_GROUPS
    gl = jnp.where(is_group, logits, -jnp.inf)
    g_max = jnp.max(gl, axis=1, keepdims=True)
    g_idx = jnp.min(jnp.where(gl == g_max, lane_f, float(LANES)), axis=1, keepdims=True)
    p_group = 1.0 / jnp.sum(jnp.where(is_group, jnp.exp(logits - g_max), 0.0), axis=1, keepdims=True)
    in_group = jnp.logical_and(lane >= N_GROUPS, lane < N_GROUPS + N_EXP)
    in_group = jnp.logical_and(in_group, ((lane - N_GROUPS) // E_PER_GROUP).astype(F32) == g_idx)
    el = jnp.where(in_group, logits, -jnp.inf)
    v1 = jnp.max(el, axis=1, keepdims=True)
    i1 = jnp.min(jnp.where(el == v1, lane_f, float(LANES)), axis=1, keepdims=True)
    el2 = jnp.where(lane_f == i1, -jnp.inf, el)
    v2 = jnp.max(el2, axis=1, keepdims=True)
    i2 = jnp.min(jnp.where(el2 == v2, lane_f, float(LANES)), axis=1, keepdims=True)
    t = jnp.exp(v2 - v1)
    c1 = p_group / (1.0 + t)
    c2 = p_group * t / (1.0 + t)
    e1 = i1 - N_GROUPS
    e2 = i2 - N_GROUPS

    a1 = (lane_f == e1).astype(F32)
    a2 = (lane_f == e2).astype(F32)
    a = a1 + a2
    row = lax.broadcasted_iota(jnp.int32, (tm, tm), 0)
    col = lax.broadcasted_iota(jnp.int32, (tm, tm), 1)
    earlier = (col < row).astype(BF16)
    before = _dot(earlier, a.astype(BF16)) + carry_ref[...]
    r1 = jnp.sum(a1 * before, axis=1, keepdims=True)
    r2 = jnp.sum(a2 * before, axis=1, keepdims=True)
    carry_ref[...] += jnp.sum(a, axis=0, keepdims=True)
    cnt_ref[...] = carry_ref[...]

    record = jnp.zeros((tm, LANES), F32)
    for slot, val in ((_R_E1, e1), (_R_E2, e2), (_R_C1, c1), (_R_C2, c2), (_R_RANK1, r1), (_R_RANK2, r2)):
        record = jnp.where(lane == slot, val, record)
    route_ref[...] = record


def _route(xp, xs, norm_g, w_router, b_router):
    tm = MOE_ROUTE_TM
    (n_p, d_model), n_s = xp.shape, xs.shape[0]
    assert n_p % tm == 0 and n_s % tm == 0
    pt, st = n_p // tm, n_s // tm
    n = n_p + n_s
    const = lambda shape: pl.BlockSpec(shape, lambda i: (0, 0))
    return pl.pallas_call(
        functools.partial(_route_kernel, prompt_tiles=pt),
        grid=(pt + st,),
        in_specs=[pl.BlockSpec((tm, d_model), lambda i: (jnp.minimum(i, pt - 1), 0)),
                  pl.BlockSpec((tm, d_model), lambda i: (jnp.maximum(i - pt, 0), 0)),
                  const((1, d_model)), const((d_model, LANES)), const((1, LANES))],
        out_specs=[pl.BlockSpec((tm, d_model), lambda i: (i, 0)),
                   pl.BlockSpec((tm, LANES), lambda i: (i, 0)),
                   const((1, LANES))],
        out_shape=[jax.ShapeDtypeStruct((n, d_model), F32), jax.ShapeDtypeStruct((n, LANES), F32),
                   jax.ShapeDtypeStruct((1, LANES), F32)],
        scratch_shapes=[pltpu.VMEM((1, LANES), F32)],
        compiler_params=_params(("arbitrary",)),
        name="moe_route",
    )(xp, xs, norm_g.reshape(1, -1), w_router, b_router)


def _row_copy(src, src_row, dst, dst_row, sem):
    return pltpu.make_async_copy(src.at[pl.ds(src_row, 1), :], dst.at[pl.ds(dst_row, 1), :], sem)


def _row_token_kernel(pos1_ref, pos2_ref, out_ref, *, n_tokens):
    def init(r, carry):
        out_ref[r] = 0
        return carry

    def place(tok, carry):
        out_ref[pos1_ref[tok]] = tok
        out_ref[pos2_ref[tok]] = tok
        return carry

    lax.fori_loop(0, out_ref.shape[0], init, 0, unroll=8)
    lax.fori_loop(0, n_tokens, place, 0, unroll=8)


def _row_tokens(pos1, pos2, rows):
    return pl.pallas_call(
        functools.partial(_row_token_kernel, n_tokens=pos1.shape[0]),
        grid_spec=pltpu.PrefetchScalarGridSpec(
            num_scalar_prefetch=2, grid=(1,), in_specs=[],
            out_specs=pl.BlockSpec(memory_space=pltpu.SMEM)),
        out_shape=jax.ShapeDtypeStruct((rows,), jnp.int32),
        compiler_params=_params(("arbitrary",)),
        name="moe_row_tokens",
    )(pos1, pos2)


def _expert_kernel(tile_expert_ref, n_tiles_ref, row_token_ref, xn_hbm, wg_ref, wu_ref, wd_ref, y_ref,
                   xbuf, sem, *, tile):
    del tile_expert_ref
    t = pl.program_id(0)
    n_valid = n_tiles_ref[0]
    slot = t % 2

    def gather(tile_idx, into):
        base = tile_idx * tile

        def body(g, carry):
            r0 = pl.multiple_of(g * ROW_GROUP, ROW_GROUP)
            for k in range(ROW_GROUP):
                _row_copy(xn_hbm, row_token_ref[base + r0 + k], xbuf.at[into], r0 + k,
                          sem.at[into]).start(priority=k % 2)
            return carry

        lax.fori_loop(0, tile // ROW_GROUP, body, 0)

    def wait_rows(into):
        def body(r, carry):
            _row_copy(xn_hbm, 0, xbuf.at[into], 0, sem.at[into]).wait()
            return carry

        lax.fori_loop(0, tile, body, 0, unroll=8)

    @pl.when(jnp.logical_and(t == 0, n_valid > 0))
    def _():
        gather(0, 0)

    @pl.when(t + 1 < n_valid)
    def _():
        gather(t + 1, 1 - slot)

    @pl.when(t < n_valid)
    def _():
        wait_rows(slot)
        x = xbuf[slot].astype(BF16)
        h = jax.nn.silu(_dot(x, wg_ref[...].astype(BF16))) * _dot(x, wu_ref[...].astype(BF16))
        y_ref[...] = _dot(h.astype(BF16), wd_ref[...].astype(BF16))

    @pl.when(t >= n_valid)
    def _():
        y_ref[...] = jnp.zeros(y_ref.shape, y_ref.dtype)


def _experts(tile_expert, n_tiles, row_token, xn, w_gate, w_up, w_down, *, layer):
    tile = MOE_ROW_TILE
    rows, d_model = row_token.shape[0], xn.shape[1]
    d_exp = w_gate.shape[-1]
    w_in = pl.BlockSpec((None, None, d_model, d_exp), lambda t, te, nt, rt: (layer, te[t], 0, 0))
    w_out = pl.BlockSpec((None, None, d_exp, d_model), lambda t, te, nt, rt: (layer, te[t], 0, 0))
    return pl.pallas_call(
        functools.partial(_expert_kernel, tile=tile),
        grid_spec=pltpu.PrefetchScalarGridSpec(
            num_scalar_prefetch=3, grid=(rows // tile,),
            in_specs=[pl.BlockSpec(memory_space=pl.ANY), w_in, w_in, w_out],
            out_specs=pl.BlockSpec((tile, d_model), lambda t, te, nt, rt: (t, 0)),
            scratch_shapes=[pltpu.VMEM((2, tile, d_model), F32), pltpu.SemaphoreType.DMA((2,))]),
        out_shape=jax.ShapeDtypeStruct((rows, d_model), F32),
        compiler_params=_params(("arbitrary",)),
        name="moe_experts",
    )(tile_expert, n_tiles, row_token, xn, w_gate, w_up, w_down)


def _combine_kernel(pos1_ref, pos2_ref, xp_ref, xs_ref, route_ref, y_hbm, op_ref, os_ref, ybuf, sem, *,
                    tile, prompt_tiles):
    i = pl.program_id(0)
    base = i * tile

    def start(g, carry):
        r0 = pl.multiple_of(g * ROW_GROUP, ROW_GROUP)
        for k in range(ROW_GROUP):
            tok = base + r0 + k
            _row_copy(y_hbm, pos1_ref[tok], ybuf.at[0], r0 + k, sem).start(priority=0)
            _row_copy(y_hbm, pos2_ref[tok], ybuf.at[1], r0 + k, sem).start(priority=1)
        return carry

    def wait(r, carry):
        _row_copy(y_hbm, 0, ybuf.at[0], 0, sem).wait()
        _row_copy(y_hbm, 0, ybuf.at[1], 0, sem).wait()
        return carry

    lax.fori_loop(0, tile // ROW_GROUP, start, 0)
    lax.fori_loop(0, tile, wait, 0, unroll=8)
    y = route_ref[:, _R_C1:_R_C1 + 1] * ybuf[0] + route_ref[:, _R_C2:_R_C2 + 1] * ybuf[1]

    @pl.when(i < prompt_tiles)
    def _():
        op_ref[...] = xp_ref[...] + y

    @pl.when(i >= prompt_tiles)
    def _():
        os_ref[...] = xs_ref[...] + y


def _combine(pos1, pos2, xp, xs, route, y_sorted):
    tile = MOE_TOK_TILE
    (n_p, d_model), n_s = xp.shape, xs.shape[0]
    assert n_p % tile == 0 and n_s % tile == 0
    pt, st = n_p // tile, n_s // tile
    p_spec = pl.BlockSpec((tile, d_model), lambda i, p1, p2: (jnp.minimum(i, pt - 1), 0))
    s_spec = pl.BlockSpec((tile, d_model), lambda i, p1, p2: (jnp.maximum(i - pt, 0), 0))
    return pl.pallas_call(
        functools.partial(_combine_kernel, tile=tile, prompt_tiles=pt),
        grid_spec=pltpu.PrefetchScalarGridSpec(
            num_scalar_prefetch=2, grid=(pt + st,),
            in_specs=[p_spec, s_spec, pl.BlockSpec((tile, LANES), lambda i, p1, p2: (i, 0)),
                      pl.BlockSpec(memory_space=pl.ANY)],
            out_specs=[p_spec, s_spec],
            scratch_shapes=[pltpu.VMEM((2, tile, d_model), F32), pltpu.SemaphoreType.DMA(())]),
        out_shape=[jax.ShapeDtypeStruct(xp.shape, F32), jax.ShapeDtypeStruct(xs.shape, F32)],
        compiler_params=_params(("arbitrary",)),
        name="moe_combine",
    )(pos1, pos2, xp, xs, route, y_sorted)


def _moe(xp, xs, norm_g, w_router, b_router, w_gate, w_up, w_down, *, layer):
    n = xp.shape[0] + xs.shape[0]
    tile = MOE_ROW_TILE
    n_exp = w_gate.shape[1]
    row_tiles = (2 * n) // tile + n_exp
    xn, route, counts = _route(xp, xs, norm_g, w_router, b_router)

    counts = counts[0, :n_exp].astype(jnp.int32)
    padded = ((counts + tile - 1) // tile) * tile
    upto = jnp.arange(n_exp)[None, :] <= jnp.arange(n_exp)[:, None]
    ends = jnp.sum(jnp.where(upto, padded[None, :], 0), axis=1)
    offsets = ends - padded
    as_int = lambda lane_idx: route[:, lane_idx].astype(jnp.int32)
    pos1 = offsets[as_int(_R_E1)] + as_int(_R_RANK1)
    pos2 = offsets[as_int(_R_E2)] + as_int(_R_RANK2)
    tile_start = jnp.arange(row_tiles, dtype=jnp.int32) * tile
    tile_expert = jnp.minimum(jnp.sum((tile_start[:, None] >= ends[None, :]).astype(jnp.int32), axis=1),
                              n_exp - 1)
    n_tiles = ends[-1:] // tile

    row_token = _row_tokens(pos1, pos2, row_tiles * tile)
    y_sorted = _experts(tile_expert, n_tiles, row_token, xn, w_gate, w_up, w_down, layer=layer)
    return _combine(pos1, pos2, xp, xs, route, y_sorted)


def _rope_tables(pos):
    half = D_HALF // 2
    inv = ROPE_THETA ** (-2.0 * jnp.arange(half, dtype=F32) / D_HALF)
    ang = pos.astype(F32)[:, None] * inv[None, :]
    cos, sin = jnp.cos(ang), jnp.sin(ang)
    cos_full = jnp.tile(cos, (1, HEAD_DIM // half))
    sin_signed = jnp.tile(jnp.concatenate([-sin, sin], axis=1), (1, HEAD_DIM // D_HALF))
    return cos_full, sin_signed


def kernel(x_prompt, x_sample, cache_a_k, cache_a_v, cache_b_k, cache_b_v, cache_c_k, cache_c_v, norm_mix, w_in, b_gate, q_norm_a, k_norm_a, lam_q1, lam_k1, lam_q2, lam_k2, subln_a, q_norm_c, k_norm_c, rel_bias_c, w_branch, w_out, norm_ffn, w_group, b_group, w_expert_router, b_expert_router, w_gate_e, w_up_e, w_down_e):
    batch, seq, d_model = x_prompt.shape
    streams, new_len, _ = x_sample.shape
    depth, _, past, h_a, _ = cache_a_k.shape
    h_b, h_c = cache_b_k.shape[3], cache_c_k.shape[3]
    w_a, w_b, w_c = h_a * HEAD_DIM, h_b * HEAD_DIM, h_c * HEAD_DIM
    n_prompt, n_sample = batch * seq, streams * new_len
    tm = PROJ_TM
    assert seq % tm == 0 and n_sample % tm == 0 and tm % new_len == 0
    c_rows = min(C_WIN, seq)

    xp = x_prompt.reshape(n_prompt, d_model)
    xs = x_sample.reshape(n_sample, d_model)
    tab_p = _rope_tables(jnp.arange(seq))
    tab_s = _rope_tables(jnp.tile(past + jnp.arange(new_len), tm // new_len))
    bias_far, bias_own, bias_grp = _expand_bias(rel_bias_c.reshape(depth * h_c, N_REL))
    rows_view = lambda c: c.reshape(c.shape[0], c.shape[1], c.shape[2] * c.shape[3], c.shape[4])
    ca_k, ca_v, cb_k, cb_v, cc_k, cc_v = map(rows_view, (cache_a_k, cache_a_v, cache_b_k, cache_b_v,
                                                          cache_c_k, cache_c_v))

    w_in16, wb16, wo16 = w_in.astype(BF16), w_branch.astype(BF16), w_out.astype(BF16)
    kv_p = kv_s = None
    for l in range(depth):
        proj_w = (norm_mix[l], w_in16, b_gate[l], q_norm_a[l], k_norm_a[l], q_norm_c[l],
                  k_norm_c[l])
        proj_kw = dict(layer=l, depth=depth, w_a=w_a, w_b=w_b, w_c=w_c, d_model=d_model)
        qa_p, qb_p, qc_p, g_p, kv_p = _proj(xp, *proj_w, *tab_p, kv_p, **proj_kw)
        qa_s, qb_s, qc_s, g_s, kv_s = _proj(xs, *proj_w, *tab_s, kv_s, **proj_kw)
        lam_args = _lam_args(lam_q1[l], lam_k1[l], lam_q2[l], lam_k2[l], subln_a[l])

        oa_p = _attn_a_prompt(qa_p, kv_p[0], kv_p[1], lam_args, layer=l, batch=batch, seq=seq, heads=h_a)
        ob_p = _attn_b_prompt(qb_p, kv_p[2], kv_p[3], layer=l, batch=batch, seq=seq, heads=h_b)
        oc_p = _attn_c_prompt(qc_p, kv_p[4], kv_p[5], bias_grp, layer=l, batch=batch, seq=seq, heads=h_c)
        sample_kw = dict(layer=l, streams=streams, new_len=new_len)
        oa_s = _attn_a_sample(qa_s, ca_k, ca_v, kv_s[0], kv_s[1], lam_args, heads=h_a, **sample_kw)
        ob_s = _attn_b_sample(qb_s, cb_k, cb_v, kv_s[2], kv_s[3], heads=h_b, **sample_kw)
        oc_s = _attn_c_sample(qc_s, cc_k, cc_v, kv_s[4], kv_s[5], bias_far, bias_own, heads=h_c, **sample_kw)

        xp = _merge(xp, oa_p, ob_p, oc_p, g_p, wb16, wo16, layer=l)
        xs = _merge(xs, oa_s, ob_s, oc_s, g_s, wb16, wo16, layer=l)

        w_router = jnp.concatenate(
            [w_group[l], jnp.transpose(w_expert_router[l], (1, 0, 2)).reshape(d_model, N_EXP)], axis=1)
        w_router = jnp.pad(w_router, ((0, 0), (0, LANES - w_router.shape[1])))
        b_router = jnp.pad(jnp.concatenate([b_group[l], b_expert_router[l].reshape(-1)]),
                           (0, LANES - N_GROUPS - N_EXP)).reshape(1, LANES)
        xp, xs = _moe(xp, xs, norm_ffn[l], w_router, b_router, w_gate_e, w_up_e, w_down_e, layer=l)

    heads_of = [h_a, h_a, h_b, h_b, h_c, h_c]
    prompt_kv = [t.reshape(depth, batch, seq, h, HEAD_DIM) for t, h in zip(kv_p, heads_of)]
    prompt_kv[4] = prompt_kv[4][:, :, seq - c_rows:]
    prompt_kv[5] = prompt_kv[5][:, :, seq - c_rows:]
    sample_kv = [t.reshape(depth, streams, new_len, h, HEAD_DIM) for t, h in zip(kv_s, heads_of)]
    return (xp.reshape(batch, seq, d_model), xs.reshape(streams, new_len, d_model), *prompt_kv, *sample_kv)
```

```python
import functools
import math

import jax
import jax.numpy as jnp
from jax import lax
from jax.experimental import pallas as pl
from jax.experimental.pallas import tpu as pltpu

F32 = jnp.float32
BF16 = jnp.bfloat16

CHUNK = 64
HEAD_DIM = 128
D_HALF = HEAD_DIM // 2
C_BAND_CHUNKS = 8
C_WIN = C_BAND_CHUNKS * CHUNK
MAX_REL = 128
N_REL = 2 * MAX_REL + 1
ROPE_THETA = 10000.0
N_GROUPS = 4
E_PER_GROUP = 4
N_EXP = N_GROUPS * E_PER_GROUP
EPS = 1e-6
N_BRANCH = 3

LANES = 128
VMEM_LIMIT = 56 * 1024 * 1024

PROJ_TM = 1024
PROJ_TN = 512
ATTN_A_TQ = 512
ATTN_B_TQ = 512
SB_SUB = 256
SAMPLE_TK = 512
SAMPLE_A_TK = 1024
BAND_GROUP = 4
BAND_ROWS = BAND_GROUP * CHUNK
BAND_KEYS = (BAND_GROUP + C_BAND_CHUNKS) * CHUNK
MERGE_TM = 256
MOE_ROUTE_TM = 512
MOE_ROW_TILE = 256
MOE_TOK_TILE = 256


def _lambda_init(layer_idx):
    return 0.8 - 0.6 * math.exp(-0.3 * layer_idx)


def _dot(a, b):
    return jnp.dot(a, b, preferred_element_type=F32)


def _dot_nt(a, b):
    return lax.dot_general(a, b, (((1,), (1,)), ((), ())), preferred_element_type=F32)


def _split_bf16(x):
    hi = x.astype(BF16)
    lo = (x - hi.astype(F32)).astype(BF16)
    return hi, lo


def _dot_f32acc(x, w_bf16):
    hi, lo = _split_bf16(x)
    return _dot(hi, w_bf16) + _dot(lo, w_bf16)


def _softplus(z):
    return jnp.maximum(z, 0.0) + jnp.log(1.0 + jnp.exp(-jnp.abs(z)))


def _params(sem, vmem=VMEM_LIMIT):
    return pltpu.CompilerParams(dimension_semantics=sem, vmem_limit_bytes=vmem)


def _bias_kernel(tab_ref, far_ref, own_ref, grp_ref):
    r = pl.program_id(0)
    half = BAND_KEYS - 2 * CHUNK

    def rel_index(shape, key_shift):
        qi = lax.broadcasted_iota(jnp.int32, shape, 0)
        kj = lax.broadcasted_iota(jnp.int32, shape, 1) - key_shift
        return jnp.clip(qi + C_WIN - kj, -MAX_REL, MAX_REL) + MAX_REL, kj

    idx_even, kj_even = rel_index((CHUNK, half), 0)
    idx_odd, kj_odd = rel_index((CHUNK, half), CHUNK)

    def body(t, carry):
        val = tab_ref[r, t]
        return tuple(jnp.where(idx == t, val, acc) for idx, acc in zip((idx_even, idx_odd), carry))

    zeros = jnp.zeros((CHUNK, half), F32)
    even, odd = lax.fori_loop(0, N_REL, body, (zeros, zeros))
    far_ref[0] = even[:, :C_WIN]
    own_ref[0] = even[:, C_WIN:C_WIN + CHUNK]
    band = lambda kj: jnp.logical_and(kj >= 0, kj < C_WIN + CHUNK)
    even = jnp.where(band(kj_even), even, -jnp.inf)
    odd = jnp.where(band(kj_odd), odd, -jnp.inf)
    for a in range(BAND_GROUP):
        lead = (a // 2) * 2 * CHUNK
        tail = BAND_KEYS - lead - half
        pieces = [even if a % 2 == 0 else odd]
        if lead:
            pieces.insert(0, jnp.full((CHUNK, lead), -jnp.inf, F32))
        if tail:
            pieces.append(jnp.full((CHUNK, tail), -jnp.inf, F32))
        grp_ref[0, a * CHUNK:(a + 1) * CHUNK, :] = jnp.concatenate(pieces, axis=1)


def _expand_bias(table):
    rows = table.shape[0]
    shapes = [(CHUNK, C_WIN), (CHUNK, CHUNK), (BAND_ROWS, BAND_KEYS)]
    return pl.pallas_call(
        _bias_kernel,
        grid=(rows,),
        in_specs=[pl.BlockSpec(memory_space=pltpu.SMEM)],
        out_specs=[pl.BlockSpec((1,) + s, lambda r: (r, 0, 0)) for s in shapes],
        out_shape=[jax.ShapeDtypeStruct((rows,) + s, F32) for s in shapes],
        compiler_params=_params(("arbitrary",)),
        name="bias_expand",
    )(table)


_PROJ_NAMES = ["qa", "ka", "va", "qb", "kb", "vb", "qc", "kc", "vc", "g"]
_PROJ_KV = [1, 2, 4, 5, 7, 8]


def _proj_kernel(*refs, bounds, n_prev):
    (x_ref, ng_ref, w_ref, bg_ref, qna_ref, kna_ref, qnc_ref, knc_ref, cos_ref, sin_ref) = refs[:10]
    (qa_ref, ka_ref, va_ref, qb_ref, kb_ref, vb_ref, qc_ref, kc_ref, vc_ref, g_ref,
     xn_ref) = refs[10 + n_prev:]
    j = pl.program_id(1)

    @pl.when(j == 0)
    def _():
        x = x_ref[...]
        ms = jnp.mean(x * x, axis=-1, keepdims=True)
        xn_ref[...] = (x * lax.rsqrt(ms + EPS) * ng_ref[...]).astype(BF16)

    z = _dot(xn_ref[...], w_ref[...])
    tm, tn = z.shape
    heads = tn // HEAD_DIM

    r_i = lax.broadcasted_iota(jnp.int32, (HEAD_DIM, HEAD_DIM), 0)
    c_i = lax.broadcasted_iota(jnp.int32, (HEAD_DIM, HEAD_DIM), 1)
    ones_map = ((r_i // D_HALF) == (c_i // D_HALF)).astype(BF16)
    ones_head = jnp.ones((HEAD_DIM, HEAD_DIM), BF16)
    lane = lax.broadcasted_iota(jnp.int32, (tm, HEAD_DIM), 1)
    first_half = (lane % D_HALF) < (D_HALF // 2)

    def map_norm_rope(zh, gain):
        ms = _dot_f32acc(zh * zh, ones_map) * (1.0 / D_HALF)
        y = zh * lax.rsqrt(ms + EPS) * gain
        partner = jnp.where(first_half,
                            pltpu.roll(y, HEAD_DIM - D_HALF // 2, 1),
                            pltpu.roll(y, D_HALF // 2, 1))
        return y * cos_ref[...] + partner * sin_ref[...]

    def head_norm(zh, gain):
        ms = _dot_f32acc(zh * zh, ones_head) * (1.0 / HEAD_DIM)
        return zh * lax.rsqrt(ms + EPS) * gain

    def per_head(fn, out_ref, scale=None):
        for hh in range(heads):
            sl = slice(hh * HEAD_DIM, (hh + 1) * HEAD_DIM)
            y = fn(z[:, sl])
            if scale is not None:
                y = y * scale
            out_ref[:, sl] = y.astype(out_ref.dtype)

    def in_range(name):
        lo, hi = bounds[name]
        return jnp.logical_and(j >= lo, j < hi)

    @pl.when(in_range("qa"))
    def _():
        per_head(lambda zh: map_norm_rope(zh, qna_ref[...]), qa_ref, scale=D_HALF ** -0.5)

    @pl.when(in_range("ka"))
    def _():
        per_head(lambda zh: map_norm_rope(zh, kna_ref[...]), ka_ref)

    @pl.when(in_range("va"))
    def _():
        va_ref[...] = z

    @pl.when(in_range("qb"))
    def _():
        qb_ref[...] = z.astype(BF16)

    @pl.when(in_range("kb"))
    def _():
        kb_ref[...] = z

    @pl.when(in_range("vb"))
    def _():
        vb_ref[...] = z

    @pl.when(in_range("qc"))
    def _():
        per_head(lambda zh: head_norm(zh, qnc_ref[...]), qc_ref)

    @pl.when(in_range("kc"))
    def _():
        per_head(lambda zh: head_norm(zh, knc_ref[...]), kc_ref)

    @pl.when(in_range("vc"))
    def _():
        vc_ref[...] = z

    @pl.when(in_range("g"))
    def _():
        g_ref[...] = (0.5 * jnp.tanh(0.5 * (z + bg_ref[...])) + 0.5).astype(BF16)


def _proj(x, norm_g, w16, b_gate, qn_a, kn_a, qn_c, kn_c, cos_tab, sin_tab, prev_kv, *, layer, depth,
          w_a, w_b, w_c, d_model):
    n = x.shape[0]
    tm, tn = PROJ_TM, PROJ_TN
    n_in = w16.shape[2]
    assert n % tm == 0 and n_in % tn == 0 and w_a % tn == 0 and w_b == tn and w_c == tn
    assert cos_tab.shape[0] % tm == 0
    tab_tiles = cos_tab.shape[0] // tm
    na = w_a // tn
    widths = [na, na, na, 1, 1, 1, 1, 1, 1, N_BRANCH * d_model // tn]
    bounds, start = {}, 0
    for name, wd in zip(_PROJ_NAMES, widths):
        bounds[name] = (start, start + wd)
        start += wd
    assert start == n_in // tn

    def col_block(name):
        lo, hi = bounds[name]
        return lambda j: jnp.clip(j - lo, 0, hi - lo - 1)

    tab_map = lambda i, j: (i % tab_tiles, 0)
    g_col = col_block("g")
    vec = lambda width: pl.BlockSpec((1, width), lambda i, j: (0, 0))
    in_specs = [
        pl.BlockSpec((tm, d_model), lambda i, j: (i, 0), pipeline_mode=pl.Buffered(1)),
        vec(d_model),
        pl.BlockSpec((None, d_model, tn), lambda i, j: (layer, 0, j)),
        pl.BlockSpec((1, tn), lambda i, j: (0, g_col(j))),
        vec(HEAD_DIM), vec(HEAD_DIM), vec(HEAD_DIM), vec(HEAD_DIM),
        pl.BlockSpec((tm, HEAD_DIM), tab_map),
        pl.BlockSpec((tm, HEAD_DIM), tab_map),
    ]
    out_widths = [w_a, w_a, w_a, w_b, w_b, w_b, w_c, w_c, w_c, N_BRANCH * d_model]
    out_specs, out_shape = [], []
    for idx, (name, wd) in enumerate(zip(_PROJ_NAMES, out_widths)):
        cb = col_block(name)
        if idx in _PROJ_KV:
            out_specs.append(pl.BlockSpec((None, tm, tn), lambda i, j, cb=cb: (layer, i, cb(j))))
            out_shape.append(jax.ShapeDtypeStruct((depth, n, wd), F32))
        else:
            out_specs.append(pl.BlockSpec((tm, tn), lambda i, j, cb=cb: (i, cb(j))))
            out_shape.append(jax.ShapeDtypeStruct((n, wd), BF16))
    prev = [] if prev_kv is None else list(prev_kv)
    aliases = {len(in_specs) + k: _PROJ_KV[k] for k in range(len(prev))}
    in_specs = in_specs + [pl.BlockSpec(memory_space=pl.ANY)] * len(prev)
    tile2 = lambda v: jnp.concatenate([v, v]).reshape(1, HEAD_DIM)
    outs = pl.pallas_call(
        functools.partial(_proj_kernel, bounds=bounds, n_prev=len(prev)),
        grid=(n // tm, n_in // tn),
        in_specs=in_specs,
        out_specs=out_specs,
        out_shape=out_shape,
        scratch_shapes=[pltpu.VMEM((tm, d_model), BF16)],
        input_output_aliases=aliases,
        compiler_params=_params(("arbitrary", "arbitrary")),
        name="proj",
    )(x, norm_g.reshape(1, -1), w16, b_gate.reshape(1, -1), tile2(qn_a), tile2(kn_a),
      qn_c.reshape(1, -1), kn_c.reshape(1, -1), cos_tab, sin_tab, *prev)
    return outs[0], outs[3], outs[6], outs[9], [outs[k] for k in _PROJ_KV]


def _head_rows(ref, h, heads, start=0, size=None):
    size = ref.shape[0] // heads if size is None else size
    return ref[pl.ds(start * heads + h, size, stride=heads), :]


def _stack_maps(q):
    lane = lax.broadcasted_iota(jnp.int32, q.shape, 1)
    zero = jnp.zeros_like(q)
    return jnp.concatenate([jnp.where(lane < D_HALF, q, zero), jnp.where(lane >= D_HALF, q, zero)], axis=0)


def _diff_lambda(lq1_ref, lk1_ref, lq2_ref, lk2_ref, lam_init):
    a = jnp.exp(jnp.sum(lq1_ref[...] * lk1_ref[...], axis=1, keepdims=True))
    b = jnp.exp(jnp.sum(lq2_ref[...] * lk2_ref[...], axis=1, keepdims=True))
    return a - b + lam_init


def _diff_finish(l, acc, t, lam, subln, lam_init):
    o1 = acc[:t] / l[:t]
    o2 = acc[t:] / l[t:]
    o = o1 - lam * o2
    ms = jnp.mean(o * o, axis=-1, keepdims=True)
    return o * lax.rsqrt(ms + EPS) * subln * (1.0 - lam_init)


def _attn_a_prompt_kernel(q_ref, k_ref, v_ref, lq1_ref, lk1_ref, lq2_ref, lk2_ref, sg_ref, o_ref,
                          s_ref, m_ref, l_ref, acc_ref, *, tq, lam_init):
    qi = pl.program_id(2)
    qq = _stack_maps(q_ref[...])

    def lane_fold(op, x, acc):
        for c in range(x.shape[1] // LANES):
            acc = op(acc, x[:, c * LANES:(c + 1) * LANES])
        return acc

    def score_block(kb, masked):
        start = pl.multiple_of(kb * tq, tq)
        s = _dot_nt(qq, k_ref[pl.ds(start, tq), :].astype(BF16))
        if masked:
            q_chunk = (lax.broadcasted_iota(jnp.int32, (2 * tq, 1), 0) % tq) // CHUNK
            k_chunk = lax.broadcasted_iota(jnp.int32, (1, tq), 1) // CHUNK
            s = jnp.where(k_chunk <= q_chunk, s, -jnp.inf)
        s_ref[kb] = s
        m_ref[...] = lane_fold(jnp.maximum, s, m_ref[...])

    def score_body(kb, carry):
        score_block(kb, False)
        return carry

    m_ref[...] = jnp.full(m_ref.shape, -jnp.inf, F32)
    lax.fori_loop(0, qi, score_body, 0)
    score_block(qi, True)
    m = jnp.max(m_ref[...], axis=1, keepdims=True)

    def value_body(kb, carry):
        start = pl.multiple_of(kb * tq, tq)
        p = jnp.exp(s_ref[kb] - m)
        l_ref[...] = lane_fold(jnp.add, p, l_ref[...])
        acc_ref[...] += _dot(p.astype(BF16), v_ref[pl.ds(start, tq), :].astype(BF16))
        return carry

    l_ref[...] = jnp.zeros(l_ref.shape, F32)
    acc_ref[...] = jnp.zeros(acc_ref.shape, F32)
    lax.fori_loop(0, qi + 1, value_body, 0)
    l = jnp.sum(l_ref[...], axis=1, keepdims=True)
    lam = _diff_lambda(lq1_ref, lk1_ref, lq2_ref, lk2_ref, lam_init)
    o_ref[...] = _diff_finish(l, acc_ref[...], tq, lam, sg_ref[...], lam_init).astype(o_ref.dtype)


def _lam_specs():
    spec = pl.BlockSpec((1, D_HALF), lambda *_: (0, 0))
    return [spec, spec, spec, spec, pl.BlockSpec((1, HEAD_DIM), lambda *_: (0, 0))]


def _lam_args(lq1, lk1, lq2, lk2, subln):
    return (lq1.reshape(1, -1), lk1.reshape(1, -1), lq2.reshape(1, -1), lk2.reshape(1, -1),
            subln.reshape(1, -1))


def _attn_a_prompt(q16, k32, v32, lam_args, *, layer, batch, seq, heads):
    n = q16.shape[0]
    tq = ATTN_A_TQ
    assert seq % tq == 0 and tq % CHUNK == 0
    nq = seq // tq
    kv_spec = pl.BlockSpec((None, seq, HEAD_DIM), lambda b, h, qi: (layer, b, h))
    return pl.pallas_call(
        functools.partial(_attn_a_prompt_kernel, tq=tq, lam_init=_lambda_init(layer)),
        grid=(batch, heads, nq),
        in_specs=[pl.BlockSpec((tq, HEAD_DIM), lambda b, h, qi: (b * nq + qi, h)), kv_spec, kv_spec]
                 + _lam_specs(),
        out_specs=pl.BlockSpec((tq, HEAD_DIM), lambda b, h, qi: (b * nq + qi, h)),
        out_shape=jax.ShapeDtypeStruct((n, heads * HEAD_DIM), BF16),
        scratch_shapes=[pltpu.VMEM((nq, 2 * tq, tq), F32),
                        pltpu.VMEM((2 * tq, LANES), F32), pltpu.VMEM((2 * tq, LANES), F32),
                        pltpu.VMEM((2 * tq, HEAD_DIM), F32)],
        compiler_params=_params(("arbitrary", "arbitrary", "arbitrary")),
        name="attn_a_prompt",
    )(q16, k32, v32, *lam_args)


def _attn_a_sample_kernel(q_ref, ck_ref, cv_ref, kn_ref, vn_ref, lq1_ref, lk1_ref, lq2_ref, lk2_ref, sg_ref,
                          o_ref, qq_ref, s_ref, sn_ref, m_ref, mf_ref, l_ref, acc_ref, *,
                          heads, new_len, lam_init):
    phase = pl.program_id(1)
    kt = pl.program_id(2)
    nkt = pl.num_programs(2)

    def lane_fold(op, x, acc):
        for c in range(x.shape[1] // LANES):
            acc = op(acc, x[:, c * LANES:(c + 1) * LANES])
        return acc

    @pl.when(jnp.logical_and(phase == 0, kt == 0))
    def _():
        for h in range(heads):
            sl = slice(h * HEAD_DIM, (h + 1) * HEAD_DIM)
            qq = _stack_maps(q_ref[:, sl])
            qq_ref[h] = qq
            s_new = _dot_nt(qq, kn_ref[:, sl].astype(BF16))
            sn_ref[h] = s_new
            m_ref[h] = jnp.broadcast_to(jnp.max(s_new, axis=1, keepdims=True), m_ref.shape[1:])

    @pl.when(phase == 0)
    def _():
        for h in range(heads):
            s = _dot_nt(qq_ref[h], _head_rows(ck_ref, h, heads).astype(BF16))
            s_ref[h, kt] = s
            m_ref[h] = lane_fold(jnp.maximum, s, m_ref[h])

    @pl.when(jnp.logical_and(phase == 1, kt == 0))
    def _():
        lane = lax.broadcasted_iota(jnp.int32, l_ref.shape[1:], 1)
        for h in range(heads):
            sl = slice(h * HEAD_DIM, (h + 1) * HEAD_DIM)
            m = jnp.max(m_ref[h], axis=1, keepdims=True)
            mf_ref[h] = m
            p_new = jnp.exp(sn_ref[h] - m)
            l_ref[h] = jnp.where(lane == 0, jnp.sum(p_new, axis=1, keepdims=True), 0.0)
            acc_ref[h] = _dot(p_new.astype(BF16), vn_ref[:, sl].astype(BF16))

    @pl.when(phase == 1)
    def _():
        for h in range(heads):
            p = jnp.exp(s_ref[h, kt] - mf_ref[h])
            l_ref[h] = lane_fold(jnp.add, p, l_ref[h])
            acc_ref[h] += _dot(p.astype(BF16), _head_rows(cv_ref, h, heads).astype(BF16))

    @pl.when(jnp.logical_and(phase == 1, kt == nkt - 1))
    def _():
        lam = _diff_lambda(lq1_ref, lk1_ref, lq2_ref, lk2_ref, lam_init)
        for h in range(heads):
            l = jnp.sum(l_ref[h], axis=1, keepdims=True)
            y = _diff_finish(l, acc_ref[h], new_len, lam, sg_ref[...], lam_init)
            o_ref[:, h * HEAD_DIM:(h + 1) * HEAD_DIM] = y.astype(o_ref.dtype)


def _attn_a_sample(q16, cache_k, cache_v, k32, v32, lam_args, *, layer, streams, new_len, heads):
    past = cache_k.shape[2] // heads
    tk = SAMPLE_A_TK
    assert past % tk == 0
    nkt = past // tk
    width = heads * HEAD_DIM
    q_spec = pl.BlockSpec((new_len, width), lambda b, p, kt: (b, 0))
    new_spec = pl.BlockSpec((None, new_len, width), lambda b, p, kt: (layer, b, 0))
    k_spec = pl.BlockSpec((None, None, tk * heads, HEAD_DIM),
                          lambda b, p, kt: (layer, b, jnp.where(p == 0, kt, nkt - 1), 0))
    v_spec = pl.BlockSpec((None, None, tk * heads, HEAD_DIM),
                          lambda b, p, kt: (layer, b, jnp.where(p == 0, 0, kt), 0))
    rows = 2 * new_len
    return pl.pallas_call(
        functools.partial(_attn_a_sample_kernel, heads=heads, new_len=new_len,
                          lam_init=_lambda_init(layer)),
        grid=(streams, 2, nkt),
        in_specs=[q_spec, k_spec, v_spec, new_spec, new_spec] + _lam_specs(),
        out_specs=q_spec,
        out_shape=jax.ShapeDtypeStruct(q16.shape, BF16),
        scratch_shapes=[pltpu.VMEM((heads, rows, HEAD_DIM), BF16),
                        pltpu.VMEM((heads, nkt, rows, tk), F32),
                        pltpu.VMEM((heads, rows, new_len), F32),
                        pltpu.VMEM((heads, rows, LANES), F32),
                        pltpu.VMEM((heads, rows, 1), F32),
                        pltpu.VMEM((heads, rows, LANES), F32),
                        pltpu.VMEM((heads, rows, HEAD_DIM), F32)],
        compiler_params=_params(("arbitrary", "arbitrary", "arbitrary")),
        name="attn_a_sample",
    )(q16, cache_k, cache_v, k32, v32, *lam_args)


def _upper_ones(t):
    r = lax.broadcasted_iota(jnp.int32, (t, t), 0)
    c = lax.broadcasted_iota(jnp.int32, (t, t), 1)
    return (r > c).astype(BF16)


def _sb_block(q, kblk, vblk, ones_u, c, acc, scale, strict):
    z = _dot_nt(q, kblk) * scale
    sp = _softplus(z)
    log_stay = -sp
    if strict is not None:
        log_stay = jnp.where(strict, log_stay, 0.0)
    between = _dot_f32acc(log_stay, ones_u)
    w = jnp.exp((z - sp) + between + c)
    if strict is not None:
        w = jnp.where(strict, w, 0.0)
    acc = acc + _dot(w.astype(BF16), vblk)
    c = c + jnp.sum(log_stay, axis=1, keepdims=True)
    return c, acc


def _attn_b_prompt_kernel(q_ref, k_ref, v_ref, o_ref, c_ref, acc_ref, *, tq, tk):
    qi = pl.program_id(2)
    q = q_ref[...]
    scale = HEAD_DIM ** -0.5
    ones_u = _upper_ones(tk)
    per_q = tq // tk
    c_ref[...] = jnp.zeros(c_ref.shape, F32)
    acc_ref[...] = jnp.zeros(acc_ref.shape, F32)

    def block(kb, masked):
        start = pl.multiple_of(kb * tk, tk)
        strict = None
        if masked:
            row = lax.broadcasted_iota(jnp.int32, (tq, 1), 0)
            col = lax.broadcasted_iota(jnp.int32, (1, tk), 1)
            strict = start + col < qi * tq + row
        c, acc = _sb_block(q, k_ref[pl.ds(start, tk), :].astype(BF16), v_ref[pl.ds(start, tk), :].astype(BF16),
                           ones_u, c_ref[...], acc_ref[...], scale, strict)
        c_ref[...] = c
        acc_ref[...] = acc

    for d in reversed(range(per_q)):
        block(qi * per_q + d, True)

    def body(t, carry):
        block(qi * per_q - 1 - t, False)
        return carry

    lax.fori_loop(0, qi * per_q, body, 0)
    o_ref[...] = acc_ref[...].astype(o_ref.dtype)


def _attn_b_prompt(q16, k32, v32, *, layer, batch, seq, heads):
    n = q16.shape[0]
    tq, tk = ATTN_B_TQ, SB_SUB
    assert seq % tq == 0 and tq % tk == 0
    nq = seq // tq
    kv_spec = pl.BlockSpec((None, seq, HEAD_DIM), lambda b, h, qi: (layer, b, h))
    return pl.pallas_call(
        functools.partial(_attn_b_prompt_kernel, tq=tq, tk=tk),
        grid=(batch, heads, nq),
        in_specs=[pl.BlockSpec((tq, HEAD_DIM), lambda b, h, qi: (b * nq + qi, h)), kv_spec, kv_spec],
        out_specs=pl.BlockSpec((tq, HEAD_DIM), lambda b, h, qi: (b * nq + qi, h)),
        out_shape=jax.ShapeDtypeStruct((n, heads * HEAD_DIM), BF16),
        scratch_shapes=[pltpu.VMEM((tq, 1), F32), pltpu.VMEM((tq, HEAD_DIM), F32)],
        compiler_params=_params(("arbitrary", "arbitrary", "arbitrary")),
        name="attn_b_prompt",
    )(q16, k32, v32)


def _attn_b_sample_kernel(q_ref, ck_ref, cv_ref, kn_ref, vn_ref, o_ref, u_ref, c_ref, acc_ref, *,
                          heads, new_len):
    b = pl.program_id(0)
    kt = pl.program_id(1)
    nkt = pl.num_programs(1)
    scale = HEAD_DIM ** -0.5
    tk = ck_ref.shape[0] // heads
    head_cols = lambda h: slice(h * HEAD_DIM, (h + 1) * HEAD_DIM)
    head_rows = lambda h: slice(h * new_len, (h + 1) * new_len)

    @pl.when(jnp.logical_and(b == 0, kt == 0))
    def _():
        u_ref[...] = _upper_ones(tk)

    @pl.when(kt == 0)
    def _():
        ones_new = _upper_ones(new_len)
        row = lax.broadcasted_iota(jnp.int32, (new_len, new_len), 0)
        col = lax.broadcasted_iota(jnp.int32, (new_len, new_len), 1)
        for h in range(heads):
            sl = head_cols(h)
            c, acc = _sb_block(q_ref[:, sl], kn_ref[:, sl].astype(BF16), vn_ref[:, sl].astype(BF16), ones_new,
                               jnp.zeros((new_len, 1), F32), jnp.zeros((new_len, HEAD_DIM), F32),
                               scale, col < row)
            c_ref[head_rows(h), :] = c
            acc_ref[h] = acc

    z = jnp.concatenate([_dot_nt(q_ref[:, head_cols(h)], _head_rows(ck_ref, h, heads).astype(BF16))
                         for h in range(heads)], axis=0) * scale
    sp = _softplus(z)
    log_stay = -sp
    between = _dot_f32acc(log_stay, u_ref[...])
    w = jnp.exp((z - sp) + between + c_ref[...]).astype(BF16)
    for h in range(heads):
        acc_ref[h] += _dot(w[head_rows(h), :], _head_rows(cv_ref, h, heads).astype(BF16))
    c_ref[...] += jnp.sum(log_stay, axis=1, keepdims=True)

    @pl.when(kt == nkt - 1)
    def _():
        for h in range(heads):
            o_ref[:, head_cols(h)] = acc_ref[h].astype(o_ref.dtype)


def _attn_b_sample(q16, cache_k, cache_v, k32, v32, *, layer, streams, new_len, heads):
    past = cache_k.shape[2] // heads
    tk = SAMPLE_TK
    nkt = past // tk
    assert past % tk == 0
    width = heads * HEAD_DIM
    q_spec = pl.BlockSpec((new_len, width), lambda b, kt: (b, 0))
    new_spec = pl.BlockSpec((None, new_len, width), lambda b, kt: (layer, b, 0))
    cache_spec = pl.BlockSpec((None, None, tk * heads, HEAD_DIM), lambda b, kt: (layer, b, nkt - 1 - kt, 0))
    return pl.pallas_call(
        functools.partial(_attn_b_sample_kernel, heads=heads, new_len=new_len),
        grid=(streams, nkt),
        in_specs=[q_spec, cache_spec, cache_spec, new_spec, new_spec],
        out_specs=q_spec,
        out_shape=jax.ShapeDtypeStruct(q16.shape, BF16),
        scratch_shapes=[pltpu.VMEM((tk, tk), BF16),
                        pltpu.VMEM((heads * new_len, 1), F32),
                        pltpu.VMEM((heads, new_len, HEAD_DIM), F32)],
        compiler_params=_params(("arbitrary", "arbitrary")),
        name="attn_b_sample",
    )(q16, cache_k, cache_v, k32, v32)


def _attn_c_prompt_kernel(q_ref, k_ref, v_ref, bgrp_ref, o_ref, k16_ref, v16_ref, *, seq):
    k16_ref[0:C_WIN, :] = jnp.zeros((C_WIN, HEAD_DIM), BF16)
    v16_ref[0:C_WIN, :] = jnp.zeros((C_WIN, HEAD_DIM), BF16)
    k16_ref[C_WIN:, :] = k_ref[...].astype(BF16)
    v16_ref[C_WIN:, :] = v_ref[...].astype(BF16)
    scale = HEAD_DIM ** -0.5
    key_chunk = lax.broadcasted_iota(jnp.int32, (1, BAND_KEYS), 1) // CHUNK

    def group(gi):
        r0 = pl.multiple_of(gi * BAND_ROWS, BAND_ROWS)
        s = _dot_nt(q_ref[pl.ds(r0, BAND_ROWS), :], k16_ref[pl.ds(r0, BAND_KEYS), :]) * scale + bgrp_ref[0]
        s = jnp.where(key_chunk + gi * BAND_GROUP >= C_BAND_CHUNKS, s, -jnp.inf)
        m = jnp.max(s, axis=1, keepdims=True)
        p = jnp.exp(s - m)
        l = jnp.sum(p, axis=1, keepdims=True)
        o = _dot(p.astype(BF16), v16_ref[pl.ds(r0, BAND_KEYS), :]) / l
        o_ref[pl.ds(r0, BAND_ROWS), :] = o.astype(o_ref.dtype)

    def body(t, carry):
        group(2 * t)
        group(2 * t + 1)
        return carry

    lax.fori_loop(0, seq // BAND_ROWS // 2, body, 0)


def _attn_c_prompt(q16, k32, v32, bias_grp, *, layer, batch, seq, heads):
    n = q16.shape[0]
    assert seq % (2 * BAND_ROWS) == 0
    kv_spec = pl.BlockSpec((None, seq, HEAD_DIM), lambda b, h: (layer, b, h))
    return pl.pallas_call(
        functools.partial(_attn_c_prompt_kernel, seq=seq),
        grid=(batch, heads),
        in_specs=[pl.BlockSpec((seq, HEAD_DIM), lambda b, h: (b, h)), kv_spec, kv_spec,
                  pl.BlockSpec((1, BAND_ROWS, BAND_KEYS), lambda b, h: (layer * heads + h, 0, 0))],
        out_specs=pl.BlockSpec((seq, HEAD_DIM), lambda b, h: (b, h)),
        out_shape=jax.ShapeDtypeStruct((n, heads * HEAD_DIM), BF16),
        scratch_shapes=[pltpu.VMEM((seq + C_WIN, HEAD_DIM), BF16),
                        pltpu.VMEM((seq + C_WIN, HEAD_DIM), BF16)],
        compiler_params=_params(("arbitrary", "arbitrary")),
        name="attn_c_prompt",
    )(q16, k32, v32, bias_grp)


def _attn_c_sample_kernel(q_ref, ck_ref, cv_ref, kn_ref, vn_ref, bfar_ref, bown_ref, o_ref, *, heads):
    scale = HEAD_DIM ** -0.5
    for h in range(heads):
        sl = slice(h * HEAD_DIM, (h + 1) * HEAD_DIM)
        q = q_ref[:, sl]
        s_far = _dot_nt(q, _head_rows(ck_ref, h, heads).astype(BF16)) * scale + bfar_ref[h]
        s_own = _dot_nt(q, kn_ref[:, sl].astype(BF16)) * scale + bown_ref[h]
        m = jnp.maximum(jnp.max(s_far, axis=1, keepdims=True), jnp.max(s_own, axis=1, keepdims=True))
        p_far = jnp.exp(s_far - m)
        p_own = jnp.exp(s_own - m)
        l = jnp.sum(p_far, axis=1, keepdims=True) + jnp.sum(p_own, axis=1, keepdims=True)
        o = (_dot(p_far.astype(BF16), _head_rows(cv_ref, h, heads).astype(BF16))
             + _dot(p_own.astype(BF16), vn_ref[:, sl].astype(BF16))) / l
        o_ref[:, sl] = o.astype(o_ref.dtype)


def _attn_c_sample(q16, cache_k, cache_v, k32, v32, bias_far, bias_own, *, layer, streams, new_len, heads):
    assert cache_k.shape[2] == C_WIN * heads and new_len == CHUNK
    width = heads * HEAD_DIM
    q_spec = pl.BlockSpec((new_len, width), lambda b: (b, 0))
    new_spec = pl.BlockSpec((None, new_len, width), lambda b: (layer, b, 0))
    cache_spec = pl.BlockSpec((None, None, C_WIN * heads, HEAD_DIM), lambda b: (layer, b, 0, 0))
    return pl.pallas_call(
        functools.partial(_attn_c_sample_kernel, heads=heads),
        grid=(streams,),
        in_specs=[q_spec, cache_spec, cache_spec, new_spec, new_spec,
                  pl.BlockSpec((heads, CHUNK, C_WIN), lambda b: (layer, 0, 0)),
                  pl.BlockSpec((heads, CHUNK, CHUNK), lambda b: (layer, 0, 0))],
        out_specs=q_spec,
        out_shape=jax.ShapeDtypeStruct(q16.shape, BF16),
        compiler_params=_params(("arbitrary",)),
        name="attn_c_sample",
    )(q16, cache_k, cache_v, k32, v32, bias_far, bias_own)


def _merge_kernel(x_ref, oa_ref, ob_ref, oc_ref, g_ref, wb_ref, wo_ref, o_ref, *, w_a, w_b, d_model):
    ya = _dot(oa_ref[...], wb_ref[0:w_a, :])
    yb = _dot(ob_ref[...], wb_ref[w_a:w_a + w_b, :])
    yc = _dot(oc_ref[...], wb_ref[w_a + w_b:, :])
    m = (g_ref[:, 0:d_model].astype(F32) * ya + g_ref[:, d_model:2 * d_model].astype(F32) * yb
         + g_ref[:, 2 * d_model:].astype(F32) * yc)
    o_ref[...] = x_ref[...] + _dot(m.astype(BF16), wo_ref[...])


def _merge(x, oa, ob, oc, g, wb16, wo16, *, layer):
    n, d_model = x.shape
    tm = MERGE_TM
    assert n % tm == 0
    w_a, w_b, w_c = oa.shape[1], ob.shape[1], oc.shape[1]
    rows = lambda width: pl.BlockSpec((tm, width), lambda i: (i, 0))
    resident = lambda shape: pl.BlockSpec((None,) + shape[1:], lambda i: (layer, 0, 0),
                                          pipeline_mode=pl.Buffered(1))
    return pl.pallas_call(
        functools.partial(_merge_kernel, w_a=w_a, w_b=w_b, d_model=d_model),
        grid=(n // tm,),
        in_specs=[rows(d_model), rows(w_a), rows(w_b), rows(w_c), rows(N_BRANCH * d_model),
                  resident(wb16.shape), resident(wo16.shape)],
        out_specs=rows(d_model),
        out_shape=jax.ShapeDtypeStruct((n, d_model), F32),
        compiler_params=_params(("arbitrary",)),
        name="merge",
    )(x, oa, ob, oc, g, wb16, wo16)


_R_E1, _R_E2, _R_C1, _R_C2, _R_RANK1, _R_RANK2 = range(6)


def _route_kernel(xp_ref, xs_ref, ng_ref, wr_ref, br_ref, xn_ref, route_ref, cnt_ref, carry_ref, *,
                  prompt_tiles):
    i = pl.program_id(0)
    tm = xp_ref.shape[0]
    lane = lax.broadcasted_iota(jnp.int32, (tm, LANES), 1)
    lane_f = lane.astype(F32)

    @pl.when(i == 0)
    def _():
        carry_ref[...] = jnp.zeros(carry_ref.shape, F32)

    x = jnp.where(i < prompt_tiles, xp_ref[...], xs_ref[...])
    ms = jnp.mean(x * x, axis=-1, keepdims=True)
    xn = x * lax.rsqrt(ms + EPS) * ng_ref[...]
    xn_ref[...] = xn

    x_hi, x_lo = _split_bf16(xn)
    w_hi, w_lo = _split_bf16(wr_ref[...])
    logits = _dot(x_hi, w_hi) + _dot(x_lo, w_hi) + _dot(x_hi, w_lo) + br_ref[...]
    is_group = lane < N_GROUPS
    gl = jnp.where(is_group, logits, -jnp.inf)
    g_max = jnp.max(gl, axis=1, keepdims=True)
    g_idx = jnp.min(jnp.where(gl == g_max, lane_f, float(LANES)), axis=1, keepdims=True)
    p_group = 1.0 / jnp.sum(jnp.where(is_group, jnp.exp(logits - g_max), 0.0), axis=1, keepdims=True)
    in_group = jnp.logical_and(lane >= N_GROUPS, lane < N_GROUPS + N_EXP)
    in_group = jnp.logical_and(in_group, ((lane - N_GROUPS) // E_PER_GROUP).astype(F32) == g_idx)
    el = jnp.where(in_group, logits, -jnp.inf)
    v1 = jnp.max(el, axis=1, keepdims=True)
    i1 = jnp.min(jnp.where(el == v1, lane_f, float(LANES)), axis=1, keepdims=True)
    el2 = jnp.where(lane_f == i1, -jnp.inf, el)
    v2 = jnp.max(el2, axis=1, keepdims=True)
    i2 = jnp.min(jnp.where(el2 == v2, lane_f, float(LANES)), axis=1, keepdims=True)
    t = jnp.exp(v2 - v1)
    c1 = p_group / (1.0 + t)
    c2 = p_group * t / (1.0 + t)
    e1 = i1 - N_GROUPS
    e2 = i2 - N_GROUPS

    a1 = (lane_f == e1).astype(F32)
    a2 = (lane_f == e2).astype(F32)
    a = a1 + a2
    row = lax.broadcasted_iota(jnp.int32, (tm, tm), 0)
    col = lax.broadcasted_iota(jnp.int32, (tm, tm), 1)
    earlier = (col < row).astype(BF16)
    before = _dot(earlier, a.astype(BF16)) + carry_ref[...]
    r1 = jnp.sum(a1 * before, axis=1, keepdims=True)
    r2 = jnp.sum(a2 * before, axis=1, keepdims=True)
    carry_ref[...] += jnp.sum(a, axis=0, keepdims=True)
    cnt_ref[...] = carry_ref[...]

    record = jnp.zeros((tm, LANES), F32)
    for slot, val in ((_R_E1, e1), (_R_E2, e2), (_R_C1, c1), (_R_C2, c2), (_R_RANK1, r1), (_R_RANK2, r2)):
        record = jnp.where(lane == slot, val, record)
    route_ref[...] = record


def _route(xp, xs, norm_g, w_router, b_router):
    tm = MOE_ROUTE_TM
    (n_p, d_model), n_s = xp.shape, xs.shape[0]
    assert n_p % tm == 0 and n_s % tm == 0
    pt, st = n_p // tm, n_s // tm
    n = n_p + n_s
    const = lambda shape: pl.BlockSpec(shape, lambda i: (0, 0))
    return pl.pallas_call(
        functools.partial(_route_kernel, prompt_tiles=pt),
        grid=(pt + st,),
        in_specs=[pl.BlockSpec((tm, d_model), lambda i: (jnp.minimum(i, pt - 1), 0)),
                  pl.BlockSpec((tm, d_model), lambda i: (jnp.maximum(i - pt, 0), 0)),
                  const((1, d_model)), const((d_model, LANES)), const((1, LANES))],
        out_specs=[pl.BlockSpec((tm, d_model), lambda i: (i, 0)),
                   pl.BlockSpec((tm, LANES), lambda i: (i, 0)),
                   const((1, LANES))],
        out_shape=[jax.ShapeDtypeStruct((n, d_model), F32), jax.ShapeDtypeStruct((n, LANES), F32),
                   jax.ShapeDtypeStruct((1, LANES), F32)],
        scratch_shapes=[pltpu.VMEM((1, LANES), F32)],
        compiler_params=_params(("arbitrary",)),
        name="moe_route",
    )(xp, xs, norm_g.reshape(1, -1), w_router, b_router)


def _row_copy(src, src_row, dst, dst_row, sem):
    return pltpu.make_async_copy(src.at[pl.ds(src_row, 1), :], dst.at[pl.ds(dst_row, 1), :], sem)


def _row_token_kernel(pos1_ref, pos2_ref, out_ref, *, n_tokens):
    def init(r, carry):
        out_ref[r] = 0
        return carry

    def place(tok, carry):
        out_ref[pos1_ref[tok]] = tok
        out_ref[pos2_ref[tok]] = tok
        return carry

    lax.fori_loop(0, out_ref.shape[0], init, 0, unroll=8)
    lax.fori_loop(0, n_tokens, place, 0, unroll=8)


def _row_tokens(pos1, pos2, rows):
    return pl.pallas_call(
        functools.partial(_row_token_kernel, n_tokens=pos1.shape[0]),
        grid_spec=pltpu.PrefetchScalarGridSpec(
            num_scalar_prefetch=2, grid=(1,), in_specs=[],
            out_specs=pl.BlockSpec(memory_space=pltpu.SMEM)),
        out_shape=jax.ShapeDtypeStruct((rows,), jnp.int32),
        compiler_params=_params(("arbitrary",)),
        name="moe_row_tokens",
    )(pos1, pos2)


def _expert_kernel(tile_expert_ref, n_tiles_ref, row_token_ref, xn_hbm, wg_ref, wu_ref, wd_ref, y_ref,
                   xbuf, sem, *, tile):
    del tile_expert_ref
    t = pl.program_id(0)
    n_valid = n_tiles_ref[0]
    cur = t % 2
    prev = 1 - cur

    def issue(lo, hi):
        base = t * tile
        for r in range(lo, hi):
            _row_copy(xn_hbm, row_token_ref[base + r], xbuf.at[cur], r, sem.at[cur]).start(priority=r % 2)

    def wait_prev():
        def body(r, carry):
            _row_copy(xn_hbm, 0, xbuf.at[prev], 0, sem.at[prev]).wait()
            return carry

        lax.fori_loop(0, tile, body, 0, unroll=8)

    def swiglu(between):
        x = xbuf[prev].astype(BF16)
        between(0)
        a = _dot(x, wg_ref[...].astype(BF16))
        between(1)
        b = _dot(x, wu_ref[...].astype(BF16))
        between(2)
        y = _dot((jax.nn.silu(a) * b).astype(BF16), wd_ref[...].astype(BF16))
        between(3)
        y_ref[...] = y

    quarter = tile // 4

    @pl.when(jnp.logical_and(t == 0, n_valid > 0))
    def _():
        issue(0, tile)

    @pl.when(jnp.logical_and(t >= 1, t < n_valid))
    def _():
        wait_prev()
        swiglu(lambda i: issue(i * quarter, (i + 1) * quarter))

    @pl.when(jnp.logical_and(t >= 1, t == n_valid))
    def _():
        wait_prev()
        swiglu(lambda i: None)

    @pl.when(t > n_valid)
    def _():
        y_ref[...] = jnp.zeros(y_ref.shape, y_ref.dtype)


def _experts(tile_expert, n_tiles, row_token, xn, w_gate, w_up, w_down, *, layer):
    tile = MOE_ROW_TILE
    rows, d_model = row_token.shape[0], xn.shape[1]
    d_exp = w_gate.shape[-1]
    done = lambda t: jnp.maximum(t - 1, 0)
    w_in = pl.BlockSpec((None, None, d_model, d_exp), lambda t, te, nt, rt: (layer, te[done(t)], 0, 0))
    w_out = pl.BlockSpec((None, None, d_exp, d_model), lambda t, te, nt, rt: (layer, te[done(t)], 0, 0))
    return pl.pallas_call(
        functools.partial(_expert_kernel, tile=tile),
        grid_spec=pltpu.PrefetchScalarGridSpec(
            num_scalar_prefetch=3, grid=(rows // tile + 1,),
            in_specs=[pl.BlockSpec(memory_space=pl.ANY), w_in, w_in, w_out],
            out_specs=pl.BlockSpec((tile, d_model), lambda t, te, nt, rt: (done(t), 0)),
            scratch_shapes=[pltpu.VMEM((2, tile, d_model), F32), pltpu.SemaphoreType.DMA((2,))]),
        out_shape=jax.ShapeDtypeStruct((rows, d_model), F32),
        compiler_params=_params(("arbitrary",)),
        name="moe_experts",
    )(tile_expert, n_tiles, row_token, xn, w_gate, w_up, w_down)


def _combine_kernel(pos1_ref, pos2_ref, xp_ref, xs_ref, route_ref, y_hbm, op_ref, os_ref, ybuf, sem, *,
                    tile, prompt_tiles):
    i = pl.program_id(0)
    base = i * tile

    for r in range(tile):
        _row_copy(y_hbm, pos1_ref[base + r], ybuf.at[0], r, sem).start(priority=0)
        _row_copy(y_hbm, pos2_ref[base + r], ybuf.at[1], r, sem).start(priority=1)

    def wait(r, carry):
        _row_copy(y_hbm, 0, ybuf.at[0], 0, sem).wait()
        _row_copy(y_hbm, 0, ybuf.at[1], 0, sem).wait()
        return carry

    lax.fori_loop(0, tile, wait, 0, unroll=8)
    y = route_ref[:, _R_C1:_R_C1 + 1] * ybuf[0] + route_ref[:, _R_C2:_R_C2 + 1] * ybuf[1]

    @pl.when(i < prompt_tiles)
    def _():
        op_ref[...] = xp_ref[...] + y

    @pl.when(i >= prompt_tiles)
    def _():
        os_ref[...] = xs_ref[...] + y


def _combine(pos1, pos2, xp, xs, route, y_sorted):
    tile = MOE_TOK_TILE
    (n_p, d_model), n_s = xp.shape, xs.shape[0]
    assert n_p % tile == 0 and n_s % tile == 0
    pt, st = n_p // tile, n_s // tile
    p_spec = pl.BlockSpec((tile, d_model), lambda i, p1, p2: (jnp.minimum(i, pt - 1), 0))
    s_spec = pl.BlockSpec((tile, d_model), lambda i, p1, p2: (jnp.maximum(i - pt, 0), 0))
    return pl.pallas_call(
        functools.partial(_combine_kernel, tile=tile, prompt_tiles=pt),
        grid_spec=pltpu.PrefetchScalarGridSpec(
            num_scalar_prefetch=2, grid=(pt + st,),
            in_specs=[p_spec, s_spec, pl.BlockSpec((tile, LANES), lambda i, p1, p2: (i, 0)),
                      pl.BlockSpec(memory_space=pl.ANY)],
            out_specs=[p_spec, s_spec],
            scratch_shapes=[pltpu.VMEM((2, tile, d_model), F32), pltpu.SemaphoreType.DMA(())]),
        out_shape=[jax.ShapeDtypeStruct(xp.shape, F32), jax.ShapeDtypeStruct(xs.shape, F32)],
        compiler_params=_params(("arbitrary",)),
        name="moe_combine",
    )(pos1, pos2, xp, xs, route, y_sorted)


def _moe(xp, xs, norm_g, w_router, b_router, w_gate, w_up, w_down, *, layer):
    n = xp.shape[0] + xs.shape[0]
    tile = MOE_ROW_TILE
    n_exp = w_gate.shape[1]
    row_tiles = (2 * n) // tile + n_exp
    xn, route, counts = _route(xp, xs, norm_g, w_router, b_router)

    counts = counts[0, :n_exp].astype(jnp.int32)
    padded = ((counts + tile - 1) // tile) * tile
    upto = jnp.arange(n_exp)[None, :] <= jnp.arange(n_exp)[:, None]
    ends = jnp.sum(jnp.where(upto, padded[None, :], 0), axis=1)
    offsets = ends - padded
    as_int = lambda lane_idx: route[:, lane_idx].astype(jnp.int32)
    pos1 = offsets[as_int(_R_E1)] + as_int(_R_RANK1)
    pos2 = offsets[as_int(_R_E2)] + as_int(_R_RANK2)
    tile_start = jnp.arange(row_tiles, dtype=jnp.int32) * tile
    tile_expert = jnp.minimum(jnp.sum((tile_start[:, None] >= ends[None, :]).astype(jnp.int32), axis=1),
                              n_exp - 1)
    n_tiles = ends[-1:] // tile

    row_token = _row_tokens(pos1, pos2, row_tiles * tile)
    y_sorted = _experts(tile_expert, n_tiles, row_token, xn, w_gate, w_up, w_down, layer=layer)
    return _combine(pos1, pos2, xp, xs, route, y_sorted)


def _rope_tables(pos):
    half = D_HALF // 2
    inv = ROPE_THETA ** (-2.0 * jnp.arange(half, dtype=F32) / D_HALF)
    ang = pos.astype(F32)[:, None] * inv[None, :]
    cos, sin = jnp.cos(ang), jnp.sin(ang)
    cos_full = jnp.tile(cos, (1, HEAD_DIM // half))
    sin_signed = jnp.tile(jnp.concatenate([-sin, sin], axis=1), (1, HEAD_DIM // D_HALF))
    return cos_full, sin_signed


def kernel(x_prompt, x_sample, cache_a_k, cache_a_v, cache_b_k, cache_b_v, cache_c_k, cache_c_v, norm_mix, w_in, b_gate, q_norm_a, k_norm_a, lam_q1, lam_k1, lam_q2, lam_k2, subln_a, q_norm_c, k_norm_c, rel_bias_c, w_branch, w_out, norm_ffn, w_group, b_group, w_expert_router, b_expert_router, w_gate_e, w_up_e, w_down_e):
    batch, seq, d_model = x_prompt.shape
    streams, new_len, _ = x_sample.shape
    depth, _, past, h_a, _ = cache_a_k.shape
    h_b, h_c = cache_b_k.shape[3], cache_c_k.shape[3]
    w_a, w_b, w_c = h_a * HEAD_DIM, h_b * HEAD_DIM, h_c * HEAD_DIM
    n_prompt, n_sample = batch * seq, streams * new_len
    tm = PROJ_TM
    assert seq % tm == 0 and n_sample % tm == 0 and tm % new_len == 0
    c_rows = min(C_WIN, seq)

    xp = x_prompt.reshape(n_prompt, d_model)
    xs = x_sample.reshape(n_sample, d_model)
    tab_p = _rope_tables(jnp.arange(seq))
    tab_s = _rope_tables(jnp.tile(past + jnp.arange(new_len), tm // new_len))
    bias_far, bias_own, bias_grp = _expand_bias(rel_bias_c.reshape(depth * h_c, N_REL))
    rows_view = lambda c: c.reshape(c.shape[0], c.shape[1], c.shape[2] * c.shape[3], c.shape[4])
    ca_k, ca_v, cb_k, cb_v, cc_k, cc_v = map(rows_view, (cache_a_k, cache_a_v, cache_b_k, cache_b_v,
                                                          cache_c_k, cache_c_v))

    w_in16, wb16, wo16 = w_in.astype(BF16), w_branch.astype(BF16), w_out.astype(BF16)
    kv_p = kv_s = None
    for l in range(depth):
        proj_w = (norm_mix[l], w_in16, b_gate[l], q_norm_a[l], k_norm_a[l], q_norm_c[l],
                  k_norm_c[l])
        proj_kw = dict(layer=l, depth=depth, w_a=w_a, w_b=w_b, w_c=w_c, d_model=d_model)
        qa_p, qb_p, qc_p, g_p, kv_p = _proj(xp, *proj_w, *tab_p, kv_p, **proj_kw)
        qa_s, qb_s, qc_s, g_s, kv_s = _proj(xs, *proj_w, *tab_s, kv_s, **proj_kw)
        lam_args = _lam_args(lam_q1[l], lam_k1[l], lam_q2[l], lam_k2[l], subln_a[l])

        oa_p = _attn_a_prompt(qa_p, kv_p[0], kv_p[1], lam_args, layer=l, batch=batch, seq=seq, heads=h_a)
        ob_p = _attn_b_prompt(qb_p, kv_p[2], kv_p[3], layer=l, batch=batch, seq=seq, heads=h_b)
        oc_p = _attn_c_prompt(qc_p, kv_p[4], kv_p[5], bias_grp, layer=l, batch=batch, seq=seq, heads=h_c)
        sample_kw = dict(layer=l, streams=streams, new_len=new_len)
        oa_s = _attn_a_sample(qa_s, ca_k, ca_v, kv_s[0], kv_s[1], lam_args, heads=h_a, **sample_kw)
        ob_s = _attn_b_sample(qb_s, cb_k, cb_v, kv_s[2], kv_s[3], heads=h_b, **sample_kw)
        oc_s = _attn_c_sample(qc_s, cc_k, cc_v, kv_s[4], kv_s[5], bias_far, bias_own, heads=h_c, **sample_kw)

        xp = _merge(xp, oa_p, ob_p, oc_p, g_p, wb16, wo16, layer=l)
        xs = _merge(xs, oa_s, ob_s, oc_s, g_s, wb16, wo16, layer=l)

        w_router = jnp.concatenate(
            [w_group[l], jnp.transpose(w_expert_router[l], (1, 0, 2)).reshape(d_model, N_EXP)], axis=1)
        w_router = jnp.pad(w_router, ((0, 0), (0, LANES - w_router.shape[1])))
        b_router = jnp.pad(jnp.concatenate([b_group[l], b_expert_router[l].reshape(-1)]),
                           (0, LANES - N_GROUPS - N_EXP)).reshape(1, LANES)
        xp, xs = _moe(xp, xs, norm_ffn[l], w_router, b_router, w_gate_e, w_up_e, w_down_e, layer=l)

    heads_of = [h_a, h_a, h_b, h_b, h_c, h_c]
    prompt_kv = [t.reshape(depth, batch, seq, h, HEAD_DIM) for t, h in zip(kv_p, heads_of)]
    prompt_kv[4] = prompt_kv[4][:, :, seq - c_rows:]
    prompt_kv[5] = prompt_kv[5][:, :, seq - c_rows:]
    sample_kv = [t.reshape(depth, streams, new_len, h, HEAD_DIM) for t, h in zip(kv_s, heads_of)]
    return (xp.reshape(batch, seq, d_model), xs.reshape(streams, new_len, d_model), *prompt_kv, *sample_kv)
```

```python
import functools
import math

import jax
import jax.numpy as jnp
from jax import lax
from jax.experimental import pallas as pl
from jax.experimental.pallas import tpu as pltpu

F32 = jnp.float32
BF16 = jnp.bfloat16

CHUNK = 64
HEAD_DIM = 128
D_HALF = HEAD_DIM // 2
C_BAND_CHUNKS = 8
C_WIN = C_BAND_CHUNKS * CHUNK
MAX_REL = 128
N_REL = 2 * MAX_REL + 1
ROPE_THETA = 10000.0
N_GROUPS = 4
E_PER_GROUP = 4
N_EXP = N_GROUPS * E_PER_GROUP
EPS = 1e-6
N_BRANCH = 3

LANES = 128
VMEM_LIMIT = 56 * 1024 * 1024

PROJ_TM = 1024
PROJ_TN = 512
ATTN_A_TQ = 512
ATTN_B_TQ = 512
SB_SUB = 256
SAMPLE_TK = 512
SAMPLE_A_TK = 1024
BAND_GROUP = 4
BAND_ROWS = BAND_GROUP * CHUNK
BAND_KEYS = (BAND_GROUP + C_BAND_CHUNKS) * CHUNK
MERGE_TM = 256
MOE_ROUTE_TM = 512
MOE_ROW_TILE = 256
MOE_TOK_TILE = 256
ROW_GROUP = 8


def _lambda_init(layer_idx):
    return 0.8 - 0.6 * math.exp(-0.3 * layer_idx)


def _dot(a, b):
    return jnp.dot(a, b, preferred_element_type=F32)


def _dot_nt(a, b):
    return lax.dot_general(a, b, (((1,), (1,)), ((), ())), preferred_element_type=F32)


def _split_bf16(x):
    hi = x.astype(BF16)
    lo = (x - hi.astype(F32)).astype(BF16)
    return hi, lo


def _dot_f32acc(x, w_bf16):
    hi, lo = _split_bf16(x)
    return _dot(hi, w_bf16) + _dot(lo, w_bf16)


def _softplus(z):
    return jnp.maximum(z, 0.0) + jnp.log(1.0 + jnp.exp(-jnp.abs(z)))


def _params(sem, vmem=VMEM_LIMIT):
    return pltpu.CompilerParams(dimension_semantics=sem, vmem_limit_bytes=vmem)


def _bias_kernel(tab_ref, far_ref, own_ref, grp_ref):
    r = pl.program_id(0)
    half = BAND_KEYS - 2 * CHUNK

    def rel_index(shape, key_shift):
        qi = lax.broadcasted_iota(jnp.int32, shape, 0)
        kj = lax.broadcasted_iota(jnp.int32, shape, 1) - key_shift
        return jnp.clip(qi + C_WIN - kj, -MAX_REL, MAX_REL) + MAX_REL, kj

    idx_even, kj_even = rel_index((CHUNK, half), 0)
    idx_odd, kj_odd = rel_index((CHUNK, half), CHUNK)

    def body(t, carry):
        val = tab_ref[r, t]
        return tuple(jnp.where(idx == t, val, acc) for idx, acc in zip((idx_even, idx_odd), carry))

    zeros = jnp.zeros((CHUNK, half), F32)
    even, odd = lax.fori_loop(0, N_REL, body, (zeros, zeros))
    far_ref[0] = even[:, :C_WIN]
    own_ref[0] = even[:, C_WIN:C_WIN + CHUNK]
    band = lambda kj: jnp.logical_and(kj >= 0, kj < C_WIN + CHUNK)
    even = jnp.where(band(kj_even), even, -jnp.inf)
    odd = jnp.where(band(kj_odd), odd, -jnp.inf)
    for a in range(BAND_GROUP):
        lead = (a // 2) * 2 * CHUNK
        tail = BAND_KEYS - lead - half
        pieces = [even if a % 2 == 0 else odd]
        if lead:
            pieces.insert(0, jnp.full((CHUNK, lead), -jnp.inf, F32))
        if tail:
            pieces.append(jnp.full((CHUNK, tail), -jnp.inf, F32))
        grp_ref[0, a * CHUNK:(a + 1) * CHUNK, :] = jnp.concatenate(pieces, axis=1)


def _expand_bias(table):
    rows = table.shape[0]
    shapes = [(CHUNK, C_WIN), (CHUNK, CHUNK), (BAND_ROWS, BAND_KEYS)]
    return pl.pallas_call(
        _bias_kernel,
        grid=(rows,),
        in_specs=[pl.BlockSpec(memory_space=pltpu.SMEM)],
        out_specs=[pl.BlockSpec((1,) + s, lambda r: (r, 0, 0)) for s in shapes],
        out_shape=[jax.ShapeDtypeStruct((rows,) + s, F32) for s in shapes],
        compiler_params=_params(("arbitrary",)),
        name="bias_expand",
    )(table)


_PROJ_NAMES = ["qa", "ka", "va", "qb", "kb", "vb", "qc", "kc", "vc", "g"]
_PROJ_KV = [1, 2, 4, 5, 7, 8]
_PROJ_HEAD_ROWS = [4, 5, 7, 8]


def _proj_kernel(*refs, bounds, n_prev):
    (x_ref, ng_ref, w_ref, bg_ref, qna_ref, kna_ref, qnc_ref, knc_ref, cos_ref, sin_ref) = refs[:10]
    (qa_ref, ka_ref, va_ref, qb_ref, kb_ref, vb_ref, qc_ref, kc_ref, vc_ref, g_ref,
     xn_ref) = refs[10 + n_prev:]
    j = pl.program_id(1)

    @pl.when(j == 0)
    def _():
        x = x_ref[...]
        ms = jnp.mean(x * x, axis=-1, keepdims=True)
        xn_ref[...] = (x * lax.rsqrt(ms + EPS) * ng_ref[...]).astype(BF16)

    z = _dot(xn_ref[...], w_ref[...])
    tm, tn = z.shape
    heads = tn // HEAD_DIM

    r_i = lax.broadcasted_iota(jnp.int32, (HEAD_DIM, HEAD_DIM), 0)
    c_i = lax.broadcasted_iota(jnp.int32, (HEAD_DIM, HEAD_DIM), 1)
    ones_map = ((r_i // D_HALF) == (c_i // D_HALF)).astype(BF16)
    ones_head = jnp.ones((HEAD_DIM, HEAD_DIM), BF16)
    lane = lax.broadcasted_iota(jnp.int32, (tm, HEAD_DIM), 1)
    first_half = (lane % D_HALF) < (D_HALF // 2)

    def map_norm_rope(zh, gain):
        ms = _dot_f32acc(zh * zh, ones_map) * (1.0 / D_HALF)
        y = zh * lax.rsqrt(ms + EPS) * gain
        partner = jnp.where(first_half,
                            pltpu.roll(y, HEAD_DIM - D_HALF // 2, 1),
                            pltpu.roll(y, D_HALF // 2, 1))
        return y * cos_ref[...] + partner * sin_ref[...]

    def head_norm(zh, gain):
        ms = _dot_f32acc(zh * zh, ones_head) * (1.0 / HEAD_DIM)
        return zh * lax.rsqrt(ms + EPS) * gain

    def per_head(fn, out_ref, scale=None, head_rows=False):
        for hh in range(heads):
            sl = slice(hh * HEAD_DIM, (hh + 1) * HEAD_DIM)
            y = fn(z[:, sl])
            if scale is not None:
                y = y * scale
            if head_rows:
                out_ref[pl.ds(hh, tm, stride=heads), :] = y.astype(out_ref.dtype)
            else:
                out_ref[:, sl] = y.astype(out_ref.dtype)

    same = lambda zh: zh

    def in_range(name):
        lo, hi = bounds[name]
        return jnp.logical_and(j >= lo, j < hi)

    @pl.when(in_range("qa"))
    def _():
        per_head(lambda zh: map_norm_rope(zh, qna_ref[...]), qa_ref, scale=D_HALF ** -0.5)

    @pl.when(in_range("ka"))
    def _():
        per_head(lambda zh: map_norm_rope(zh, kna_ref[...]), ka_ref)

    @pl.when(in_range("va"))
    def _():
        va_ref[...] = z

    @pl.when(in_range("qb"))
    def _():
        qb_ref[...] = z.astype(BF16)

    @pl.when(in_range("kb"))
    def _():
        per_head(same, kb_ref, head_rows=True)

    @pl.when(in_range("vb"))
    def _():
        per_head(same, vb_ref, head_rows=True)

    @pl.when(in_range("qc"))
    def _():
        per_head(lambda zh: head_norm(zh, qnc_ref[...]), qc_ref)

    @pl.when(in_range("kc"))
    def _():
        per_head(lambda zh: head_norm(zh, knc_ref[...]), kc_ref, head_rows=True)

    @pl.when(in_range("vc"))
    def _():
        per_head(same, vc_ref, head_rows=True)

    @pl.when(in_range("g"))
    def _():
        g_ref[...] = (0.5 * jnp.tanh(0.5 * (z + bg_ref[...])) + 0.5).astype(BF16)


def _proj(x, norm_g, w16, b_gate, qn_a, kn_a, qn_c, kn_c, cos_tab, sin_tab, prev_kv, *, layer, depth,
          w_a, w_b, w_c, d_model):
    n = x.shape[0]
    tm, tn = PROJ_TM, PROJ_TN
    n_in = w16.shape[2]
    assert n % tm == 0 and n_in % tn == 0 and w_a % tn == 0 and w_b == tn and w_c == tn
    assert cos_tab.shape[0] % tm == 0
    tab_tiles = cos_tab.shape[0] // tm
    na = w_a // tn
    widths = [na, na, na, 1, 1, 1, 1, 1, 1, N_BRANCH * d_model // tn]
    bounds, start = {}, 0
    for name, wd in zip(_PROJ_NAMES, widths):
        bounds[name] = (start, start + wd)
        start += wd
    assert start == n_in // tn

    def col_block(name):
        lo, hi = bounds[name]
        return lambda j: jnp.clip(j - lo, 0, hi - lo - 1)

    tab_map = lambda i, j: (i % tab_tiles, 0)
    g_col = col_block("g")
    vec = lambda width: pl.BlockSpec((1, width), lambda i, j: (0, 0))
    in_specs = [
        pl.BlockSpec((tm, d_model), lambda i, j: (i, 0), pipeline_mode=pl.Buffered(1)),
        vec(d_model),
        pl.BlockSpec((None, d_model, tn), lambda i, j: (layer, 0, j)),
        pl.BlockSpec((1, tn), lambda i, j: (0, g_col(j))),
        vec(HEAD_DIM), vec(HEAD_DIM), vec(HEAD_DIM), vec(HEAD_DIM),
        pl.BlockSpec((tm, HEAD_DIM), tab_map),
        pl.BlockSpec((tm, HEAD_DIM), tab_map),
    ]
    out_widths = [w_a, w_a, w_a, w_b, w_b, w_b, w_c, w_c, w_c, N_BRANCH * d_model]
    out_specs, out_shape = [], []
    for idx, (name, wd) in enumerate(zip(_PROJ_NAMES, out_widths)):
        cb = col_block(name)
        if idx in _PROJ_HEAD_ROWS:
            assert wd == tn
            out_specs.append(pl.BlockSpec((None, tm * (tn // HEAD_DIM), HEAD_DIM), lambda i, j: (layer, i, 0)))
            out_shape.append(jax.ShapeDtypeStruct((depth, n * (tn // HEAD_DIM), HEAD_DIM), F32))
        elif idx in _PROJ_KV:
            out_specs.append(pl.BlockSpec((None, tm, tn), lambda i, j, cb=cb: (layer, i, cb(j))))
            out_shape.append(jax.ShapeDtypeStruct((depth, n, wd), F32))
        else:
            out_specs.append(pl.BlockSpec((tm, tn), lambda i, j, cb=cb: (i, cb(j))))
            out_shape.append(jax.ShapeDtypeStruct((n, wd), BF16))
    prev = [] if prev_kv is None else list(prev_kv)
    aliases = {len(in_specs) + k: _PROJ_KV[k] for k in range(len(prev))}
    in_specs = in_specs + [pl.BlockSpec(memory_space=pl.ANY)] * len(prev)
    tile2 = lambda v: jnp.concatenate([v, v]).reshape(1, HEAD_DIM)
    outs = pl.pallas_call(
        functools.partial(_proj_kernel, bounds=bounds, n_prev=len(prev)),
        grid=(n // tm, n_in // tn),
        in_specs=in_specs,
        out_specs=out_specs,
        out_shape=out_shape,
        scratch_shapes=[pltpu.VMEM((tm, d_model), BF16)],
        input_output_aliases=aliases,
        compiler_params=_params(("arbitrary", "arbitrary")),
        name="proj",
    )(x, norm_g.reshape(1, -1), w16, b_gate.reshape(1, -1), tile2(qn_a), tile2(kn_a),
      qn_c.reshape(1, -1), kn_c.reshape(1, -1), cos_tab, sin_tab, *prev)
    return outs[0], outs[3], outs[6], outs[9], [outs[k] for k in _PROJ_KV]


def _head_rows(ref, h, heads, start=0, size=None):
    size = ref.shape[0] // heads if size is None else size
    return ref[pl.ds(start * heads + h, size, stride=heads), :]


def _stack_maps(q):
    lane = lax.broadcasted_iota(jnp.int32, q.shape, 1)
    zero = jnp.zeros_like(q)
    return jnp.concatenate([jnp.where(lane < D_HALF, q, zero), jnp.where(lane >= D_HALF, q, zero)], axis=0)


def _diff_lambda(lq1_ref, lk1_ref, lq2_ref, lk2_ref, lam_init):
    a = jnp.exp(jnp.sum(lq1_ref[...] * lk1_ref[...], axis=1, keepdims=True))
    b = jnp.exp(jnp.sum(lq2_ref[...] * lk2_ref[...], axis=1, keepdims=True))
    return a - b + lam_init


def _diff_finish(l, acc, t, lam, subln, lam_init):
    o1 = acc[:t] / l[:t]
    o2 = acc[t:] / l[t:]
    o = o1 - lam * o2
    ms = jnp.mean(o * o, axis=-1, keepdims=True)
    return o * lax.rsqrt(ms + EPS) * subln * (1.0 - lam_init)


def _attn_a_prompt_kernel(q_ref, k_ref, v_ref, lq1_ref, lk1_ref, lq2_ref, lk2_ref, sg_ref, o_ref,
                          s_ref, m_ref, l_ref, acc_ref, *, tq, lam_init):
    qi = pl.program_id(2)
    qq = _stack_maps(q_ref[...])

    def lane_fold(op, x, acc):
        for c in range(x.shape[1] // LANES):
            acc = op(acc, x[:, c * LANES:(c + 1) * LANES])
        return acc

    def score_block(kb, masked):
        start = pl.multiple_of(kb * tq, tq)
        s = _dot_nt(qq, k_ref[pl.ds(start, tq), :].astype(BF16))
        if masked:
            q_chunk = (lax.broadcasted_iota(jnp.int32, (2 * tq, 1), 0) % tq) // CHUNK
            k_chunk = lax.broadcasted_iota(jnp.int32, (1, tq), 1) // CHUNK
            s = jnp.where(k_chunk <= q_chunk, s, -jnp.inf)
        s_ref[kb] = s
        m_ref[...] = lane_fold(jnp.maximum, s, m_ref[...])

    def score_body(kb, carry):
        score_block(kb, False)
        return carry

    m_ref[...] = jnp.full(m_ref.shape, -jnp.inf, F32)
    lax.fori_loop(0, qi, score_body, 0)
    score_block(qi, True)
    m = jnp.max(m_ref[...], axis=1, keepdims=True)

    def value_body(kb, carry):
        start = pl.multiple_of(kb * tq, tq)
        p = jnp.exp(s_ref[kb] - m)
        l_ref[...] = lane_fold(jnp.add, p, l_ref[...])
        acc_ref[...] += _dot(p.astype(BF16), v_ref[pl.ds(start, tq), :].astype(BF16))
        return carry

    l_ref[...] = jnp.zeros(l_ref.shape, F32)
    acc_ref[...] = jnp.zeros(acc_ref.shape, F32)
    lax.fori_loop(0, qi + 1, value_body, 0)
    l = jnp.sum(l_ref[...], axis=1, keepdims=True)
    lam = _diff_lambda(lq1_ref, lk1_ref, lq2_ref, lk2_ref, lam_init)
    o_ref[...] = _diff_finish(l, acc_ref[...], tq, lam, sg_ref[...], lam_init).astype(o_ref.dtype)


def _lam_specs():
    spec = pl.BlockSpec((1, D_HALF), lambda *_: (0, 0))
    return [spec, spec, spec, spec, pl.BlockSpec((1, HEAD_DIM), lambda *_: (0, 0))]


def _lam_args(lq1, lk1, lq2, lk2, subln):
    return (lq1.reshape(1, -1), lk1.reshape(1, -1), lq2.reshape(1, -1), lk2.reshape(1, -1),
            subln.reshape(1, -1))


def _attn_a_prompt(q16, k32, v32, lam_args, *, layer, batch, seq, heads):
    n = q16.shape[0]
    tq = ATTN_A_TQ
    assert seq % tq == 0 and tq % CHUNK == 0
    nq = seq // tq
    kv_spec = pl.BlockSpec((None, seq, HEAD_DIM), lambda b, h, qi: (layer, b, h))
    return pl.pallas_call(
        functools.partial(_attn_a_prompt_kernel, tq=tq, lam_init=_lambda_init(layer)),
        grid=(batch, heads, nq),
        in_specs=[pl.BlockSpec((tq, HEAD_DIM), lambda b, h, qi: (b * nq + qi, h)), kv_spec, kv_spec]
                 + _lam_specs(),
        out_specs=pl.BlockSpec((tq, HEAD_DIM), lambda b, h, qi: (b * nq + qi, h)),
        out_shape=jax.ShapeDtypeStruct((n, heads * HEAD_DIM), BF16),
        scratch_shapes=[pltpu.VMEM((nq, 2 * tq, tq), F32),
                        pltpu.VMEM((2 * tq, LANES), F32), pltpu.VMEM((2 * tq, LANES), F32),
                        pltpu.VMEM((2 * tq, HEAD_DIM), F32)],
        compiler_params=_params(("arbitrary", "arbitrary", "arbitrary")),
        name="attn_a_prompt",
    )(q16, k32, v32, *lam_args)


def _attn_a_sample_kernel(q_ref, ck_ref, cv_ref, kn_ref, vn_ref, lq1_ref, lk1_ref, lq2_ref, lk2_ref, sg_ref,
                          o_ref, qq_ref, s_ref, sn_ref, m_ref, mf_ref, l_ref, acc_ref, *,
                          heads, new_len, lam_init):
    phase = pl.program_id(1)
    kt = pl.program_id(2)
    nkt = pl.num_programs(2)

    def lane_fold(op, x, acc):
        for c in range(x.shape[1] // LANES):
            acc = op(acc, x[:, c * LANES:(c + 1) * LANES])
        return acc

    @pl.when(jnp.logical_and(phase == 0, kt == 0))
    def _():
        for h in range(heads):
            sl = slice(h * HEAD_DIM, (h + 1) * HEAD_DIM)
            qq = _stack_maps(q_ref[:, sl])
            qq_ref[h] = qq
            s_new = _dot_nt(qq, kn_ref[:, sl].astype(BF16))
            sn_ref[h] = s_new
            m_ref[h] = jnp.broadcast_to(jnp.max(s_new, axis=1, keepdims=True), m_ref.shape[1:])

    @pl.when(phase == 0)
    def _():
        for h in range(heads):
            s = _dot_nt(qq_ref[h], _head_rows(ck_ref, h, heads).astype(BF16))
            s_ref[h, kt] = s
            m_ref[h] = lane_fold(jnp.maximum, s, m_ref[h])

    @pl.when(jnp.logical_and(phase == 1, kt == 0))
    def _():
        lane = lax.broadcasted_iota(jnp.int32, l_ref.shape[1:], 1)
        for h in range(heads):
            sl = slice(h * HEAD_DIM, (h + 1) * HEAD_DIM)
            m = jnp.max(m_ref[h], axis=1, keepdims=True)
            mf_ref[h] = m
            p_new = jnp.exp(sn_ref[h] - m)
            l_ref[h] = jnp.where(lane == 0, jnp.sum(p_new, axis=1, keepdims=True), 0.0)
            acc_ref[h] = _dot(p_new.astype(BF16), vn_ref[:, sl].astype(BF16))

    @pl.when(phase == 1)
    def _():
        for h in range(heads):
            p = jnp.exp(s_ref[h, kt] - mf_ref[h])
            l_ref[h] = lane_fold(jnp.add, p, l_ref[h])
            acc_ref[h] += _dot(p.astype(BF16), _head_rows(cv_ref, h, heads).astype(BF16))

    @pl.when(jnp.logical_and(phase == 1, kt == nkt - 1))
    def _():
        lam = _diff_lambda(lq1_ref, lk1_ref, lq2_ref, lk2_ref, lam_init)
        for h in range(heads):
            l = jnp.sum(l_ref[h], axis=1, keepdims=True)
            y = _diff_finish(l, acc_ref[h], new_len, lam, sg_ref[...], lam_init)
            o_ref[:, h * HEAD_DIM:(h + 1) * HEAD_DIM] = y.astype(o_ref.dtype)


def _attn_a_sample(q16, cache_k, cache_v, k32, v32, lam_args, *, layer, streams, new_len, heads):
    past = cache_k.shape[2] // heads
    tk = SAMPLE_A_TK
    assert past % tk == 0
    nkt = past // tk
    width = heads * HEAD_DIM
    q_spec = pl.BlockSpec((new_len, width), lambda b, p, kt: (b, 0))
    new_spec = pl.BlockSpec((None, new_len, width), lambda b, p, kt: (layer, b, 0))
    k_spec = pl.BlockSpec((None, None, tk * heads, HEAD_DIM),
                          lambda b, p, kt: (layer, b, jnp.where(p == 0, kt, nkt - 1), 0))
    v_spec = pl.BlockSpec((None, None, tk * heads, HEAD_DIM),
                          lambda b, p, kt: (layer, b, jnp.where(p == 0, 0, kt), 0))
    rows = 2 * new_len
    return pl.pallas_call(
        functools.partial(_attn_a_sample_kernel, heads=heads, new_len=new_len,
                          lam_init=_lambda_init(layer)),
        grid=(streams, 2, nkt),
        in_specs=[q_spec, k_spec, v_spec, new_spec, new_spec] + _lam_specs(),
        out_specs=q_spec,
        out_shape=jax.ShapeDtypeStruct(q16.shape, BF16),
        scratch_shapes=[pltpu.VMEM((heads, rows, HEAD_DIM), BF16),
                        pltpu.VMEM((heads, nkt, rows, tk), F32),
                        pltpu.VMEM((heads, rows, new_len), F32),
                        pltpu.VMEM((heads, rows, LANES), F32),
                        pltpu.VMEM((heads, rows, 1), F32),
                        pltpu.VMEM((heads, rows, LANES), F32),
                        pltpu.VMEM((heads, rows, HEAD_DIM), F32)],
        compiler_params=_params(("arbitrary", "arbitrary", "arbitrary")),
        name="attn_a_sample",
    )(q16, cache_k, cache_v, k32, v32, *lam_args)


def _upper_ones(t):
    r = lax.broadcasted_iota(jnp.int32, (t, t), 0)
    c = lax.broadcasted_iota(jnp.int32, (t, t), 1)
    return (r > c).astype(BF16)


def _sb_block(q, kblk, vblk, ones_u, c, acc, scale, strict):
    z = _dot_nt(q, kblk) * scale
    sp = _softplus(z)
    log_stay = -sp
    if strict is not None:
        log_stay = jnp.where(strict, log_stay, 0.0)
    between = _dot_f32acc(log_stay, ones_u)
    w = jnp.exp((z - sp) + between + c)
    if strict is not None:
        w = jnp.where(strict, w, 0.0)
    acc = acc + _dot(w.astype(BF16), vblk)
    c = c + jnp.sum(log_stay, axis=1, keepdims=True)
    return c, acc


def _attn_b_prompt_kernel(q_ref, k_ref, v_ref, o_ref, c_ref, acc_ref, *, tq, tk, heads):
    h = pl.program_id(1)
    qi = pl.program_id(2)
    q = q_ref[...]
    scale = HEAD_DIM ** -0.5
    ones_u = _upper_ones(tk)
    per_q = tq // tk
    c_ref[...] = jnp.zeros(c_ref.shape, F32)
    acc_ref[...] = jnp.zeros(acc_ref.shape, F32)

    def block(kb, masked):
        start = pl.multiple_of(kb * tk, tk)
        strict = None
        if masked:
            row = lax.broadcasted_iota(jnp.int32, (tq, 1), 0)
            col = lax.broadcasted_iota(jnp.int32, (1, tk), 1)
            strict = start + col < qi * tq + row
        rows = pl.ds(start * heads + h, tk, stride=heads)
        c, acc = _sb_block(q, k_ref[rows, :].astype(BF16), v_ref[rows, :].astype(BF16),
                           ones_u, c_ref[...], acc_ref[...], scale, strict)
        c_ref[...] = c
        acc_ref[...] = acc

    for d in reversed(range(per_q)):
        block(qi * per_q + d, True)

    def body(t, carry):
        block(qi * per_q - 1 - t, False)
        return carry

    lax.fori_loop(0, qi * per_q, body, 0)
    o_ref[...] = acc_ref[...].astype(o_ref.dtype)


def _attn_b_prompt(q16, k32, v32, *, layer, batch, seq, heads):
    n = q16.shape[0]
    tq, tk = ATTN_B_TQ, SB_SUB
    assert seq % tq == 0 and tq % tk == 0
    nq = seq // tq
    kv_spec = pl.BlockSpec((None, seq * heads, HEAD_DIM), lambda b, h, qi: (layer, b, 0))
    return pl.pallas_call(
        functools.partial(_attn_b_prompt_kernel, tq=tq, tk=tk, heads=heads),
        grid=(batch, heads, nq),
        in_specs=[pl.BlockSpec((tq, HEAD_DIM), lambda b, h, qi: (b * nq + qi, h)), kv_spec, kv_spec],
        out_specs=pl.BlockSpec((tq, HEAD_DIM), lambda b, h, qi: (b * nq + qi, h)),
        out_shape=jax.ShapeDtypeStruct((n, heads * HEAD_DIM), BF16),
        scratch_shapes=[pltpu.VMEM((tq, 1), F32), pltpu.VMEM((tq, HEAD_DIM), F32)],
        compiler_params=_params(("arbitrary", "arbitrary", "arbitrary")),
        name="attn_b_prompt",
    )(q16, k32, v32)


def _attn_b_sample_kernel(q_ref, ck_ref, cv_ref, kn_ref, vn_ref, o_ref, u_ref, c_ref, acc_ref, *,
                          heads, new_len):
    b = pl.program_id(0)
    kt = pl.program_id(1)
    nkt = pl.num_programs(1)
    scale = HEAD_DIM ** -0.5
    tk = ck_ref.shape[0] // heads
    head_cols = lambda h: slice(h * HEAD_DIM, (h + 1) * HEAD_DIM)
    head_rows = lambda h: slice(h * new_len, (h + 1) * new_len)

    @pl.when(jnp.logical_and(b == 0, kt == 0))
    def _():
        u_ref[...] = _upper_ones(tk)

    @pl.when(kt == 0)
    def _():
        ones_new = _upper_ones(new_len)
        row = lax.broadcasted_iota(jnp.int32, (new_len, new_len), 0)
        col = lax.broadcasted_iota(jnp.int32, (new_len, new_len), 1)
        for h in range(heads):
            sl = head_cols(h)
            c, acc = _sb_block(q_ref[:, sl], _head_rows(kn_ref, h, heads).astype(BF16),
                               _head_rows(vn_ref, h, heads).astype(BF16), ones_new,
                               jnp.zeros((new_len, 1), F32), jnp.zeros((new_len, HEAD_DIM), F32),
                               scale, col < row)
            c_ref[head_rows(h), :] = c
            acc_ref[h] = acc

    z = jnp.concatenate([_dot_nt(q_ref[:, head_cols(h)], _head_rows(ck_ref, h, heads).astype(BF16))
                         for h in range(heads)], axis=0) * scale
    sp = _softplus(z)
    log_stay = -sp
    between = _dot_f32acc(log_stay, u_ref[...])
    w = jnp.exp((z - sp) + between + c_ref[...]).astype(BF16)
    for h in range(heads):
        acc_ref[h] += _dot(w[head_rows(h), :], _head_rows(cv_ref, h, heads).astype(BF16))
    c_ref[...] += jnp.sum(log_stay, axis=1, keepdims=True)

    @pl.when(kt == nkt - 1)
    def _():
        for h in range(heads):
            o_ref[:, head_cols(h)] = acc_ref[h].astype(o_ref.dtype)


def _attn_b_sample(q16, cache_k, cache_v, k32, v32, *, layer, streams, new_len, heads):
    past = cache_k.shape[2] // heads
    tk = SAMPLE_TK
    nkt = past // tk
    assert past % tk == 0
    width = heads * HEAD_DIM
    q_spec = pl.BlockSpec((new_len, width), lambda b, kt: (b, 0))
    new_spec = pl.BlockSpec((None, new_len * heads, HEAD_DIM), lambda b, kt: (layer, b, 0))
    cache_spec = pl.BlockSpec((None, None, tk * heads, HEAD_DIM), lambda b, kt: (layer, b, nkt - 1 - kt, 0))
    return pl.pallas_call(
        functools.partial(_attn_b_sample_kernel, heads=heads, new_len=new_len),
        grid=(streams, nkt),
        in_specs=[q_spec, cache_spec, cache_spec, new_spec, new_spec],
        out_specs=q_spec,
        out_shape=jax.ShapeDtypeStruct(q16.shape, BF16),
        scratch_shapes=[pltpu.VMEM((tk, tk), BF16),
                        pltpu.VMEM((heads * new_len, 1), F32),
                        pltpu.VMEM((heads, new_len, HEAD_DIM), F32)],
        compiler_params=_params(("arbitrary", "arbitrary")),
        name="attn_b_sample",
    )(q16, cache_k, cache_v, k32, v32)


def _attn_c_prompt_kernel(q_ref, k_ref, v_ref, bgrp_ref, o_ref, k16_ref, v16_ref, *, seq, heads):
    k16_ref[0:C_WIN, :] = jnp.zeros((C_WIN, HEAD_DIM), BF16)
    v16_ref[0:C_WIN, :] = jnp.zeros((C_WIN, HEAD_DIM), BF16)
    own = pl.ds(pl.program_id(1), seq, stride=heads)
    k16_ref[C_WIN:, :] = k_ref[own, :].astype(BF16)
    v16_ref[C_WIN:, :] = v_ref[own, :].astype(BF16)
    scale = HEAD_DIM ** -0.5
    key_chunk = lax.broadcasted_iota(jnp.int32, (1, BAND_KEYS), 1) // CHUNK

    def group(gi):
        r0 = pl.multiple_of(gi * BAND_ROWS, BAND_ROWS)
        s = _dot_nt(q_ref[pl.ds(r0, BAND_ROWS), :], k16_ref[pl.ds(r0, BAND_KEYS), :]) * scale + bgrp_ref[0]
        s = jnp.where(key_chunk + gi * BAND_GROUP >= C_BAND_CHUNKS, s, -jnp.inf)
        m = jnp.max(s, axis=1, keepdims=True)
        p = jnp.exp(s - m)
        l = jnp.sum(p, axis=1, keepdims=True)
        o = _dot(p.astype(BF16), v16_ref[pl.ds(r0, BAND_KEYS), :]) / l
        o_ref[pl.ds(r0, BAND_ROWS), :] = o.astype(o_ref.dtype)

    def body(t, carry):
        group(2 * t)
        group(2 * t + 1)
        return carry

    lax.fori_loop(0, seq // BAND_ROWS // 2, body, 0)


def _attn_c_prompt(q16, k32, v32, bias_grp, *, layer, batch, seq, heads):
    n = q16.shape[0]
    assert seq % (2 * BAND_ROWS) == 0
    kv_spec = pl.BlockSpec((None, seq * heads, HEAD_DIM), lambda b, h: (layer, b, 0))
    return pl.pallas_call(
        functools.partial(_attn_c_prompt_kernel, seq=seq, heads=heads),
        grid=(batch, heads),
        in_specs=[pl.BlockSpec((seq, HEAD_DIM), lambda b, h: (b, h)), kv_spec, kv_spec,
                  pl.BlockSpec((1, BAND_ROWS, BAND_KEYS), lambda b, h: (layer * heads + h, 0, 0))],
        out_specs=pl.BlockSpec((seq, HEAD_DIM), lambda b, h: (b, h)),
        out_shape=jax.ShapeDtypeStruct((n, heads * HEAD_DIM), BF16),
        scratch_shapes=[pltpu.VMEM((seq + C_WIN, HEAD_DIM), BF16),
                        pltpu.VMEM((seq + C_WIN, HEAD_DIM), BF16)],
        compiler_params=_params(("arbitrary", "arbitrary")),
        name="attn_c_prompt",
    )(q16, k32, v32, bias_grp)


def _attn_c_sample_kernel(q_ref, ck_ref, cv_ref, kn_ref, vn_ref, bfar_ref, bown_ref, o_ref, *, heads):
    scale = HEAD_DIM ** -0.5
    for h in range(heads):
        sl = slice(h * HEAD_DIM, (h + 1) * HEAD_DIM)
        q = q_ref[:, sl]
        s_far = _dot_nt(q, _head_rows(ck_ref, h, heads).astype(BF16)) * scale + bfar_ref[h]
        s_own = _dot_nt(q, _head_rows(kn_ref, h, heads).astype(BF16)) * scale + bown_ref[h]
        m = jnp.maximum(jnp.max(s_far, axis=1, keepdims=True), jnp.max(s_own, axis=1, keepdims=True))
        p_far = jnp.exp(s_far - m)
        p_own = jnp.exp(s_own - m)
        l = jnp.sum(p_far, axis=1, keepdims=True) + jnp.sum(p_own, axis=1, keepdims=True)
        o = (_dot(p_far.astype(BF16), _head_rows(cv_ref, h, heads).astype(BF16))
             + _dot(p_own.astype(BF16), _head_rows(vn_ref, h, heads).astype(BF16))) / l
        o_ref[:, sl] = o.astype(o_ref.dtype)


def _attn_c_sample(q16, cache_k, cache_v, k32, v32, bias_far, bias_own, *, layer, streams, new_len, heads):
    assert cache_k.shape[2] == C_WIN * heads and new_len == CHUNK
    width = heads * HEAD_DIM
    q_spec = pl.BlockSpec((new_len, width), lambda b: (b, 0))
    new_spec = pl.BlockSpec((None, new_len * heads, HEAD_DIM), lambda b: (layer, b, 0))
    cache_spec = pl.BlockSpec((None, None, C_WIN * heads, HEAD_DIM), lambda b: (layer, b, 0, 0))
    return pl.pallas_call(
        functools.partial(_attn_c_sample_kernel, heads=heads),
        grid=(streams,),
        in_specs=[q_spec, cache_spec, cache_spec, new_spec, new_spec,
                  pl.BlockSpec((heads, CHUNK, C_WIN), lambda b: (layer, 0, 0)),
                  pl.BlockSpec((heads, CHUNK, CHUNK), lambda b: (layer, 0, 0))],
        out_specs=q_spec,
        out_shape=jax.ShapeDtypeStruct(q16.shape, BF16),
        compiler_params=_params(("arbitrary",)),
        name="attn_c_sample",
    )(q16, cache_k, cache_v, k32, v32, bias_far, bias_own)


def _merge_kernel(x_ref, oa_ref, ob_ref, oc_ref, g_ref, wb_ref, wo_ref, o_ref, *, w_a, w_b, d_model):
    ya = _dot(oa_ref[...], wb_ref[0:w_a, :])
    yb = _dot(ob_ref[...], wb_ref[w_a:w_a + w_b, :])
    yc = _dot(oc_ref[...], wb_ref[w_a + w_b:, :])
    m = (g_ref[:, 0:d_model].astype(F32) * ya + g_ref[:, d_model:2 * d_model].astype(F32) * yb
         + g_ref[:, 2 * d_model:].astype(F32) * yc)
    o_ref[...] = x_ref[...] + _dot(m.astype(BF16), wo_ref[...])


def _merge(x, oa, ob, oc, g, wb16, wo16, *, layer):
    n, d_model = x.shape
    tm = MERGE_TM
    assert n % tm == 0
    w_a, w_b, w_c = oa.shape[1], ob.shape[1], oc.shape[1]
    rows = lambda width: pl.BlockSpec((tm, width), lambda i: (i, 0))
    resident = lambda shape: pl.BlockSpec((None,) + shape[1:], lambda i: (layer, 0, 0),
                                          pipeline_mode=pl.Buffered(1))
    return pl.pallas_call(
        functools.partial(_merge_kernel, w_a=w_a, w_b=w_b, d_model=d_model),
        grid=(n // tm,),
        in_specs=[rows(d_model), rows(w_a), rows(w_b), rows(w_c), rows(N_BRANCH * d_model),
                  resident(wb16.shape), resident(wo16.shape)],
        out_specs=rows(d_model),
        out_shape=jax.ShapeDtypeStruct((n, d_model), F32),
        compiler_params=_params(("arbitrary",)),
        name="merge",
    )(x, oa, ob, oc, g, wb16, wo16)


_R_E1, _R_E2, _R_C1, _R_C2, _R_RANK1, _R_RANK2 = range(6)


def _route_kernel(xp_ref, xs_ref, ng_ref, wr_ref, br_ref, xn_ref, route_ref, cnt_ref, carry_ref, *,
                  prompt_tiles):
    i = pl.program_id(0)
    tm = xp_ref.shape[0]
    lane = lax.broadcasted_iota(jnp.int32, (tm, LANES), 1)
    lane_f = lane.astype(F32)

    @pl.when(i == 0)
    def _():
        carry_ref[...] = jnp.zeros(carry_ref.shape, F32)

    x = jnp.where(i < prompt_tiles, xp_ref[...], xs_ref[...])
    ms = jnp.mean(x * x, axis=-1, keepdims=True)
    xn = x * lax.rsqrt(ms + EPS) * ng_ref[...]
    xn_ref[...] = xn

    x_hi, x_lo = _split_bf16(xn)
    w_hi, w_lo = _split_bf16(wr_ref[...])
    logits = _dot(x_hi, w_hi) + _dot(x_lo, w_hi) + _dot(x_hi, w_lo) + br_ref[...]
    is_group = lane < N_GROUPS
    gl = jnp.where(is_group, logits, -jnp.inf)
    g_max = jnp.max(gl, axis=1, keepdims=True)
    g_idx = jnp.min(jnp.where(gl == g_max, lane_f, float(LANES)), axis=1, keepdims=True)
    p_group = 1.0 / jnp.sum(jnp.where(is_group, jnp.exp(logits - g_max), 0.0), axis=1, keepdims=True)
    in_group = jnp.logical_and(lane >= N_GROUPS, lane < N_GROUPS + N_EXP)
    in_group = jnp.logical_and(in_group, ((lane - N_GROUPS) // E_PER_GROUP).astype(F32) == g_idx)
    el = jnp.where(in_group, logits, -jnp.inf)
    v1 = jnp.max(el, axis=1, keepdims=True)
    i1 = jnp.min(jnp.where(el == v1, lane_f, float(LANES)), axis=1, keepdims=True)
    el2 = jnp.where(lane_f == i1, -jnp.inf, el)
    v2 = jnp.max(el2, axis=1, keepdims=True)
    i2 = jnp.min(jnp.where(el2 == v2, lane_f, float(LANES)), axis=1, keepdims=True)
    t = jnp.exp(v2 - v1)
    c1 = p_group / (1.0 + t)
    c2 = p_group * t / (1.0 + t)
    e1 = i1 - N_GROUPS
    e2 = i2 - N_GROUPS

    a1 = (lane_f == e1).astype(F32)
    a2 = (lane_f == e2).astype(F32)
    a = a1 + a2
    row = lax.broadcasted_iota(jnp.int32, (tm, tm), 0)
    col = lax.broadcasted_iota(jnp.int32, (tm, tm), 1)
    earlier = (col < row).astype(BF16)
    before = _dot(earlier, a.astype(BF16)) + carry_ref[...]
    r1 = jnp.sum(a1 * before, axis=1, keepdims=True)
    r2 = jnp.sum(a2 * before, axis=1, keepdims=True)
    carry_ref[...] += jnp.sum(a, axis=0, keepdims=True)
    cnt_ref[...] = carry_ref[...]

    record = jnp.zeros((tm, LANES), F32)
    for slot, val in ((_R_E1, e1), (_R_E2, e2), (_R_C1, c1), (_R_C2, c2), (_R_RANK1, r1), (_R_RANK2, r2)):
        record = jnp.where(lane == slot, val, record)
    route_ref[...] = record


def _route(xp, xs, norm_g, w_router, b_router):
    tm = MOE_ROUTE_TM
    (n_p, d_model), n_s = xp.shape, xs.shape[0]
    assert n_p % tm == 0 and n_s % tm == 0
    pt, st = n_p // tm, n_s // tm
    n = n_p + n_s
    const = lambda shape: pl.BlockSpec(shape, lambda i: (0, 0))
    return pl.pallas_call(
        functools.partial(_route_kernel, prompt_tiles=pt),
        grid=(pt + st,),
        in_specs=[pl.BlockSpec((tm, d_model), lambda i: (jnp.minimum(i, pt - 1), 0)),
                  pl.BlockSpec((tm, d_model), lambda i: (jnp.maximum(i - pt, 0), 0)),
                  const((1, d_model)), const((d_model, LANES)), const((1, LANES))],
        out_specs=[pl.BlockSpec((tm, d_model), lambda i: (i, 0)),
                   pl.BlockSpec((tm, LANES), lambda i: (i, 0)),
                   const((1, LANES))],
        out_shape=[jax.ShapeDtypeStruct((n, d_model), F32), jax.ShapeDtypeStruct((n, LANES), F32),
                   jax.ShapeDtypeStruct((1, LANES), F32)],
        scratch_shapes=[pltpu.VMEM((1, LANES), F32)],
        compiler_params=_params(("arbitrary",)),
        name="moe_route",
    )(xp, xs, norm_g.reshape(1, -1), w_router, b_router)


def _row_copy(src, src_row, dst, dst_row, sem):
    return pltpu.make_async_copy(src.at[pl.ds(src_row, 1), :], dst.at[pl.ds(dst_row, 1), :], sem)


def _row_token_kernel(pos1_ref, pos2_ref, out_ref, *, n_tokens):
    def init(r, carry):
        out_ref[r] = 0
        return carry

    def place(tok, carry):
        out_ref[pos1_ref[tok]] = tok
        out_ref[pos2_ref[tok]] = tok
        return carry

    lax.fori_loop(0, out_ref.shape[0], init, 0, unroll=8)
    lax.fori_loop(0, n_tokens, place, 0, unroll=8)


def _row_tokens(pos1, pos2, rows):
    return pl.pallas_call(
        functools.partial(_row_token_kernel, n_tokens=pos1.shape[0]),
        grid_spec=pltpu.PrefetchScalarGridSpec(
            num_scalar_prefetch=2, grid=(1,), in_specs=[],
            out_specs=pl.BlockSpec(memory_space=pltpu.SMEM)),
        out_shape=jax.ShapeDtypeStruct((rows,), jnp.int32),
        compiler_params=_params(("arbitrary",)),
        name="moe_row_tokens",
    )(pos1, pos2)


def _expert_kernel(tile_expert_ref, n_tiles_ref, row_token_ref, xn_hbm, wg_ref, wu_ref, wd_ref, y_ref,
                   xbuf, sem, *, tile):
    del tile_expert_ref
    t = pl.program_id(0)
    n_valid = n_tiles_ref[0]
    slot = t % 2

    def gather(tile_idx, into):
        base = tile_idx * tile

        def body(g, carry):
            r0 = pl.multiple_of(g * ROW_GROUP, ROW_GROUP)
            for k in range(ROW_GROUP):
                _row_copy(xn_hbm, row_token_ref[base + r0 + k], xbuf.at[into], r0 + k,
                          sem.at[into]).start(priority=k % 2)
            return carry

        lax.fori_loop(0, tile // ROW_GROUP, body, 0)

    def wait_rows(into):
        def body(r, carry):
            _row_copy(xn_hbm, 0, xbuf.at[into], 0, sem.at[into]).wait()
            return carry

        lax.fori_loop(0, tile, body, 0, unroll=8)

    @pl.when(jnp.logical_and(t == 0, n_valid > 0))
    def _():
        gather(0, 0)

    @pl.when(t + 1 < n_valid)
    def _():
        gather(t + 1, 1 - slot)

    @pl.when(t < n_valid)
    def _():
        wait_rows(slot)
        x = xbuf[slot].astype(BF16)
        h = jax.nn.silu(_dot(x, wg_ref[...].astype(BF16))) * _dot(x, wu_ref[...].astype(BF16))
        y_ref[...] = _dot(h.astype(BF16), wd_ref[...].astype(BF16))

    @pl.when(t >= n_valid)
    def _():
        y_ref[...] = jnp.zeros(y_ref.shape, y_ref.dtype)


def _experts(tile_expert, n_tiles, row_token, xn, w_gate, w_up, w_down, *, layer):
    tile = MOE_ROW_TILE
    rows, d_model = row_token.shape[0], xn.shape[1]
    d_exp = w_gate.shape[-1]
    w_in = pl.BlockSpec((None, None, d_model, d_exp), lambda t, te, nt, rt: (layer, te[t], 0, 0))
    w_out = pl.BlockSpec((None, None, d_exp, d_model), lambda t, te, nt, rt: (layer, te[t], 0, 0))
    return pl.pallas_call(
        functools.partial(_expert_kernel, tile=tile),
        grid_spec=pltpu.PrefetchScalarGridSpec(
            num_scalar_prefetch=3, grid=(rows // tile,),
            in_specs=[pl.BlockSpec(memory_space=pl.ANY), w_in, w_in, w_out],
            out_specs=pl.BlockSpec((tile, d_model), lambda t, te, nt, rt: (t, 0)),
            scratch_shapes=[pltpu.VMEM((2, tile, d_model), F32), pltpu.SemaphoreType.DMA((2,))]),
        out_shape=jax.ShapeDtypeStruct((rows, d_model), F32),
        compiler_params=_params(("arbitrary",)),
        name="moe_experts",
    )(tile_expert, n_tiles, row_token, xn, w_gate, w_up, w_down)


def _combine_kernel(pos1_ref, pos2_ref, xp_ref, xs_ref, route_ref, y_hbm, op_ref, os_ref, ybuf, sem, *,
                    tile, prompt_tiles):
    i = pl.program_id(0)
    base = i * tile

    for r in range(tile):
        _row_copy(y_hbm, pos1_ref[base + r], ybuf.at[0], r, sem).start(priority=0)
        _row_copy(y_hbm, pos2_ref[base + r], ybuf.at[1], r, sem).start(priority=1)

    def wait(r, carry):
        _row_copy(y_hbm, 0, ybuf.at[0], 0, sem).wait()
        _row_copy(y_hbm, 0, ybuf.at[1], 0, sem).wait()
        return carry

    lax.fori_loop(0, tile, wait, 0, unroll=8)
    y = route_ref[:, _R_C1:_R_C1 + 1] * ybuf[0] + route_ref[:, _R_C2:_R_C2 + 1] * ybuf[1]

    @pl.when(i < prompt_tiles)
    def _():
        op_ref[...] = xp_ref[...] + y

    @pl.when(i >= prompt_tiles)
    def _():
        os_ref[...] = xs_ref[...] + y


def _combine(pos1, pos2, xp, xs, route, y_sorted):
    tile = MOE_TOK_TILE
    (n_p, d_model), n_s = xp.shape, xs.shape[0]
    assert n_p % tile == 0 and n_s % tile == 0
    pt, st = n_p // tile, n_s // tile
    p_spec = pl.BlockSpec((tile, d_model), lambda i, p1, p2: (jnp.minimum(i, pt - 1), 0))
    s_spec = pl.BlockSpec((tile, d_model), lambda i, p1, p2: (jnp.maximum(i - pt, 0), 0))
    return pl.pallas_call(
        functools.partial(_combine_kernel, tile=tile, prompt_tiles=pt),
        grid_spec=pltpu.PrefetchScalarGridSpec(
            num_scalar_prefetch=2, grid=(pt + st,),
            in_specs=[p_spec, s_spec, pl.BlockSpec((tile, LANES), lambda i, p1, p2: (i, 0)),
                      pl.BlockSpec(memory_space=pl.ANY)],
            out_specs=[p_spec, s_spec],
            scratch_shapes=[pltpu.VMEM((2, tile, d_model), F32), pltpu.SemaphoreType.DMA(())]),
        out_shape=[jax.ShapeDtypeStruct(xp.shape, F32), jax.ShapeDtypeStruct(xs.shape, F32)],
        compiler_params=_params(("arbitrary",)),
        name="moe_combine",
    )(pos1, pos2, xp, xs, route, y_sorted)


def _moe(xp, xs, norm_g, w_router, b_router, w_gate, w_up, w_down, *, layer):
    n = xp.shape[0] + xs.shape[0]
    tile = MOE_ROW_TILE
    n_exp = w_gate.shape[1]
    row_tiles = (2 * n) // tile + n_exp
    xn, route, counts = _route(xp, xs, norm_g, w_router, b_router)

    counts = counts[0, :n_exp].astype(jnp.int32)
    padded = ((counts + tile - 1) // tile) * tile
    upto = jnp.arange(n_exp)[None, :] <= jnp.arange(n_exp)[:, None]
    ends = jnp.sum(jnp.where(upto, padded[None, :], 0), axis=1)
    offsets = ends - padded
    as_int = lambda lane_idx: route[:, lane_idx].astype(jnp.int32)
    pos1 = offsets[as_int(_R_E1)] + as_int(_R_RANK1)
    pos2 = offsets[as_int(_R_E2)] + as_int(_R_RANK2)
    tile_start = jnp.arange(row_tiles, dtype=jnp.int32) * tile
    tile_expert = jnp.minimum(jnp.sum((tile_start[:, None] >= ends[None, :]).astype(jnp.int32), axis=1),
                              n_exp - 1)
    n_tiles = ends[-1:] // tile

    row_token = _row_tokens(pos1, pos2, row_tiles * tile)
    y_sorted = _experts(tile_expert, n_tiles, row_token, xn, w_gate, w_up, w_down, layer=layer)
    return _combine(pos1, pos2, xp, xs, route, y_sorted)


def _rope_tables(pos):
    half = D_HALF // 2
    inv = ROPE_THETA ** (-2.0 * jnp.arange(half, dtype=F32) / D_HALF)
    ang = pos.astype(F32)[:, None] * inv[None, :]
    cos, sin = jnp.cos(ang), jnp.sin(ang)
    cos_full = jnp.tile(cos, (1, HEAD_DIM // half))
    sin_signed = jnp.tile(jnp.concatenate([-sin, sin], axis=1), (1, HEAD_DIM // D_HALF))
    return cos_full, sin_signed


def kernel(x_prompt, x_sample, cache_a_k, cache_a_v, cache_b_k, cache_b_v, cache_c_k, cache_c_v, norm_mix, w_in, b_gate, q_norm_a, k_norm_a, lam_q1, lam_k1, lam_q2, lam_k2, subln_a, q_norm_c, k_norm_c, rel_bias_c, w_branch, w_out, norm_ffn, w_group, b_group, w_expert_router, b_expert_router, w_gate_e, w_up_e, w_down_e):
    batch, seq, d_model = x_prompt.shape
    streams, new_len, _ = x_sample.shape
    depth, _, past, h_a, _ = cache_a_k.shape
    h_b, h_c = cache_b_k.shape[3], cache_c_k.shape[3]
    w_a, w_b, w_c = h_a * HEAD_DIM, h_b * HEAD_DIM, h_c * HEAD_DIM
    n_prompt, n_sample = batch * seq, streams * new_len
    tm = PROJ_TM
    assert seq % tm == 0 and n_sample % tm == 0 and tm % new_len == 0
    c_rows = min(C_WIN, seq)

    xp = x_prompt.reshape(n_prompt, d_model)
    xs = x_sample.reshape(n_sample, d_model)
    tab_p = _rope_tables(jnp.arange(seq))
    tab_s = _rope_tables(jnp.tile(past + jnp.arange(new_len), tm // new_len))
    bias_far, bias_own, bias_grp = _expand_bias(rel_bias_c.reshape(depth * h_c, N_REL))
    rows_view = lambda c: c.reshape(c.shape[0], c.shape[1], c.shape[2] * c.shape[3], c.shape[4])
    ca_k, ca_v, cb_k, cb_v, cc_k, cc_v = map(rows_view, (cache_a_k, cache_a_v, cache_b_k, cache_b_v,
                                                          cache_c_k, cache_c_v))

    w_in16, wb16, wo16 = w_in.astype(BF16), w_branch.astype(BF16), w_out.astype(BF16)
    kv_p = kv_s = None
    for l in range(depth):
        proj_w = (norm_mix[l], w_in16, b_gate[l], q_norm_a[l], k_norm_a[l], q_norm_c[l],
                  k_norm_c[l])
        proj_kw = dict(layer=l, depth=depth, w_a=w_a, w_b=w_b, w_c=w_c, d_model=d_model)
        qa_p, qb_p, qc_p, g_p, kv_p = _proj(xp, *proj_w, *tab_p, kv_p, **proj_kw)
        qa_s, qb_s, qc_s, g_s, kv_s = _proj(xs, *proj_w, *tab_s, kv_s, **proj_kw)
        lam_args = _lam_args(lam_q1[l], lam_k1[l], lam_q2[l], lam_k2[l], subln_a[l])

        oa_p = _attn_a_prompt(qa_p, kv_p[0], kv_p[1], lam_args, layer=l, batch=batch, seq=seq, heads=h_a)
        ob_p = _attn_b_prompt(qb_p, kv_p[2], kv_p[3], layer=l, batch=batch, seq=seq, heads=h_b)
        oc_p = _attn_c_prompt(qc_p, kv_p[4], kv_p[5], bias_grp, layer=l, batch=batch, seq=seq, heads=h_c)
        sample_kw = dict(layer=l, streams=streams, new_len=new_len)
        oa_s = _attn_a_sample(qa_s, ca_k, ca_v, kv_s[0], kv_s[1], lam_args, heads=h_a, **sample_kw)
        ob_s = _attn_b_sample(qb_s, cb_k, cb_v, kv_s[2], kv_s[3], heads=h_b, **sample_kw)
        oc_s = _attn_c_sample(qc_s, cc_k, cc_v, kv_s[4], kv_s[5], bias_far, bias_own, heads=h_c, **sample_kw)

        xp = _merge(xp, oa_p, ob_p, oc_p, g_p, wb16, wo16, layer=l)
        xs = _merge(xs, oa_s, ob_s, oc_s, g_s, wb16, wo16, layer=l)

        w_router = jnp.concatenate(
            [w_group[l], jnp.transpose(w_expert_router[l], (1, 0, 2)).reshape(d_model, N_EXP)], axis=1)
        w_router = jnp.pad(w_router, ((0, 0), (0, LANES - w_router.shape[1])))
        b_router = jnp.pad(jnp.concatenate([b_group[l], b_expert_router[l].reshape(-1)]),
                           (0, LANES - N_GROUPS - N_EXP)).reshape(1, LANES)
        xp, xs = _moe(xp, xs, norm_ffn[l], w_router, b_router, w_gate_e, w_up_e, w_down_e, layer=l)

    heads_of = [h_a, h_a, h_b, h_b, h_c, h_c]
    prompt_kv = [t.reshape(depth, batch, seq, h, HEAD_DIM) for t, h in zip(kv_p, heads_of)]
    prompt_kv[4] = prompt_kv[4][:, :, seq - c_rows:]
    prompt_kv[5] = prompt_kv[5][:, :, seq - c_rows:]
    sample_kv = [t.reshape(depth, streams, new_len, h, HEAD_DIM) for t, h in zip(kv_s, heads_of)]
    return (xp.reshape(batch, seq, d_model), xs.reshape(streams, new_len, d_model), *prompt_kv, *sample_kv)
```

```python
import functools
import math

import jax
import jax.numpy as jnp
from jax import lax
from jax.experimental import pallas as pl
from jax.experimental.pallas import tpu as pltpu

F32 = jnp.float32
BF16 = jnp.bfloat16

CHUNK = 64
HEAD_DIM = 128
D_HALF = HEAD_DIM // 2
C_BAND_CHUNKS = 8
C_WIN = C_BAND_CHUNKS * CHUNK
MAX_REL = 128
N_REL = 2 * MAX_REL + 1
ROPE_THETA = 10000.0
N_GROUPS = 4
E_PER_GROUP = 4
N_EXP = N_GROUPS * E_PER_GROUP
EPS = 1e-6
N_BRANCH = 3

LANES = 128
VMEM_LIMIT = 56 * 1024 * 1024

PROJ_TM = 1024
PROJ_TN = 512
ATTN_A_TQ = 512
ATTN_B_TQ = 512
SB_SUB = 256
SAMPLE_TK = 512
SAMPLE_A_TK = 2048
BAND_GROUP = 4
BAND_ROWS = BAND_GROUP * CHUNK
BAND_KEYS = (BAND_GROUP + C_BAND_CHUNKS) * CHUNK
MERGE_TM = 256
MOE_ROUTE_TM = 512
MOE_ROW_TILE = 256
MOE_TOK_TILE = 256
ROW_GROUP = 8


def _lambda_init(layer_idx):
    return 0.8 - 0.6 * math.exp(-0.3 * layer_idx)


def _dot(a, b):
    return jnp.dot(a, b, preferred_element_type=F32)


def _dot_nt(a, b):
    return lax.dot_general(a, b, (((1,), (1,)), ((), ())), preferred_element_type=F32)


def _split_bf16(x):
    hi = x.astype(BF16)
    lo = (x - hi.astype(F32)).astype(BF16)
    return hi, lo


def _dot_f32acc(x, w_bf16):
    hi, lo = _split_bf16(x)
    return _dot(hi, w_bf16) + _dot(lo, w_bf16)


def _softplus(z):
    return jnp.maximum(z, 0.0) + jnp.log(1.0 + jnp.exp(-jnp.abs(z)))


def _params(sem, vmem=VMEM_LIMIT):
    return pltpu.CompilerParams(dimension_semantics=sem, vmem_limit_bytes=vmem)


def _bias_kernel(tab_ref, far_ref, own_ref, grp_ref):
    r = pl.program_id(0)
    half = BAND_KEYS - 2 * CHUNK

    def rel_index(shape, key_shift):
        qi = lax.broadcasted_iota(jnp.int32, shape, 0)
        kj = lax.broadcasted_iota(jnp.int32, shape, 1) - key_shift
        return jnp.clip(qi + C_WIN - kj, -MAX_REL, MAX_REL) + MAX_REL, kj

    idx_even, kj_even = rel_index((CHUNK, half), 0)
    idx_odd, kj_odd = rel_index((CHUNK, half), CHUNK)

    def body(t, carry):
        val = tab_ref[r, t]
        return tuple(jnp.where(idx == t, val, acc) for idx, acc in zip((idx_even, idx_odd), carry))

    zeros = jnp.zeros((CHUNK, half), F32)
    even, odd = lax.fori_loop(MAX_REL - CHUNK + 1, N_REL, body, (zeros, zeros))
    far_ref[0] = even[:, :C_WIN]
    own_ref[0] = even[:, C_WIN:C_WIN + CHUNK]
    band = lambda kj: jnp.logical_and(kj >= 0, kj < C_WIN + CHUNK)
    even = jnp.where(band(kj_even), even, -jnp.inf)
    odd = jnp.where(band(kj_odd), odd, -jnp.inf)
    for a in range(BAND_GROUP):
        lead = (a // 2) * 2 * CHUNK
        tail = BAND_KEYS - lead - half
        pieces = [even if a % 2 == 0 else odd]
        if lead:
            pieces.insert(0, jnp.full((CHUNK, lead), -jnp.inf, F32))
        if tail:
            pieces.append(jnp.full((CHUNK, tail), -jnp.inf, F32))
        grp_ref[0, a * CHUNK:(a + 1) * CHUNK, :] = jnp.concatenate(pieces, axis=1)


def _expand_bias(table):
    rows = table.shape[0]
    shapes = [(CHUNK, C_WIN), (CHUNK, CHUNK), (BAND_ROWS, BAND_KEYS)]
    return pl.pallas_call(
        _bias_kernel,
        grid=(rows,),
        in_specs=[pl.BlockSpec(memory_space=pltpu.SMEM)],
        out_specs=[pl.BlockSpec((1,) + s, lambda r: (r, 0, 0)) for s in shapes],
        out_shape=[jax.ShapeDtypeStruct((rows,) + s, F32) for s in shapes],
        compiler_params=_params(("arbitrary",)),
        name="bias_expand",
    )(table)


_PROJ_NAMES = ["qa", "ka", "va", "qb", "kb", "vb", "qc", "kc", "vc", "g"]
_PROJ_KV = [1, 2, 4, 5, 7, 8]
_PROJ_HEAD_ROWS = [4, 5, 7, 8]


def _proj_kernel(*refs, bounds, n_prev):
    (x_ref, ng_ref, w_ref, bg_ref, qna_ref, kna_ref, qnc_ref, knc_ref, cos_ref, sin_ref) = refs[:10]
    (qa_ref, ka_ref, va_ref, qb_ref, kb_ref, vb_ref, qc_ref, kc_ref, vc_ref, g_ref,
     xn_ref) = refs[10 + n_prev:]
    j = pl.program_id(1)

    @pl.when(j == 0)
    def _():
        x = x_ref[...]
        ms = jnp.mean(x * x, axis=-1, keepdims=True)
        xn_ref[...] = (x * lax.rsqrt(ms + EPS) * ng_ref[...]).astype(BF16)

    z = _dot(xn_ref[...], w_ref[...])
    tm, tn = z.shape
    heads = tn // HEAD_DIM

    r_i = lax.broadcasted_iota(jnp.int32, (HEAD_DIM, HEAD_DIM), 0)
    c_i = lax.broadcasted_iota(jnp.int32, (HEAD_DIM, HEAD_DIM), 1)
    ones_map = ((r_i // D_HALF) == (c_i // D_HALF)).astype(BF16)
    ones_head = jnp.ones((HEAD_DIM, HEAD_DIM), BF16)
    lane = lax.broadcasted_iota(jnp.int32, (tm, HEAD_DIM), 1)
    first_half = (lane % D_HALF) < (D_HALF // 2)

    def map_norm_rope(zh, gain):
        ms = _dot_f32acc(zh * zh, ones_map) * (1.0 / D_HALF)
        y = zh * lax.rsqrt(ms + EPS) * gain
        partner = jnp.where(first_half,
                            pltpu.roll(y, HEAD_DIM - D_HALF // 2, 1),
                            pltpu.roll(y, D_HALF // 2, 1))
        return y * cos_ref[...] + partner * sin_ref[...]

    def head_norm(zh, gain):
        ms = _dot_f32acc(zh * zh, ones_head) * (1.0 / HEAD_DIM)
        return zh * lax.rsqrt(ms + EPS) * gain

    def per_head(fn, out_ref, scale=None, head_rows=False):
        for hh in range(heads):
            sl = slice(hh * HEAD_DIM, (hh + 1) * HEAD_DIM)
            y = fn(z[:, sl])
            if scale is not None:
                y = y * scale
            if head_rows:
                out_ref[pl.ds(hh, tm, stride=heads), :] = y.astype(out_ref.dtype)
            else:
                out_ref[:, sl] = y.astype(out_ref.dtype)

    same = lambda zh: zh

    def in_range(name):
        lo, hi = bounds[name]
        return jnp.logical_and(j >= lo, j < hi)

    @pl.when(in_range("qa"))
    def _():
        per_head(lambda zh: map_norm_rope(zh, qna_ref[...]), qa_ref, scale=D_HALF ** -0.5)

    @pl.when(in_range("ka"))
    def _():
        per_head(lambda zh: map_norm_rope(zh, kna_ref[...]), ka_ref)

    @pl.when(in_range("va"))
    def _():
        va_ref[...] = z

    @pl.when(in_range("qb"))
    def _():
        qb_ref[...] = z.astype(BF16)

    @pl.when(in_range("kb"))
    def _():
        per_head(same, kb_ref, head_rows=True)

    @pl.when(in_range("vb"))
    def _():
        per_head(same, vb_ref, head_rows=True)

    @pl.when(in_range("qc"))
    def _():
        per_head(lambda zh: head_norm(zh, qnc_ref[...]), qc_ref)

    @pl.when(in_range("kc"))
    def _():
        per_head(lambda zh: head_norm(zh, knc_ref[...]), kc_ref, head_rows=True)

    @pl.when(in_range("vc"))
    def _():
        per_head(same, vc_ref, head_rows=True)

    @pl.when(in_range("g"))
    def _():
        g_ref[...] = (0.5 * jnp.tanh(0.5 * (z + bg_ref[...])) + 0.5).astype(BF16)


def _proj(x, norm_g, w16, b_gate, qn_a, kn_a, qn_c, kn_c, cos_tab, sin_tab, prev_kv, *, layer, depth,
          w_a, w_b, w_c, d_model):
    n = x.shape[0]
    tm, tn = PROJ_TM, PROJ_TN
    n_in = w16.shape[2]
    assert n % tm == 0 and n_in % tn == 0 and w_a % tn == 0 and w_b == tn and w_c == tn
    assert cos_tab.shape[0] % tm == 0
    tab_tiles = cos_tab.shape[0] // tm
    na = w_a // tn
    widths = [na, na, na, 1, 1, 1, 1, 1, 1, N_BRANCH * d_model // tn]
    bounds, start = {}, 0
    for name, wd in zip(_PROJ_NAMES, widths):
        bounds[name] = (start, start + wd)
        start += wd
    assert start == n_in // tn

    def col_block(name):
        lo, hi = bounds[name]
        return lambda j: jnp.clip(j - lo, 0, hi - lo - 1)

    tab_map = lambda i, j: (i % tab_tiles, 0)
    g_col = col_block("g")
    vec = lambda width: pl.BlockSpec((1, width), lambda i, j: (0, 0))
    in_specs = [
        pl.BlockSpec((tm, d_model), lambda i, j: (i, 0), pipeline_mode=pl.Buffered(1)),
        vec(d_model),
        pl.BlockSpec((None, d_model, tn), lambda i, j: (layer, 0, j)),
        pl.BlockSpec((1, tn), lambda i, j: (0, g_col(j))),
        vec(HEAD_DIM), vec(HEAD_DIM), vec(HEAD_DIM), vec(HEAD_DIM),
        pl.BlockSpec((tm, HEAD_DIM), tab_map),
        pl.BlockSpec((tm, HEAD_DIM), tab_map),
    ]
    out_widths = [w_a, w_a, w_a, w_b, w_b, w_b, w_c, w_c, w_c, N_BRANCH * d_model]
    out_specs, out_shape = [], []
    for idx, (name, wd) in enumerate(zip(_PROJ_NAMES, out_widths)):
        cb = col_block(name)
        if idx in _PROJ_HEAD_ROWS:
            assert wd == tn
            out_specs.append(pl.BlockSpec((None, tm * (tn // HEAD_DIM), HEAD_DIM), lambda i, j: (layer, i, 0)))
            out_shape.append(jax.ShapeDtypeStruct((depth, n * (tn // HEAD_DIM), HEAD_DIM), F32))
        elif idx in _PROJ_KV:
            out_specs.append(pl.BlockSpec((None, tm, tn), lambda i, j, cb=cb: (layer, i, cb(j))))
            out_shape.append(jax.ShapeDtypeStruct((depth, n, wd), F32))
        else:
            out_specs.append(pl.BlockSpec((tm, tn), lambda i, j, cb=cb: (i, cb(j))))
            out_shape.append(jax.ShapeDtypeStruct((n, wd), BF16))
    prev = [] if prev_kv is None else list(prev_kv)
    aliases = {len(in_specs) + k: _PROJ_KV[k] for k in range(len(prev))}
    in_specs = in_specs + [pl.BlockSpec(memory_space=pl.ANY)] * len(prev)
    tile2 = lambda v: jnp.concatenate([v, v]).reshape(1, HEAD_DIM)
    outs = pl.pallas_call(
        functools.partial(_proj_kernel, bounds=bounds, n_prev=len(prev)),
        grid=(n // tm, n_in // tn),
        in_specs=in_specs,
        out_specs=out_specs,
        out_shape=out_shape,
        scratch_shapes=[pltpu.VMEM((tm, d_model), BF16)],
        input_output_aliases=aliases,
        compiler_params=_params(("arbitrary", "arbitrary")),
        name="proj",
    )(x, norm_g.reshape(1, -1), w16, b_gate.reshape(1, -1), tile2(qn_a), tile2(kn_a),
      qn_c.reshape(1, -1), kn_c.reshape(1, -1), cos_tab, sin_tab, *prev)
    return outs[0], outs[3], outs[6], outs[9], [outs[k] for k in _PROJ_KV]


def _head_rows(ref, h, heads, start=0, size=None):
    size = ref.shape[0] // heads if size is None else size
    return ref[pl.ds(start * heads + h, size, stride=heads), :]


def _stack_maps(q):
    lane = lax.broadcasted_iota(jnp.int32, q.shape, 1)
    zero = jnp.zeros_like(q)
    return jnp.concatenate([jnp.where(lane < D_HALF, q, zero), jnp.where(lane >= D_HALF, q, zero)], axis=0)


def _diff_lambda(lq1_ref, lk1_ref, lq2_ref, lk2_ref, lam_init):
    a = jnp.exp(jnp.sum(lq1_ref[...] * lk1_ref[...], axis=1, keepdims=True))
    b = jnp.exp(jnp.sum(lq2_ref[...] * lk2_ref[...], axis=1, keepdims=True))
    return a - b + lam_init


def _diff_finish(l, acc, t, lam, subln, lam_init):
    o1 = acc[:t] / l[:t]
    o2 = acc[t:] / l[t:]
    o = o1 - lam * o2
    ms = jnp.mean(o * o, axis=-1, keepdims=True)
    return o * lax.rsqrt(ms + EPS) * subln * (1.0 - lam_init)


def _attn_a_prompt_kernel(q_ref, k_ref, v_ref, lq1_ref, lk1_ref, lq2_ref, lk2_ref, sg_ref, o_ref,
                          s_ref, m_ref, l_ref, acc_ref, *, tq, lam_init):
    qi = pl.program_id(2)
    qq = _stack_maps(q_ref[...])

    def lane_fold(op, x, acc):
        for c in range(x.shape[1] // LANES):
            acc = op(acc, x[:, c * LANES:(c + 1) * LANES])
        return acc

    def score_block(kb, masked):
        start = pl.multiple_of(kb * tq, tq)
        s = _dot_nt(qq, k_ref[pl.ds(start, tq), :].astype(BF16))
        if masked:
            q_chunk = (lax.broadcasted_iota(jnp.int32, (2 * tq, 1), 0) % tq) // CHUNK
            k_chunk = lax.broadcasted_iota(jnp.int32, (1, tq), 1) // CHUNK
            s = jnp.where(k_chunk <= q_chunk, s, -jnp.inf)
        s_ref[kb] = s
        m_ref[...] = lane_fold(jnp.maximum, s, m_ref[...])

    def score_body(kb, carry):
        score_block(kb, False)
        return carry

    m_ref[...] = jnp.full(m_ref.shape, -jnp.inf, F32)
    lax.fori_loop(0, qi, score_body, 0)
    score_block(qi, True)
    m = jnp.max(m_ref[...], axis=1, keepdims=True)

    def value_body(kb, carry):
        start = pl.multiple_of(kb * tq, tq)
        p = jnp.exp(s_ref[kb] - m)
        l_ref[...] = lane_fold(jnp.add, p, l_ref[...])
        acc_ref[...] += _dot(p.astype(BF16), v_ref[pl.ds(start, tq), :].astype(BF16))
        return carry

    l_ref[...] = jnp.zeros(l_ref.shape, F32)
    acc_ref[...] = jnp.zeros(acc_ref.shape, F32)
    lax.fori_loop(0, qi + 1, value_body, 0)
    l = jnp.sum(l_ref[...], axis=1, keepdims=True)
    lam = _diff_lambda(lq1_ref, lk1_ref, lq2_ref, lk2_ref, lam_init)
    o_ref[...] = _diff_finish(l, acc_ref[...], tq, lam, sg_ref[...], lam_init).astype(o_ref.dtype)


def _lam_specs():
    spec = pl.BlockSpec((1, D_HALF), lambda *_: (0, 0))
    return [spec, spec, spec, spec, pl.BlockSpec((1, HEAD_DIM), lambda *_: (0, 0))]


def _lam_args(lq1, lk1, lq2, lk2, subln):
    return (lq1.reshape(1, -1), lk1.reshape(1, -1), lq2.reshape(1, -1), lk2.reshape(1, -1),
            subln.reshape(1, -1))


def _attn_a_prompt(q16, k32, v32, lam_args, *, layer, batch, seq, heads):
    n = q16.shape[0]
    tq = ATTN_A_TQ
    assert seq % tq == 0 and tq % CHUNK == 0
    nq = seq // tq
    kv_spec = pl.BlockSpec((None, seq, HEAD_DIM), lambda b, h, qi: (layer, b, h))
    return pl.pallas_call(
        functools.partial(_attn_a_prompt_kernel, tq=tq, lam_init=_lambda_init(layer)),
        grid=(batch, heads, nq),
        in_specs=[pl.BlockSpec((tq, HEAD_DIM), lambda b, h, qi: (b * nq + qi, h)), kv_spec, kv_spec]
                 + _lam_specs(),
        out_specs=pl.BlockSpec((tq, HEAD_DIM), lambda b, h, qi: (b * nq + qi, h)),
        out_shape=jax.ShapeDtypeStruct((n, heads * HEAD_DIM), BF16),
        scratch_shapes=[pltpu.VMEM((nq, 2 * tq, tq), F32),
                        pltpu.VMEM((2 * tq, LANES), F32), pltpu.VMEM((2 * tq, LANES), F32),
                        pltpu.VMEM((2 * tq, HEAD_DIM), F32)],
        compiler_params=_params(("arbitrary", "arbitrary", "arbitrary")),
        name="attn_a_prompt",
    )(q16, k32, v32, *lam_args)


def _attn_a_sample_kernel(q_ref, ck_ref, cv_ref, kn_ref, vn_ref, lq1_ref, lk1_ref, lq2_ref, lk2_ref, sg_ref,
                          o_ref, qq_ref, s_ref, sn_ref, m_ref, mf_ref, l_ref, acc_ref, *,
                          heads, new_len, lam_init):
    phase = pl.program_id(1)
    kt = pl.program_id(2)
    nkt = pl.num_programs(2)

    def lane_fold(op, x, acc):
        for c in range(x.shape[1] // LANES):
            acc = op(acc, x[:, c * LANES:(c + 1) * LANES])
        return acc

    @pl.when(jnp.logical_and(phase == 0, kt == 0))
    def _():
        for h in range(heads):
            sl = slice(h * HEAD_DIM, (h + 1) * HEAD_DIM)
            qq = _stack_maps(q_ref[:, sl])
            qq_ref[h] = qq
            s_new = _dot_nt(qq, kn_ref[:, sl].astype(BF16))
            sn_ref[h] = s_new
            m_ref[h] = jnp.broadcast_to(jnp.max(s_new, axis=1, keepdims=True), m_ref.shape[1:])

    @pl.when(phase == 0)
    def _():
        for h in range(heads):
            s = _dot_nt(qq_ref[h], _head_rows(ck_ref, h, heads).astype(BF16))
            s_ref[h, kt] = s
            m_ref[h] = lane_fold(jnp.maximum, s, m_ref[h])

    @pl.when(jnp.logical_and(phase == 1, kt == 0))
    def _():
        lane = lax.broadcasted_iota(jnp.int32, l_ref.shape[1:], 1)
        for h in range(heads):
            sl = slice(h * HEAD_DIM, (h + 1) * HEAD_DIM)
            m = jnp.max(m_ref[h], axis=1, keepdims=True)
            mf_ref[h] = m
            p_new = jnp.exp(sn_ref[h] - m)
            l_ref[h] = jnp.where(lane == 0, jnp.sum(p_new, axis=1, keepdims=True), 0.0)
            acc_ref[h] = _dot(p_new.astype(BF16), vn_ref[:, sl].astype(BF16))

    @pl.when(phase == 1)
    def _():
        for h in range(heads):
            p = jnp.exp(s_ref[h, kt] - mf_ref[h])
            l_ref[h] = lane_fold(jnp.add, p, l_ref[h])
            acc_ref[h] += _dot(p.astype(BF16), _head_rows(cv_ref, h, heads).astype(BF16))

    @pl.when(jnp.logical_and(phase == 1, kt == nkt - 1))
    def _():
        lam = _diff_lambda(lq1_ref, lk1_ref, lq2_ref, lk2_ref, lam_init)
        for h in range(heads):
            l = jnp.sum(l_ref[h], axis=1, keepdims=True)
            y = _diff_finish(l, acc_ref[h], new_len, lam, sg_ref[...], lam_init)
            o_ref[:, h * HEAD_DIM:(h + 1) * HEAD_DIM] = y.astype(o_ref.dtype)


def _attn_a_sample(q16, cache_k, cache_v, k32, v32, lam_args, *, layer, streams, new_len, heads):
    past = cache_k.shape[2] // heads
    tk = min(SAMPLE_A_TK, past)
    assert past % tk == 0
    nkt = past // tk
    width = heads * HEAD_DIM
    q_spec = pl.BlockSpec((new_len, width), lambda b, p, kt: (b, 0))
    new_spec = pl.BlockSpec((None, new_len, width), lambda b, p, kt: (layer, b, 0))
    k_spec = pl.BlockSpec((None, None, tk * heads, HEAD_DIM),
                          lambda b, p, kt: (layer, b, jnp.where(p == 0, kt, nkt - 1), 0))
    v_spec = pl.BlockSpec((None, None, tk * heads, HEAD_DIM),
                          lambda b, p, kt: (layer, b, jnp.where(p == 0, 0, kt), 0))
    rows = 2 * new_len
    return pl.pallas_call(
        functools.partial(_attn_a_sample_kernel, heads=heads, new_len=new_len,
                          lam_init=_lambda_init(layer)),
        grid=(streams, 2, nkt),
        in_specs=[q_spec, k_spec, v_spec, new_spec, new_spec] + _lam_specs(),
        out_specs=q_spec,
        out_shape=jax.ShapeDtypeStruct(q16.shape, BF16),
        scratch_shapes=[pltpu.VMEM((heads, rows, HEAD_DIM), BF16),
                        pltpu.VMEM((heads, nkt, rows, tk), F32),
                        pltpu.VMEM((heads, rows, new_len), F32),
                        pltpu.VMEM((heads, rows, LANES), F32),
                        pltpu.VMEM((heads, rows, 1), F32),
                        pltpu.VMEM((heads, rows, LANES), F32),
                        pltpu.VMEM((heads, rows, HEAD_DIM), F32)],
        compiler_params=_params(("arbitrary", "arbitrary", "arbitrary")),
        name="attn_a_sample",
    )(q16, cache_k, cache_v, k32, v32, *lam_args)


def _upper_ones(t):
    r = lax.broadcasted_iota(jnp.int32, (t, t), 0)
    c = lax.broadcasted_iota(jnp.int32, (t, t), 1)
    return (r > c).astype(BF16)


def _sb_block(q, kblk, vblk, ones_u, c, acc, scale, strict):
    z = _dot_nt(q, kblk) * scale
    sp = _softplus(z)
    log_stay = -sp
    if strict is not None:
        log_stay = jnp.where(strict, log_stay, 0.0)
    between = _dot_f32acc(log_stay, ones_u)
    w = jnp.exp((z - sp) + between + c)
    if strict is not None:
        w = jnp.where(strict, w, 0.0)
    acc = acc + _dot(w.astype(BF16), vblk)
    c = c + jnp.sum(log_stay, axis=1, keepdims=True)
    return c, acc


def _attn_b_prompt_kernel(q_ref, k_ref, v_ref, o_ref, c_ref, acc_ref, *, tq, tk, heads):
    h = pl.program_id(1)
    qi = pl.program_id(2)
    q = q_ref[...]
    scale = HEAD_DIM ** -0.5
    ones_u = _upper_ones(tk)
    per_q = tq // tk
    c_ref[...] = jnp.zeros(c_ref.shape, F32)
    acc_ref[...] = jnp.zeros(acc_ref.shape, F32)

    def block(kb, masked):
        start = pl.multiple_of(kb * tk, tk)
        strict = None
        if masked:
            row = lax.broadcasted_iota(jnp.int32, (tq, 1), 0)
            col = lax.broadcasted_iota(jnp.int32, (1, tk), 1)
            strict = start + col < qi * tq + row
        rows = pl.ds(start * heads + h, tk, stride=heads)
        c, acc = _sb_block(q, k_ref[rows, :].astype(BF16), v_ref[rows, :].astype(BF16),
                           ones_u, c_ref[...], acc_ref[...], scale, strict)
        c_ref[...] = c
        acc_ref[...] = acc

    for d in reversed(range(per_q)):
        block(qi * per_q + d, True)

    def body(t, carry):
        block(qi * per_q - 1 - t, False)
        return carry

    lax.fori_loop(0, qi * per_q, body, 0)
    o_ref[...] = acc_ref[...].astype(o_ref.dtype)


def _attn_b_prompt(q16, k32, v32, *, layer, batch, seq, heads):
    n = q16.shape[0]
    tq, tk = ATTN_B_TQ, SB_SUB
    assert seq % tq == 0 and tq % tk == 0
    nq = seq // tq
    kv_spec = pl.BlockSpec((None, seq * heads, HEAD_DIM), lambda b, h, qi: (layer, b, 0))
    return pl.pallas_call(
        functools.partial(_attn_b_prompt_kernel, tq=tq, tk=tk, heads=heads),
        grid=(batch, heads, nq),
        in_specs=[pl.BlockSpec((tq, HEAD_DIM), lambda b, h, qi: (b * nq + qi, h)), kv_spec, kv_spec],
        out_specs=pl.BlockSpec((tq, HEAD_DIM), lambda b, h, qi: (b * nq + qi, h)),
        out_shape=jax.ShapeDtypeStruct((n, heads * HEAD_DIM), BF16),
        scratch_shapes=[pltpu.VMEM((tq, 1), F32), pltpu.VMEM((tq, HEAD_DIM), F32)],
        compiler_params=_params(("arbitrary", "arbitrary", "arbitrary")),
        name="attn_b_prompt",
    )(q16, k32, v32)


def _attn_b_sample_kernel(q_ref, ck_ref, cv_ref, kn_ref, vn_ref, o_ref, u_ref, c_ref, acc_ref, *,
                          heads, new_len):
    b = pl.program_id(0)
    kt = pl.program_id(1)
    nkt = pl.num_programs(1)
    scale = HEAD_DIM ** -0.5
    tk = ck_ref.shape[0] // heads
    head_cols = lambda h: slice(h * HEAD_DIM, (h + 1) * HEAD_DIM)
    head_rows = lambda h: slice(h * new_len, (h + 1) * new_len)

    @pl.when(jnp.logical_and(b == 0, kt == 0))
    def _():
        u_ref[...] = _upper_ones(tk)

    @pl.when(kt == 0)
    def _():
        ones_new = _upper_ones(new_len)
        row = lax.broadcasted_iota(jnp.int32, (new_len, new_len), 0)
        col = lax.broadcasted_iota(jnp.int32, (new_len, new_len), 1)
        for h in range(heads):
            sl = head_cols(h)
            c, acc = _sb_block(q_ref[:, sl], _head_rows(kn_ref, h, heads).astype(BF16),
                               _head_rows(vn_ref, h, heads).astype(BF16), ones_new,
                               jnp.zeros((new_len, 1), F32), jnp.zeros((new_len, HEAD_DIM), F32),
                               scale, col < row)
            c_ref[head_rows(h), :] = c
            acc_ref[h] = acc

    z = jnp.concatenate([_dot_nt(q_ref[:, head_cols(h)], _head_rows(ck_ref, h, heads).astype(BF16))
                         for h in range(heads)], axis=0) * scale
    sp = _softplus(z)
    log_stay = -sp
    between = _dot_f32acc(log_stay, u_ref[...])
    w = jnp.exp((z - sp) + between + c_ref[...]).astype(BF16)
    for h in range(heads):
        acc_ref[h] += _dot(w[head_rows(h), :], _head_rows(cv_ref, h, heads).astype(BF16))
    c_ref[...] += jnp.sum(log_stay, axis=1, keepdims=True)

    @pl.when(kt == nkt - 1)
    def _():
        for h in range(heads):
            o_ref[:, head_cols(h)] = acc_ref[h].astype(o_ref.dtype)


def _attn_b_sample(q16, cache_k, cache_v, k32, v32, *, layer, streams, new_len, heads):
    past = cache_k.shape[2] // heads
    tk = SAMPLE_TK
    nkt = past // tk
    assert past % tk == 0
    width = heads * HEAD_DIM
    q_spec = pl.BlockSpec((new_len, width), lambda b, kt: (b, 0))
    new_spec = pl.BlockSpec((None, new_len * heads, HEAD_DIM), lambda b, kt: (layer, b, 0))
    cache_spec = pl.BlockSpec((None, None, tk * heads, HEAD_DIM), lambda b, kt: (layer, b, nkt - 1 - kt, 0))
    return pl.pallas_call(
        functools.partial(_attn_b_sample_kernel, heads=heads, new_len=new_len),
        grid=(streams, nkt),
        in_specs=[q_spec, cache_spec, cache_spec, new_spec, new_spec],
        out_specs=q_spec,
        out_shape=jax.ShapeDtypeStruct(q16.shape, BF16),
        scratch_shapes=[pltpu.VMEM((tk, tk), BF16),
                        pltpu.VMEM((heads * new_len, 1), F32),
                        pltpu.VMEM((heads, new_len, HEAD_DIM), F32)],
        compiler_params=_params(("arbitrary", "arbitrary")),
        name="attn_b_sample",
    )(q16, cache_k, cache_v, k32, v32)


def _attn_c_prompt_kernel(q_ref, k_ref, v_ref, bgrp_ref, o_ref, k16_ref, v16_ref, *, seq, heads):
    k16_ref[0:C_WIN, :] = jnp.zeros((C_WIN, HEAD_DIM), BF16)
    v16_ref[0:C_WIN, :] = jnp.zeros((C_WIN, HEAD_DIM), BF16)
    own = pl.ds(pl.program_id(1), seq, stride=heads)
    k16_ref[C_WIN:, :] = k_ref[own, :].astype(BF16)
    v16_ref[C_WIN:, :] = v_ref[own, :].astype(BF16)
    scale = HEAD_DIM ** -0.5
    key_chunk = lax.broadcasted_iota(jnp.int32, (1, BAND_KEYS), 1) // CHUNK

    def group(gi):
        r0 = pl.multiple_of(gi * BAND_ROWS, BAND_ROWS)
        s = _dot_nt(q_ref[pl.ds(r0, BAND_ROWS), :], k16_ref[pl.ds(r0, BAND_KEYS), :]) * scale + bgrp_ref[0]
        s = jnp.where(key_chunk + gi * BAND_GROUP >= C_BAND_CHUNKS, s, -jnp.inf)
        m = jnp.max(s, axis=1, keepdims=True)
        p = jnp.exp(s - m)
        l = jnp.sum(p, axis=1, keepdims=True)
        o = _dot(p.astype(BF16), v16_ref[pl.ds(r0, BAND_KEYS), :]) / l
        o_ref[pl.ds(r0, BAND_ROWS), :] = o.astype(o_ref.dtype)

    def body(t, carry):
        group(2 * t)
        group(2 * t + 1)
        return carry

    lax.fori_loop(0, seq // BAND_ROWS // 2, body, 0)


def _attn_c_prompt(q16, k32, v32, bias_grp, *, layer, batch, seq, heads):
    n = q16.shape[0]
    assert seq % (2 * BAND_ROWS) == 0
    kv_spec = pl.BlockSpec((None, seq * heads, HEAD_DIM), lambda b, h: (layer, b, 0))
    return pl.pallas_call(
        functools.partial(_attn_c_prompt_kernel, seq=seq, heads=heads),
        grid=(batch, heads),
        in_specs=[pl.BlockSpec((seq, HEAD_DIM), lambda b, h: (b, h)), kv_spec, kv_spec,
                  pl.BlockSpec((1, BAND_ROWS, BAND_KEYS), lambda b, h: (layer * heads + h, 0, 0))],
        out_specs=pl.BlockSpec((seq, HEAD_DIM), lambda b, h: (b, h)),
        out_shape=jax.ShapeDtypeStruct((n, heads * HEAD_DIM), BF16),
        scratch_shapes=[pltpu.VMEM((seq + C_WIN, HEAD_DIM), BF16),
                        pltpu.VMEM((seq + C_WIN, HEAD_DIM), BF16)],
        compiler_params=_params(("arbitrary", "arbitrary")),
        name="attn_c_prompt",
    )(q16, k32, v32, bias_grp)


def _attn_c_sample_kernel(q_ref, ck_ref, cv_ref, kn_ref, vn_ref, bfar_ref, bown_ref, o_ref, *, heads):
    scale = HEAD_DIM ** -0.5
    for h in range(heads):
        sl = slice(h * HEAD_DIM, (h + 1) * HEAD_DIM)
        q = q_ref[:, sl]
        s_far = _dot_nt(q, _head_rows(ck_ref, h, heads).astype(BF16)) * scale + bfar_ref[h]
        s_own = _dot_nt(q, _head_rows(kn_ref, h, heads).astype(BF16)) * scale + bown_ref[h]
        m = jnp.maximum(jnp.max(s_far, axis=1, keepdims=True), jnp.max(s_own, axis=1, keepdims=True))
        p_far = jnp.exp(s_far - m)
        p_own = jnp.exp(s_own - m)
        l = jnp.sum(p_far, axis=1, keepdims=True) + jnp.sum(p_own, axis=1, keepdims=True)
        o = (_dot(p_far.astype(BF16), _head_rows(cv_ref, h, heads).astype(BF16))
             + _dot(p_own.astype(BF16), _head_rows(vn_ref, h, heads).astype(BF16))) / l
        o_ref[:, sl] = o.astype(o_ref.dtype)


def _attn_c_sample(q16, cache_k, cache_v, k32, v32, bias_far, bias_own, *, layer, streams, new_len, heads):
    assert cache_k.shape[2] == C_WIN * heads and new_len == CHUNK
    width = heads * HEAD_DIM
    q_spec = pl.BlockSpec((new_len, width), lambda b: (b, 0))
    new_spec = pl.BlockSpec((None, new_len * heads, HEAD_DIM), lambda b: (layer, b, 0))
    cache_spec = pl.BlockSpec((None, None, C_WIN * heads, HEAD_DIM), lambda b: (layer, b, 0, 0))
    return pl.pallas_call(
        functools.partial(_attn_c_sample_kernel, heads=heads),
        grid=(streams,),
        in_specs=[q_spec, cache_spec, cache_spec, new_spec, new_spec,
                  pl.BlockSpec((heads, CHUNK, C_WIN), lambda b: (layer, 0, 0)),
                  pl.BlockSpec((heads, CHUNK, CHUNK), lambda b: (layer, 0, 0))],
        out_specs=q_spec,
        out_shape=jax.ShapeDtypeStruct(q16.shape, BF16),
        compiler_params=_params(("arbitrary",)),
        name="attn_c_sample",
    )(q16, cache_k, cache_v, k32, v32, bias_far, bias_own)


def _merge_kernel(x_ref, oa_ref, ob_ref, oc_ref, g_ref, wb_ref, wo_ref, o_ref, *, w_a, w_b, d_model):
    ya = _dot(oa_ref[...], wb_ref[0:w_a, :])
    yb = _dot(ob_ref[...], wb_ref[w_a:w_a + w_b, :])
    yc = _dot(oc_ref[...], wb_ref[w_a + w_b:, :])
    m = (g_ref[:, 0:d_model].astype(F32) * ya + g_ref[:, d_model:2 * d_model].astype(F32) * yb
         + g_ref[:, 2 * d_model:].astype(F32) * yc)
    o_ref[...] = x_ref[...] + _dot(m.astype(BF16), wo_ref[...])


def _merge(x, oa, ob, oc, g, wb16, wo16, *, layer):
    n, d_model = x.shape
    tm = MERGE_TM
    assert n % tm == 0
    w_a, w_b, w_c = oa.shape[1], ob.shape[1], oc.shape[1]
    rows = lambda width: pl.BlockSpec((tm, width), lambda i: (i, 0))
    resident = lambda shape: pl.BlockSpec((None,) + shape[1:], lambda i: (layer, 0, 0),
                                          pipeline_mode=pl.Buffered(1))
    return pl.pallas_call(
        functools.partial(_merge_kernel, w_a=w_a, w_b=w_b, d_model=d_model),
        grid=(n // tm,),
        in_specs=[rows(d_model), rows(w_a), rows(w_b), rows(w_c), rows(N_BRANCH * d_model),
                  resident(wb16.shape), resident(wo16.shape)],
        out_specs=rows(d_model),
        out_shape=jax.ShapeDtypeStruct((n, d_model), F32),
        compiler_params=_params(("arbitrary",)),
        name="merge",
    )(x, oa, ob, oc, g, wb16, wo16)


_R_E1, _R_E2, _R_C1, _R_C2, _R_RANK1, _R_RANK2 = range(6)


def _route_kernel(xp_ref, xs_ref, ng_ref, wr_ref, br_ref, xn_ref, route_ref, cnt_ref, carry_ref, *,
                  prompt_tiles):
    i = pl.program_id(0)
    tm = xp_ref.shape[0]
    lane = lax.broadcasted_iota(jnp.int32, (tm, LANES), 1)
    lane_f = lane.astype(F32)

    @pl.when(i == 0)
    def _():
        carry_ref[...] = jnp.zeros(carry_ref.shape, F32)

    x = jnp.where(i < prompt_tiles, xp_ref[...], xs_ref[...])
    ms = jnp.mean(x * x, axis=-1, keepdims=True)
    xn = x * lax.rsqrt(ms + EPS) * ng_ref[...]
    xn_ref[...] = xn

    x_hi, x_lo = _split_bf16(xn)
    w_hi, w_lo = _split_bf16(wr_ref[...])
    logits = _dot(x_hi, w_hi) + _dot(x_lo, w_hi) + _dot(x_hi, w_lo) + br_ref[...]
    is_group = lane < N_GROUPS
    gl = jnp.where(is_group, logits, -jnp.inf)
    g_max = jnp.max(gl, axis=1, keepdims=True)
    g_idx = jnp.min(jnp.where(gl == g_max, lane_f, float(LANES)), axis=1, keepdims=True)
    p_group = 1.0 / jnp.sum(jnp.where(is_group, jnp.exp(logits - g_max), 0.0), axis=1, keepdims=True)
    in_group = jnp.logical_and(lane >= N_GROUPS, lane < N_GROUPS + N_EXP)
    in_group = jnp.logical_and(in_group, ((lane - N_GROUPS) // E_PER_GROUP).astype(F32) == g_idx)
    el = jnp.where(in_group, logits, -jnp.inf)
    v1 = jnp.max(el, axis=1, keepdims=True)
    i1 = jnp.min(jnp.where(el == v1, lane_f, float(LANES)), axis=1, keepdims=True)
    el2 = jnp.where(lane_f == i1, -jnp.inf, el)
    v2 = jnp.max(el2, axis=1, keepdims=True)
    i2 = jnp.min(jnp.where(el2 == v2, lane_f, float(LANES)), axis=1, keepdims=True)
    t = jnp.exp(v2 - v1)
    c1 = p_group / (1.0 + t)
    c2 = p_group * t / (1.0 + t)
    e1 = i1 - N_GROUPS
    e2 = i2 - N_GROUPS

    a1 = (lane_f == e1).astype(F32)
    a2 = (lane_f == e2).astype(F32)
    a = a1 + a2
    row = lax.broadcasted_iota(jnp.int32, (tm, tm), 0)
    col = lax.broadcasted_iota(jnp.int32, (tm, tm), 1)
    earlier = (col < row).astype(BF16)
    before = _dot(earlier, a.astype(BF16)) + carry_ref[...]
    r1 = jnp.sum(a1 * before, axis=1, keepdims=True)
    r2 = jnp.sum(a2 * before, axis=1, keepdims=True)
    carry_ref[...] += jnp.sum(a, axis=0, keepdims=True)
    cnt_ref[...] = carry_ref[...]

    record = jnp.zeros((tm, LANES), F32)
    for slot, val in ((_R_E1, e1), (_R_E2, e2), (_R_C1, c1), (_R_C2, c2), (_R_RANK1, r1), (_R_RANK2, r2)):
        record = jnp.where(lane == slot, val, record)
    route_ref[...] = record


def _route(xp, xs, norm_g, w_router, b_router):
    tm = MOE_ROUTE_TM
    (n_p, d_model), n_s = xp.shape, xs.shape[0]
    assert n_p % tm == 0 and n_s % tm == 0
    pt, st = n_p // tm, n_s // tm
    n = n_p + n_s
    const = lambda shape: pl.BlockSpec(shape, lambda i: (0, 0))
    return pl.pallas_call(
        functools.partial(_route_kernel, prompt_tiles=pt),
        grid=(pt + st,),
        in_specs=[pl.BlockSpec((tm, d_model), lambda i: (jnp.minimum(i, pt - 1), 0)),
                  pl.BlockSpec((tm, d_model), lambda i: (jnp.maximum(i - pt, 0), 0)),
                  const((1, d_model)), const((d_model, LANES)), const((1, LANES))],
        out_specs=[pl.BlockSpec((tm, d_model), lambda i: (i, 0)),
                   pl.BlockSpec((tm, LANES), lambda i: (i, 0)),
                   const((1, LANES))],
        out_shape=[jax.ShapeDtypeStruct((n, d_model), F32), jax.ShapeDtypeStruct((n, LANES), F32),
                   jax.ShapeDtypeStruct((1, LANES), F32)],
        scratch_shapes=[pltpu.VMEM((1, LANES), F32)],
        compiler_params=_params(("arbitrary",)),
        name="moe_route",
    )(xp, xs, norm_g.reshape(1, -1), w_router, b_router)


def _row_copy(src, src_row, dst, dst_row, sem):
    return pltpu.make_async_copy(src.at[pl.ds(src_row, 1), :], dst.at[pl.ds(dst_row, 1), :], sem)


def _row_token_kernel(pos1_ref, pos2_ref, out_ref, *, n_tokens):
    def init(r, carry):
        out_ref[r] = 0
        return carry

    def place(tok, carry):
        out_ref[pos1_ref[tok]] = tok
        out_ref[pos2_ref[tok]] = tok
        return carry

    lax.fori_loop(0, out_ref.shape[0], init, 0, unroll=8)
    lax.fori_loop(0, n_tokens, place, 0, unroll=8)


def _row_tokens(pos1, pos2, rows):
    return pl.pallas_call(
        functools.partial(_row_token_kernel, n_tokens=pos1.shape[0]),
        grid_spec=pltpu.PrefetchScalarGridSpec(
            num_scalar_prefetch=2, grid=(1,), in_specs=[],
            out_specs=pl.BlockSpec(memory_space=pltpu.SMEM)),
        out_shape=jax.ShapeDtypeStruct((rows,), jnp.int32),
        compiler_params=_params(("arbitrary",)),
        name="moe_row_tokens",
    )(pos1, pos2)


def _expert_kernel(tile_expert_ref, n_tiles_ref, row_token_ref, xn_hbm, wg_ref, wu_ref, wd_ref, y_ref,
                   xbuf, sem, *, tile):
    del tile_expert_ref
    t = pl.program_id(0)
    n_valid = n_tiles_ref[0]
    slot = t % 2

    def gather(tile_idx, into):
        base = tile_idx * tile

        def body(g, carry):
            r0 = pl.multiple_of(g * ROW_GROUP, ROW_GROUP)
            for k in range(ROW_GROUP):
                _row_copy(xn_hbm, row_token_ref[base + r0 + k], xbuf.at[into], r0 + k,
                          sem.at[into]).start(priority=k % 2)
            return carry

        lax.fori_loop(0, tile // ROW_GROUP, body, 0)

    def wait_rows(into):
        def body(r, carry):
            _row_copy(xn_hbm, 0, xbuf.at[into], 0, sem.at[into]).wait()
            return carry

        lax.fori_loop(0, tile, body, 0, unroll=8)

    @pl.when(jnp.logical_and(t == 0, n_valid > 0))
    def _():
        gather(0, 0)

    @pl.when(t + 1 < n_valid)
    def _():
        gather(t + 1, 1 - slot)

    @pl.when(t < n_valid)
    def _():
        wait_rows(slot)
        x = xbuf[slot].astype(BF16)
        h = jax.nn.silu(_dot(x, wg_ref[...].astype(BF16))) * _dot(x, wu_ref[...].astype(BF16))
        y_ref[...] = _dot(h.astype(BF16), wd_ref[...].astype(BF16))

    @pl.when(t >= n_valid)
    def _():
        y_ref[...] = jnp.zeros(y_ref.shape, y_ref.dtype)


def _experts(tile_expert, n_tiles, row_token, xn, w_gate, w_up, w_down, *, layer):
    tile = MOE_ROW_TILE
    rows, d_model = row_token.shape[0], xn.shape[1]
    d_exp = w_gate.shape[-1]
    w_in = pl.BlockSpec((None, None, d_model, d_exp), lambda t, te, nt, rt: (layer, te[t], 0, 0))
    w_out = pl.BlockSpec((None, None, d_exp, d_model), lambda t, te, nt, rt: (layer, te[t], 0, 0))
    return pl.pallas_call(
        functools.partial(_expert_kernel, tile=tile),
        grid_spec=pltpu.PrefetchScalarGridSpec(
            num_scalar_prefetch=3, grid=(rows // tile,),
            in_specs=[pl.BlockSpec(memory_space=pl.ANY), w_in, w_in, w_out],
            out_specs=pl.BlockSpec((tile, d_model), lambda t, te, nt, rt: (t, 0)),
            scratch_shapes=[pltpu.VMEM((2, tile, d_model), F32), pltpu.SemaphoreType.DMA((2,))]),
        out_shape=jax.ShapeDtypeStruct((rows, d_model), F32),
        compiler_params=_params(("arbitrary",)),
        name="moe_experts",
    )(tile_expert, n_tiles, row_token, xn, w_gate, w_up, w_down)


def _combine_kernel(pos1_ref, pos2_ref, xp_ref, xs_ref, route_ref, y_hbm, op_ref, os_ref, ybuf, sem, *,
                    tile, prompt_tiles):
    i = pl.program_id(0)
    base = i * tile

    for r in range(tile):
        _row_copy(y_hbm, pos1_ref[base + r], ybuf.at[0], r, sem).start(priority=0)
        _row_copy(y_hbm, pos2_ref[base + r], ybuf.at[1], r, sem).start(priority=1)

    def wait(r, carry):
        _row_copy(y_hbm, 0, ybuf.at[0], 0, sem).wait()
        _row_copy(y_hbm, 0, ybuf.at[1], 0, sem).wait()
        return carry

    lax.fori_loop(0, tile, wait, 0, unroll=8)
    y = route_ref[:, _R_C1:_R_C1 + 1] * ybuf[0] + route_ref[:, _R_C2:_R_C2 + 1] * ybuf[1]

    @pl.when(i < prompt_tiles)
    def _():
        op_ref[...] = xp_ref[...] + y

    @pl.when(i >= prompt_tiles)
    def _():
        os_ref[...] = xs_ref[...] + y


def _combine(pos1, pos2, xp, xs, route, y_sorted):
    tile = MOE_TOK_TILE
    (n_p, d_model), n_s = xp.shape, xs.shape[0]
    assert n_p % tile == 0 and n_s % tile == 0
    pt, st = n_p // tile, n_s // tile
    p_spec = pl.BlockSpec((tile, d_model), lambda i, p1, p2: (jnp.minimum(i, pt - 1), 0))
    s_spec = pl.BlockSpec((tile, d_model), lambda i, p1, p2: (jnp.maximum(i - pt, 0), 0))
    return pl.pallas_call(
        functools.partial(_combine_kernel, tile=tile, prompt_tiles=pt),
        grid_spec=pltpu.PrefetchScalarGridSpec(
            num_scalar_prefetch=2, grid=(pt + st,),
            in_specs=[p_spec, s_spec, pl.BlockSpec((tile, LANES), lambda i, p1, p2: (i, 0)),
                      pl.BlockSpec(memory_space=pl.ANY)],
            out_specs=[p_spec, s_spec],
            scratch_shapes=[pltpu.VMEM((2, tile, d_model), F32), pltpu.SemaphoreType.DMA(())]),
        out_shape=[jax.ShapeDtypeStruct(xp.shape, F32), jax.ShapeDtypeStruct(xs.shape, F32)],
        compiler_params=_params(("arbitrary",)),
        name="moe_combine",
    )(pos1, pos2, xp, xs, route, y_sorted)


def _moe(xp, xs, norm_g, w_router, b_router, w_gate, w_up, w_down, *, layer):
    n = xp.shape[0] + xs.shape[0]
    tile = MOE_ROW_TILE
    n_exp = w_gate.shape[1]
    row_tiles = (2 * n) // tile + n_exp
    xn, route, counts = _route(xp, xs, norm_g, w_router, b_router)

    counts = counts[0, :n_exp].astype(jnp.int32)
    padded = ((counts + tile - 1) // tile) * tile
    upto = jnp.arange(n_exp)[None, :] <= jnp.arange(n_exp)[:, None]
    ends = jnp.sum(jnp.where(upto, padded[None, :], 0), axis=1)
    offsets = ends - padded
    as_int = lambda lane_idx: route[:, lane_idx].astype(jnp.int32)
    pos1 = offsets[as_int(_R_E1)] + as_int(_R_RANK1)
    pos2 = offsets[as_int(_R_E2)] + as_int(_R_RANK2)
    tile_start = jnp.arange(row_tiles, dtype=jnp.int32) * tile
    tile_expert = jnp.minimum(jnp.sum((tile_start[:, None] >= ends[None, :]).astype(jnp.int32), axis=1),
                              n_exp - 1)
    n_tiles = ends[-1:] // tile

    row_token = _row_tokens(pos1, pos2, row_tiles * tile)
    y_sorted = _experts(tile_expert, n_tiles, row_token, xn, w_gate, w_up, w_down, layer=layer)
    return _combine(pos1, pos2, xp, xs, route, y_sorted)


def _rope_tables(pos):
    half = D_HALF // 2
    inv = ROPE_THETA ** (-2.0 * jnp.arange(half, dtype=F32) / D_HALF)
    ang = pos.astype(F32)[:, None] * inv[None, :]
    cos, sin = jnp.cos(ang), jnp.sin(ang)
    cos_full = jnp.tile(cos, (1, HEAD_DIM // half))
    sin_signed = jnp.tile(jnp.concatenate([-sin, sin], axis=1), (1, HEAD_DIM // D_HALF))
    return cos_full, sin_signed


def kernel(x_prompt, x_sample, cache_a_k, cache_a_v, cache_b_k, cache_b_v, cache_c_k, cache_c_v, norm_mix, w_in, b_gate, q_norm_a, k_norm_a, lam_q1, lam_k1, lam_q2, lam_k2, subln_a, q_norm_c, k_norm_c, rel_bias_c, w_branch, w_out, norm_ffn, w_group, b_group, w_expert_router, b_expert_router, w_gate_e, w_up_e, w_down_e):
    batch, seq, d_model = x_prompt.shape
    streams, new_len, _ = x_sample.shape
    depth, _, past, h_a, _ = cache_a_k.shape
    h_b, h_c = cache_b_k.shape[3], cache_c_k.shape[3]
    w_a, w_b, w_c = h_a * HEAD_DIM, h_b * HEAD_DIM, h_c * HEAD_DIM
    n_prompt, n_sample = batch * seq, streams * new_len
    tm = PROJ_TM
    assert seq % tm == 0 and n_sample % tm == 0 and tm % new_len == 0
    c_rows = min(C_WIN, seq)

    xp = x_prompt.reshape(n_prompt, d_model)
    xs = x_sample.reshape(n_sample, d_model)
    tab_p = _rope_tables(jnp.arange(seq))
    tab_s = _rope_tables(jnp.tile(past + jnp.arange(new_len), tm // new_len))
    bias_far, bias_own, bias_grp = _expand_bias(rel_bias_c.reshape(depth * h_c, N_REL))
    rows_view = lambda c: c.reshape(c.shape[0], c.shape[1], c.shape[2] * c.shape[3], c.shape[4])
    ca_k, ca_v, cb_k, cb_v, cc_k, cc_v = map(rows_view, (cache_a_k, cache_a_v, cache_b_k, cache_b_v,
                                                          cache_c_k, cache_c_v))

    w_in16, wb16, wo16 = w_in.astype(BF16), w_branch.astype(BF16), w_out.astype(BF16)
    kv_p = kv_s = None
    for l in range(depth):
        proj_w = (norm_mix[l], w_in16, b_gate[l], q_norm_a[l], k_norm_a[l], q_norm_c[l],
                  k_norm_c[l])
        proj_kw = dict(layer=l, depth=depth, w_a=w_a, w_b=w_b, w_c=w_c, d_model=d_model)
        qa_p, qb_p, qc_p, g_p, kv_p = _proj(xp, *proj_w, *tab_p, kv_p, **proj_kw)
        qa_s, qb_s, qc_s, g_s, kv_s = _proj(xs, *proj_w, *tab_s, kv_s, **proj_kw)
        lam_args = _lam_args(lam_q1[l], lam_k1[l], lam_q2[l], lam_k2[l], subln_a[l])

        oa_p = _attn_a_prompt(qa_p, kv_p[0], kv_p[1], lam_args, layer=l, batch=batch, seq=seq, heads=h_a)
        ob_p = _attn_b_prompt(qb_p, kv_p[2], kv_p[3], layer=l, batch=batch, seq=seq, heads=h_b)
        oc_p = _attn_c_prompt(qc_p, kv_p[4], kv_p[5], bias_grp, layer=l, batch=batch, seq=seq, heads=h_c)
        sample_kw = dict(layer=l, streams=streams, new_len=new_len)
        oa_s = _attn_a_sample(qa_s, ca_k, ca_v, kv_s[0], kv_s[1], lam_args, heads=h_a, **sample_kw)
        ob_s = _attn_b_sample(qb_s, cb_k, cb_v, kv_s[2], kv_s[3], heads=h_b, **sample_kw)
        oc_s = _attn_c_sample(qc_s, cc_k, cc_v, kv_s[4], kv_s[5], bias_far, bias_own, heads=h_c, **sample_kw)

        xp = _merge(xp, oa_p, ob_p, oc_p, g_p, wb16, wo16, layer=l)
        xs = _merge(xs, oa_s, ob_s, oc_s, g_s, wb16, wo16, layer=l)

        w_router = jnp.concatenate(
            [w_group[l], jnp.transpose(w_expert_router[l], (1, 0, 2)).reshape(d_model, N_EXP)], axis=1)
        w_router = jnp.pad(w_router, ((0, 0), (0, LANES - w_router.shape[1])))
        b_router = jnp.pad(jnp.concatenate([b_group[l], b_expert_router[l].reshape(-1)]),
                           (0, LANES - N_GROUPS - N_EXP)).reshape(1, LANES)
        xp, xs = _moe(xp, xs, norm_ffn[l], w_router, b_router, w_gate_e, w_up_e, w_down_e, layer=l)

    heads_of = [h_a, h_a, h_b, h_b, h_c, h_c]
    prompt_kv = [t.reshape(depth, batch, seq, h, HEAD_DIM) for t, h in zip(kv_p, heads_of)]
    prompt_kv[4] = prompt_kv[4][:, :, seq - c_rows:]
    prompt_kv[5] = prompt_kv[5][:, :, seq - c_rows:]
    sample_kv = [t.reshape(depth, streams, new_len, h, HEAD_DIM) for t, h in zip(kv_s, heads_of)]
    return (xp.reshape(batch, seq, d_model), xs.reshape(streams, new_len, d_model), *prompt_kv, *sample_kv)
```
